```python
import math
import jax, jax.numpy as jnp
from jax import lax
import numpy as np

D_MODEL = 2048
BATCH = 4
SEQ = 2048
DEPTH = 2
DEC_BATCH = 128
DEC_SEQ = 4
PAST_LEN = 8192
PAGE_SIZE = 128

EPS = 1e-6
D_FF = 5632

ML_HEADS = 4
ML_DQK = D_MODEL // 16
ML_DV = D_MODEL // 8
ML_CHUNK = 64

MLA_HEADS = 8
MLA_NOPE = 128
MLA_ROPE = 64
MLA_V = 128
MLA_Q_RANK = D_MODEL // 4
MLA_KV_RANK = D_MODEL // 4
MLA_SCALE = (MLA_NOPE + MLA_ROPE) ** -0.5
ROPE_BASE = 10000.0
Q_BLOCK = 128

EVEN_SPLITS = (ML_HEADS * ML_DQK, ML_HEADS * ML_DQK, ML_HEADS * ML_DV, ML_HEADS, ML_HEADS,
               ML_HEADS * ML_DV, MLA_Q_RANK, MLA_KV_RANK, MLA_ROPE)
EVEN_IN = 2 * ML_HEADS * ML_DQK + 2 * ML_HEADS * ML_DV + 2 * ML_HEADS + MLA_Q_RANK + MLA_KV_RANK + MLA_ROPE
EVEN_OUT = ML_HEADS * ML_DV + MLA_HEADS * MLA_V

SSM_D_INNER = 2 * D_MODEL
SSM_HEADDIM = 64
SSM_HEADS = SSM_D_INNER // SSM_HEADDIM
SSM_GROUPS = 8
SSM_STATE = 128
SSM_CONV = 4
SSM_CHUNK = 64
SSM_CONV_DIM = SSM_D_INNER + 2 * SSM_GROUPS * SSM_STATE
SSM_IN = SSM_D_INNER + SSM_CONV_DIM + SSM_HEADS

kernel_name = 'hybrid_mlstm_mla_mamba2_macaron_step'


def rmsnorm(x, g):
    xf = x.astype(jnp.float32)
    y = xf * lax.rsqrt(jnp.mean(xf * xf, axis=-1, keepdims=True) + EPS)
    return (y * g.astype(jnp.float32)).astype(x.dtype)


def swiglu(x, w_gate, w_up, w_down):
    return (jax.nn.silu(x @ w_gate) * (x @ w_up)) @ w_down


def split_cols(a, sizes):
    idx = np.cumsum(np.array(sizes))[:-1].tolist()
    return jnp.split(a, idx, axis=-1)


def rope(x, pos):
    half = MLA_ROPE // 2
    inv = ROPE_BASE ** (-jnp.arange(half, dtype=jnp.float32) / half)
    ang = pos.astype(jnp.float32)[:, None] * inv[None, :]
    cos = jnp.cos(ang)[:, None, :]
    sin = jnp.sin(ang)[:, None, :]
    xf = x.astype(jnp.float32)
    x1, x2 = xf[..., :half], xf[..., half:]
    return jnp.concatenate([x1 * cos - x2 * sin, x1 * sin + x2 * cos], axis=-1).astype(x.dtype)


def mlstm_chunkwise(q, k, v, logi, logf, C0, n0, m0):
    bsz, L = q.shape[0], q.shape[1]
    lc = math.gcd(L, ML_CHUNK)
    nc = L // lc
    f32 = jnp.float32

    def chunks(a):
        return a.astype(f32).reshape((bsz, nc, lc) + a.shape[2:]).swapaxes(0, 1)

    causal = jnp.tril(jnp.ones((lc, lc), dtype=bool))

    def step(carry, inp):
        C, n, m = carry
        qt, kt, vt, it, ft = inp
        b = jnp.cumsum(ft, axis=1).transpose(0, 2, 1)
        ih = it.transpose(0, 2, 1)
        dmat = jnp.where(causal, b[..., :, None] - b[..., None, :] + ih[..., None, :], -jnp.inf)
        inter = b + m[..., None]
        m_t = jnp.maximum(inter, jnp.max(dmat, axis=-1))
        w_inter = jnp.exp(inter - m_t)
        s = jnp.einsum('bthd,bshd->bhts', qt, kt) * jnp.exp(dmat - m_t[..., None])
        num = jnp.einsum('bhts,bshv->bhtv', s, vt) + w_inter[..., None] * jnp.einsum('bhvd,bthd->bhtv', C, qt)
        den = jnp.sum(s, axis=-1) + w_inter * jnp.einsum('bhd,bthd->bht', n, qt)
        h = num / jnp.maximum(jnp.abs(den), jnp.exp(-m_t))[..., None]
        b_last = b[..., -1]
        g = b_last[..., None] - b + ih
        m_new = jnp.maximum(b_last + m, jnp.max(g, axis=-1))
        decay = jnp.exp(b_last + m - m_new)
        wg = jnp.exp(g - m_new[..., None])
        C = decay[..., None, None] * C + jnp.einsum('bhs,bshv,bshd->bhvd', wg, vt, kt)
        n = decay[..., None] * n + jnp.einsum('bhs,bshd->bhd', wg, kt)
        return (C, n, m_new), h.transpose(0, 2, 1, 3)

    init = (C0.astype(f32), n0.astype(f32), m0.astype(f32))
    (C, n, m), hs = lax.scan(step, init, (chunks(q), chunks(k), chunks(v), chunks(logi), chunks(logf)))
    return hs.swapaxes(0, 1).reshape(bsz, L, ML_HEADS, ML_DV), C, n, m


def ssd_chunked(x, dt, A, Bm, Cm, h0):
    bsz, L = x.shape[0], x.shape[1]
    G, HG = SSM_GROUPS, SSM_HEADS // SSM_GROUPS
    lc = math.gcd(L, SSM_CHUNK)
    nc = L // lc
    f32 = jnp.float32

    def chunks(a, tail):
        return a.astype(f32).reshape((bsz, nc, lc) + tail).swapaxes(0, 1)

    xs = chunks(x, (G, HG, SSM_HEADDIM))
    dts = chunks(dt, (G, HG))
    bs = chunks(Bm, (G, SSM_STATE))
    cs = chunks(Cm, (G, SSM_STATE))
    a_g = A.astype(f32).reshape(G, HG)
    causal = jnp.tril(jnp.ones((lc, lc), dtype=bool))[None, :, :, None, None]

    def step(h, inp):
        xt, dtt, bt, ct = inp
        cum = jnp.cumsum(dtt * a_g, axis=1)
        seg = jnp.where(causal, cum[:, :, None] - cum[:, None, :], -jnp.inf)
        w = jnp.einsum('btgn,bsgn->btsg', ct, bt)[..., None] * jnp.exp(seg) * dtt[:, None]
        y = (jnp.einsum('btsgh,bsghp->btghp', w, xt)
             + jnp.einsum('btgn,bghpn->btghp', ct, h) * jnp.exp(cum)[..., None])
        decay = jnp.exp(cum[:, -1:] - cum) * dtt
        h = jnp.exp(cum[:, -1])[..., None, None] * h + jnp.einsum('bsgh,bsgn,bsghp->bghpn', decay, bt, xt)
        return h, y

    h0g = h0.astype(f32).reshape(bsz, G, HG, SSM_HEADDIM, SSM_STATE)
    h, ys = lax.scan(step, h0g, (xs, dts, bs, cs))
    return (ys.swapaxes(0, 1).reshape(bsz, L, SSM_HEADS, SSM_HEADDIM),
            h.reshape(bsz, SSM_HEADS, SSM_HEADDIM, SSM_STATE))


def mla_attend_prompt(q_nope, q_rope, ckv, kr, w_ukv):
    bsz, S = q_nope.shape[0], q_nope.shape[1]
    kv = (ckv @ w_ukv).reshape(bsz, S, MLA_HEADS, MLA_NOPE + MLA_V)
    k_nope, v = kv[..., :MLA_NOPE], kv[..., MLA_NOPE:]
    qb = math.gcd(S, Q_BLOCK)
    nb = S // qb
    kpos = jnp.arange(S)

    def block(i):
        start = i * qb
        qn = lax.dynamic_slice_in_dim(q_nope, start, qb, axis=1)
        qr = lax.dynamic_slice_in_dim(q_rope, start, qb, axis=1)
        s = (jnp.einsum('bqhd,bkhd->bhqk', qn, k_nope)
             + jnp.einsum('bqhr,bkr->bhqk', qr, kr)).astype(jnp.float32) * MLA_SCALE
        qpos = start + jnp.arange(qb)
        s = jnp.where(kpos[None, :] <= qpos[:, None], s, -jnp.inf)
        p = jax.nn.softmax(s, axis=-1).astype(v.dtype)
        return jnp.einsum('bhqk,bkhv->bqhv', p, v)

    out = lax.map(block, jnp.arange(nb))
    return out.swapaxes(0, 1).reshape(bsz, S, MLA_HEADS, MLA_V)


def mla_attend_paged(q_nope, q_rope, ckv, kr, w_ukv, cache_mla, page_table):
    bsz, T = q_nope.shape[0], q_nope.shape[1]
    f32 = jnp.float32
    w = w_ukv.reshape(MLA_KV_RANK, MLA_HEADS, MLA_NOPE + MLA_V)
    w_uk, w_uv = w[..., :MLA_NOPE], w[..., MLA_NOPE:]
    q_lat = jnp.einsum('bthd,chd->bhtc', q_nope, w_uk).astype(f32)
    q_r = q_rope.transpose(0, 2, 1, 3).astype(f32)

    def scores(lat, rk):
        return (jnp.einsum('bhtc,bpc->bhtp', q_lat, lat) + jnp.einsum('bhtr,bpr->bhtp', q_r, rk)) * MLA_SCALE

    def accumulate(carry, s, lat):
        m, l, acc = carry
        m_new = jnp.maximum(m, jnp.max(s, axis=-1))
        corr = jnp.exp(m - m_new)
        p = jnp.exp(s - m_new[..., None])
        return (m_new, l * corr + jnp.sum(p, axis=-1),
                acc * corr[..., None] + jnp.einsum('bhtp,bpc->bhtc', p, lat))

    def page_step(carry, phys):
        page = cache_mla[phys].astype(f32)
        lat, rk = page[..., :MLA_KV_RANK], page[..., MLA_KV_RANK:]
        return accumulate(carry, scores(lat, rk), lat), None

    init = (jnp.full((bsz, MLA_HEADS, T), -jnp.inf, f32), jnp.zeros((bsz, MLA_HEADS, T), f32),
            jnp.zeros((bsz, MLA_HEADS, T, MLA_KV_RANK), f32))
    carry, _ = lax.scan(page_step, init, page_table.T)
    lat_new = ckv.astype(f32)
    causal = jnp.tril(jnp.ones((T, T), dtype=bool))
    s_new = jnp.where(causal, scores(lat_new, kr.astype(f32)), -jnp.inf)
    m, l, acc = accumulate(carry, s_new, lat_new)
    out_lat = acc / l[..., None]
    return jnp.einsum('bhtc,chv->bthv', out_lat.astype(w_uv.dtype), w_uv)


def even_mixer(h, pos, attend, mlstm_state, w_in_even, b_ig, b_fg, mlstm_norm_g, mla_q_norm_g,
               mla_kv_norm_g, w_uq, w_ukv, w_out_even):
    bsz, L = h.shape[0], h.shape[1]
    q, k, v, ig, fg, og, cq, ckv, kr = split_cols(h @ w_in_even, EVEN_SPLITS)
    q = q.reshape(bsz, L, ML_HEADS, ML_DQK) * (ML_DQK ** -0.5)
    k = k.reshape(bsz, L, ML_HEADS, ML_DQK)
    v = v.reshape(bsz, L, ML_HEADS, ML_DV)
    logi = (ig + b_ig).astype(jnp.float32)
    logf = jax.nn.log_sigmoid((fg + b_fg).astype(jnp.float32))
    hm, C, n, m = mlstm_chunkwise(q, k, v, logi, logf, *mlstm_state)
    hm = rmsnorm(hm.astype(h.dtype), mlstm_norm_g.reshape(ML_HEADS, ML_DV))
    hm = jax.nn.sigmoid(og).reshape(bsz, L, ML_HEADS, ML_DV) * hm
    qa = (rmsnorm(cq, mla_q_norm_g) @ w_uq).reshape(bsz, L, MLA_HEADS, MLA_NOPE + MLA_ROPE)
    q_nope, q_rope = qa[..., :MLA_NOPE], rope(qa[..., MLA_NOPE:], pos)
    ckv = rmsnorm(ckv, mla_kv_norm_g)
    kr = rope(kr[:, :, None, :], pos)[:, :, 0, :]
    ha = attend(q_nope, q_rope, ckv, kr, w_ukv)
    mixed = jnp.concatenate([hm.reshape(bsz, L, -1), ha.reshape(bsz, L, -1)], axis=-1) @ w_out_even
    return mixed, jnp.concatenate([ckv, kr], axis=-1), (C, n, m)


def mamba_mixer(h, conv_buf, ssm0, w_in_ssm, conv_w, conv_b, dt_bias, A_log, D_skip, ssm_norm_g, w_out_ssm):
    bsz, L = h.shape[0], h.shape[1]
    z, xbc, dt = split_cols(h @ w_in_ssm, (SSM_D_INNER, SSM_CONV_DIM, SSM_HEADS))
    full = jnp.concatenate([conv_buf.astype(xbc.dtype), xbc], axis=1)
    conv = conv_b
    for j in range(SSM_CONV):
        conv = conv + full[:, j:j + L, :] * conv_w[j]
    new_buf = full[:, L:, :]
    xbc = jax.nn.silu(conv)
    xs, Bm, Cm = split_cols(xbc, (SSM_D_INNER, SSM_GROUPS * SSM_STATE, SSM_GROUPS * SSM_STATE))
    xs = xs.reshape(bsz, L, SSM_HEADS, SSM_HEADDIM)
    Bm = Bm.reshape(bsz, L, SSM_GROUPS, SSM_STATE)
    Cm = Cm.reshape(bsz, L, SSM_GROUPS, SSM_STATE)
    dt = jax.nn.softplus((dt + dt_bias).astype(jnp.float32))
    A = -jnp.exp(A_log.astype(jnp.float32))
    y, h_new = ssd_chunked(xs, dt, A, Bm, Cm, ssm0)
    y = y.astype(h.dtype) + D_skip[:, None] * xs
    y = (y.reshape(bsz, L, SSM_D_INNER) * jax.nn.silu(z)).reshape(bsz, L, SSM_GROUPS, SSM_D_INNER // SSM_GROUPS)
    y = rmsnorm(y, ssm_norm_g.reshape(SSM_GROUPS, SSM_D_INNER // SSM_GROUPS)).reshape(bsz, L, SSM_D_INNER)
    return y @ w_out_ssm, h_new, new_buf


def setup_inputs(seed: int = 0) -> dict:
    key = jax.random.key(seed)
    k = jax.random.split(key, 32)
    f32 = jnp.float32

    def nrm(i, shape, scale):
        return jax.random.normal(k[i], shape, f32) * scale

    n_pages = PAST_LEN // PAGE_SIZE
    n_phys = (5 * DEC_BATCH * n_pages + 3) // 4
    page_table = jax.random.permutation(k[3], n_phys)[: DEC_BATCH * n_pages].reshape(DEC_BATCH, n_pages).astype(jnp.int32)
    dt0 = jnp.exp(jax.random.uniform(k[24], (SSM_HEADS,), f32, math.log(1e-3), math.log(1e-1)))
    return {
        'x_prompt': nrm(0, (BATCH, SEQ, D_MODEL), 1.0),
        'x_sample': nrm(1, (DEC_BATCH, DEC_SEQ, D_MODEL), 1.0),
        'cache_mla': nrm(2, (n_phys, PAGE_SIZE, MLA_KV_RANK + MLA_ROPE), 1.0),
        'state_mlstm_C': nrm(4, (DEC_BATCH, ML_HEADS, ML_DV, ML_DQK), 0.5),
        'state_mlstm_n': nrm(5, (DEC_BATCH, ML_HEADS, ML_DQK), 0.5),
        'state_mlstm_m': nrm(6, (DEC_BATCH, ML_HEADS), 0.5),
        'state_ssm': nrm(7, (DEC_BATCH, SSM_HEADS, SSM_HEADDIM, SSM_STATE), 0.1),
        'state_conv': nrm(8, (DEC_BATCH, SSM_CONV - 1, SSM_CONV_DIM), 1.0),
        'page_table': page_table,
        'norm_g': 1.0 + nrm(9, (DEPTH, 3, D_MODEL), 0.02),
        'ffn_w_gate': nrm(10, (DEPTH, 2, D_MODEL, D_FF), D_MODEL ** -0.5),
        'ffn_w_up': nrm(11, (DEPTH, 2, D_MODEL, D_FF), D_MODEL ** -0.5),
        'ffn_w_down': nrm(12, (DEPTH, 2, D_FF, D_MODEL), D_FF ** -0.5),
        'w_in_even': nrm(13, (D_MODEL, EVEN_IN), D_MODEL ** -0.5),
        'b_ig': nrm(14, (ML_HEADS,), 0.1),
        'b_fg': 3.0 + nrm(15, (ML_HEADS,), 0.5),
        'mlstm_norm_g': 1.0 + nrm(16, (ML_HEADS * ML_DV,), 0.02),
        'mla_q_norm_g': 1.0 + nrm(17, (MLA_Q_RANK,), 0.02),
        'mla_kv_norm_g': 1.0 + nrm(18, (MLA_KV_RANK,), 0.02),
        'w_uq': nrm(19, (MLA_Q_RANK, MLA_HEADS * (MLA_NOPE + MLA_ROPE)), MLA_Q_RANK ** -0.5),
        'w_ukv': nrm(20, (MLA_KV_RANK, MLA_HEADS * (MLA_NOPE + MLA_V)), MLA_KV_RANK ** -0.5),
        'w_out_even': nrm(21, (EVEN_OUT, D_MODEL), EVEN_OUT ** -0.5),
        'w_in_ssm': nrm(22, (D_MODEL, SSM_IN), D_MODEL ** -0.5),
        'conv_w': nrm(23, (SSM_CONV, SSM_CONV_DIM), SSM_CONV ** -0.5),
        'conv_b': nrm(25, (SSM_CONV_DIM,), 0.02),
        'dt_bias': dt0 + jnp.log(-jnp.expm1(-dt0)),
        'A_log': jnp.log(jax.random.uniform(k[26], (SSM_HEADS,), f32, 1.0, 16.0)),
        'D_skip': 1.0 + nrm(27, (SSM_HEADS,), 0.1),
        'ssm_norm_g': 1.0 + nrm(28, (SSM_D_INNER,), 0.02),
        'w_out_ssm': nrm(29, (SSM_D_INNER, D_MODEL), SSM_D_INNER ** -0.5),
        'final_norm_g': 1.0 + nrm(30, (D_MODEL,), 0.02),
    }


def reference(x_prompt, x_sample, cache_mla, state_mlstm_C, state_mlstm_n, state_mlstm_m, state_ssm,
              state_conv, page_table, norm_g, ffn_w_gate, ffn_w_up, ffn_w_down, w_in_even, b_ig, b_fg,
              mlstm_norm_g, mla_q_norm_g, mla_kv_norm_g, w_uq, w_ukv, w_out_even, w_in_ssm, conv_w,
              conv_b, dt_bias, A_log, D_skip, ssm_norm_g, w_out_ssm, final_norm_g):
    even_w = (w_in_even, b_ig, b_fg, mlstm_norm_g, mla_q_norm_g, mla_kv_norm_g, w_uq, w_ukv, w_out_even)
    odd_w = (w_in_ssm, conv_w, conv_b, dt_bias, A_log, D_skip, ssm_norm_g, w_out_ssm)

    def trunk(x, pos, attend, mlstm_state, ssm_state, conv_state):
        for layer in range(DEPTH):
            x = x + 0.5 * swiglu(rmsnorm(x, norm_g[layer, 0]), ffn_w_gate[layer, 0], ffn_w_up[layer, 0], ffn_w_down[layer, 0])
            hn = rmsnorm(x, norm_g[layer, 1])
            if layer % 2 == 0:
                mix, rows, mlstm_state = even_mixer(hn, pos, attend, mlstm_state, *even_w)
            else:
                mix, ssm_state, conv_state = mamba_mixer(hn, conv_state, ssm_state, *odd_w)
            x = x + mix
            x = x + 0.5 * swiglu(rmsnorm(x, norm_g[layer, 2]), ffn_w_gate[layer, 1], ffn_w_up[layer, 1], ffn_w_down[layer, 1])
        return rmsnorm(x, final_norm_g), rows, mlstm_state, ssm_state, conv_state

    bp, sp = x_prompt.shape[0], x_prompt.shape[1]
    f32 = jnp.float32
    mlstm0 = (jnp.zeros((bp, ML_HEADS, ML_DV, ML_DQK), f32), jnp.zeros((bp, ML_HEADS, ML_DQK), f32),
              jnp.zeros((bp, ML_HEADS), f32))
    ssm0 = jnp.zeros((bp, SSM_HEADS, SSM_HEADDIM, SSM_STATE), f32)
    conv0 = jnp.zeros((bp, SSM_CONV - 1, SSM_CONV_DIM), x_prompt.dtype)
    y_p, rows_p, (C_p, n_p, m_p), ssm_p, conv_p = trunk(x_prompt, jnp.arange(sp), mla_attend_prompt, mlstm0, ssm0, conv0)

    def attend_sample(qn, qr, ckv, kr, w):
        return mla_attend_paged(qn, qr, ckv, kr, w, cache_mla, page_table)

    pos_s = PAST_LEN + jnp.arange(x_sample.shape[1])
    y_s, rows_s, (C_s, n_s, m_s), ssm_s, conv_s = trunk(
        x_sample, pos_s, attend_sample, (state_mlstm_C, state_mlstm_n, state_mlstm_m), state_ssm, state_conv)
    return (y_p, y_s, rows_p, rows_s, C_p, C_s, n_p, n_s, m_p, m_s, ssm_p, ssm_s, conv_p, conv_s)
```

```python
import functools
import math

import jax
import jax.numpy as jnp
from jax import lax
from jax.experimental import pallas as pl
from jax.experimental.pallas import tpu as pltpu

F32 = jnp.float32
BF16 = jnp.bfloat16
HIGHEST = lax.Precision.HIGHEST

D_MODEL = 2048
D_FF = 5632
EPS = 1e-6
PAST_LEN = 8192
PAGE_SIZE = 128

ML_HEADS = 4
ML_DQK = 128
ML_DV = 256

MLA_HEADS = 8
MLA_NOPE = 128
MLA_ROPE = 64
MLA_V = 128
MLA_Q_RANK = 512
MLA_KV_RANK = 512
MLA_SCALE = (MLA_NOPE + MLA_ROPE) ** -0.5
ROPE_BASE = 10000.0
ROW_W = MLA_KV_RANK + MLA_ROPE

SSM_D_INNER = 4096
SSM_HEADDIM = 64
SSM_HEADS = 64
SSM_GROUPS = 8
SSM_HG = SSM_HEADS // SSM_GROUPS
SSM_GW = SSM_D_INNER // SSM_GROUPS
SSM_STATE = 128
SSM_CONV = 4
SSM_CONV_DIM = SSM_D_INNER + 2 * SSM_GROUPS * SSM_STATE

SAMPLE_PAD = 8
LANE = 128
VMEM_LIMIT = 56 * 1024 * 1024

P_Q, P_K, P_V, P_OG, P_CQ, P_CKV = 0, 512, 1024, 2048, 3072, 3584
P_KR, P_KRSW, P_GATE, P_WIDTH = 4096, 4224, 4352, 4608
Z_X, Z_B, Z_C, Z_DT, Z_WIDTH = 4096, 8192, 9216, 10240, 11264

NEG_BIG = -1e30


def _cparams(sem):
    return pltpu.CompilerParams(dimension_semantics=sem, vmem_limit_bytes=VMEM_LIMIT)


def _rms(x, g):
    return x * lax.rsqrt(jnp.mean(x * x, axis=-1, keepdims=True) + EPS) * g


def _sigmoid(x):
    return 1.0 / (1.0 + jnp.exp(-x))


def _silu(x):
    return x * _sigmoid(x)


def _log_sigmoid(x):
    return jnp.minimum(x, 0.0) - jnp.log1p(jnp.exp(-jnp.abs(x)))


def _softplus(x):
    return jnp.maximum(x, 0.0) + jnp.log1p(jnp.exp(-jnp.abs(x)))


def _dot(a, b):
    return jnp.dot(a, b, preferred_element_type=F32)


def _dot_nt(a, b):
    return lax.dot_general(a, b, (((1,), (1,)), ((), ())), preferred_element_type=F32)


def _dot_tn(a, b):
    return lax.dot_general(a, b, (((0,), (0,)), ((), ())), preferred_element_type=F32)


def _ffn_body(x_ref, g_ref, wg_ref, wu_ref, wd_ref, *rest, final_norm):
    if final_norm:
        gf_ref, o_ref, xn_ref = rest
    else:
        o_ref, xn_ref = rest
    j = pl.program_id(1)

    @pl.when(j == 0)
    def _():
        x = x_ref[...]
        xn_ref[...] = _rms(x, g_ref[...]).astype(BF16)
        o_ref[...] = x

    xn = xn_ref[...]
    hg = _dot(xn, wg_ref[...].astype(BF16))
    hu = _dot(xn, wu_ref[...].astype(BF16))
    h = _silu(hg) * (0.5 * hu)
    o_ref[...] += _dot(h.astype(BF16), wd_ref[...].astype(BF16))

    if final_norm:
        @pl.when(j == pl.num_programs(1) - 1)
        def _():
            o_ref[...] = _rms(o_ref[...], gf_ref[...])


def ffn(x, g, w_gate, w_up, w_down, layer, which, *, tm, tf, final_g=None):
    t, d = x.shape
    dff = w_gate.shape[-1]
    final_norm = final_g is not None
    in_specs = [
        pl.BlockSpec((tm, d), lambda i, j: (i, 0), pipeline_mode=pl.Buffered(1)),
        pl.BlockSpec((1, d), lambda i, j: (0, 0)),
        pl.BlockSpec((None, None, d, tf), lambda i, j: (layer, which, 0, j)),
        pl.BlockSpec((None, None, d, tf), lambda i, j: (layer, which, 0, j)),
        pl.BlockSpec((None, None, tf, d), lambda i, j: (layer, which, j, 0)),
    ]
    args = [x, g.reshape(1, d), w_gate, w_up, w_down]
    if final_norm:
        in_specs.append(pl.BlockSpec((1, d), lambda i, j: (0, 0)))
        args.append(final_g.reshape(1, d))
    return pl.pallas_call(
        functools.partial(_ffn_body, final_norm=final_norm),
        grid=(t // tm, dff // tf),
        in_specs=in_specs,
        out_specs=pl.BlockSpec((tm, d), lambda i, j: (i, 0)),
        out_shape=jax.ShapeDtypeStruct((t, d), F32),
        scratch_shapes=[pltpu.VMEM((tm, d), BF16)],
        compiler_params=_cparams(("parallel", "arbitrary")),
        name="ffn",
    )(*args)


def _norm_mm_body(x_ref, g_ref, w_ref, o_ref, xn_ref):
    @pl.when(pl.program_id(1) == 0)
    def _():
        xn_ref[...] = _rms(x_ref[...], g_ref[...]).astype(BF16)

    o_ref[...] = _dot(xn_ref[...], w_ref[...].astype(BF16))


def norm_matmul(x, g, w, *, tm, tn):
    t, d = x.shape
    n = w.shape[1]
    return pl.pallas_call(
        _norm_mm_body,
        grid=(t // tm, n // tn),
        in_specs=[
            pl.BlockSpec((tm, d), lambda i, j: (i, 0)),
            pl.BlockSpec((1, d), lambda i, j: (0, 0)),
            pl.BlockSpec((d, tn), lambda i, j: (0, j)),
        ],
        out_specs=pl.BlockSpec((tm, tn), lambda i, j: (i, j)),
        out_shape=jax.ShapeDtypeStruct((t, n), F32),
        scratch_shapes=[pltpu.VMEM((tm, d), BF16)],
        compiler_params=_cparams(("parallel", "arbitrary")),
        name="norm_matmul",
    )(x, g.reshape(1, d), w)


def _mm_res_body(*refs, n_a):
    a_refs, w_refs = refs[:n_a], refs[n_a:2 * n_a]
    res_ref, o_ref = refs[2 * n_a], refs[2 * n_a + 1]
    acc = res_ref[...]
    for a_ref, w_ref in zip(a_refs, w_refs):
        acc = acc + _dot(a_ref[...].astype(BF16), w_ref[...].astype(BF16))
    o_ref[...] = acc


def matmul_residual(a_list, w, res, *, tm, tn):
    n_a = len(a_list)
    t, ka = a_list[0].shape
    n = w.shape[1]
    in_specs = [pl.BlockSpec((tm, ka), lambda i, j: (i, 0)) for _ in a_list]
    in_specs += [pl.BlockSpec((ka, tn), lambda i, j, s=s: (s, j)) for s in range(n_a)]
    in_specs.append(pl.BlockSpec((tm, tn), lambda i, j: (i, j)))
    return pl.pallas_call(
        functools.partial(_mm_res_body, n_a=n_a),
        grid=(t // tm, n // tn),
        in_specs=in_specs,
        out_specs=pl.BlockSpec((tm, tn), lambda i, j: (i, j)),
        out_shape=jax.ShapeDtypeStruct((t, n), F32),
        compiler_params=_cparams(("parallel", "parallel")),
        name="matmul_residual",
    )(*a_list, *([w] * n_a), res)


def _swap_halves(w):
    half = w.shape[-1] // 2
    return jnp.concatenate([w[..., half:], w[..., :half]], axis=-1)


def _prep_w_in_even(w):
    d = w.shape[0]
    sizes = (512, 512, 1024, 4, 4, 1024, MLA_Q_RANK, MLA_KV_RANK, MLA_ROPE)
    idx = [0]
    for s in sizes:
        idx.append(idx[-1] + s)
    q, k, v, ig, fg, og, cq, ckv, kr = [w[:, idx[i]:idx[i + 1]] for i in range(len(sizes))]
    z64 = jnp.zeros((d, LANE - MLA_ROPE), w.dtype)
    gates = jnp.concatenate([ig, fg, jnp.zeros((d, LANE - 2 * ML_HEADS), w.dtype)], axis=1)
    pad = jnp.zeros((d, P_WIDTH - P_GATE - LANE), w.dtype)
    return jnp.concatenate([q, k, v, og, cq, ckv, kr, z64, _swap_halves(kr), z64, gates, pad], axis=1).astype(BF16)


def _prep_w_uq(w):
    r = w.shape[0]
    w3 = w.reshape(r, MLA_HEADS, MLA_NOPE + MLA_ROPE)
    nope = w3[..., :MLA_NOPE].reshape(r, MLA_HEADS * MLA_NOPE)
    rope = w3[..., MLA_NOPE:]
    zpad = jnp.zeros((r, MLA_HEADS, LANE - MLA_ROPE), w.dtype)
    ra = jnp.concatenate([rope, zpad], axis=-1).reshape(r, MLA_HEADS * LANE)
    rb = jnp.concatenate([_swap_halves(rope), zpad], axis=-1).reshape(r, MLA_HEADS * LANE)
    return jnp.concatenate([nope, ra, rb], axis=1).astype(BF16)


def _prep_w_in_ssm(w):
    d = w.shape[0]
    zx = w[:, :Z_DT]
    dt = w[:, Z_DT:].reshape(d, SSM_GROUPS, SSM_HG)
    dt = jnp.pad(dt, ((0, 0), (0, 0), (0, LANE - SSM_HG))).reshape(d, SSM_GROUPS * LANE)
    return jnp.concatenate([zx, dt], axis=1).astype(BF16)


def _rope_tables(pos):
    half = MLA_ROPE // 2
    inv = ROPE_BASE ** (-jnp.arange(half, dtype=F32) / half)
    ang = pos.astype(F32)[:, None] * inv[None, :]
    c, s = jnp.cos(ang), jnp.sin(ang)
    z = jnp.zeros((pos.shape[0], LANE - MLA_ROPE), F32)
    return jnp.concatenate([c, c, z], axis=1), jnp.concatenate([-s, s, z], axis=1)


def _q_prep_body(cq_ref, g_ref, w_ref, cos_ref, sin_ref, qn_ref, qr_ref):
    cn = _rms(cq_ref[...], g_ref[...]).astype(BF16)
    a = _dot(cn, w_ref[...])
    nw = MLA_HEADS * MLA_NOPE
    qn_ref[...] = (a[:, :nw] * MLA_SCALE).astype(BF16)
    cos, sin = cos_ref[...], sin_ref[...]
    for h in range(MLA_HEADS):
        ra = a[:, nw + h * LANE: nw + (h + 1) * LANE]
        rb = a[:, 2 * nw + h * LANE: 2 * nw + (h + 1) * LANE]
        qr_ref[:, h * LANE:(h + 1) * LANE] = ((ra * cos + rb * sin) * MLA_SCALE).astype(BF16)


def q_prep(p, g, wq, cos, sin, *, tm):
    t = p.shape[0]
    nw = MLA_HEADS * LANE
    return pl.pallas_call(
        _q_prep_body,
        grid=(t // tm,),
        in_specs=[
            pl.BlockSpec((tm, MLA_Q_RANK), lambda i: (i, P_CQ // MLA_Q_RANK)),
            pl.BlockSpec((1, MLA_Q_RANK), lambda i: (0, 0)),
            pl.BlockSpec(wq.shape, lambda i: (0, 0)),
            pl.BlockSpec((tm, LANE), lambda i: (i, 0)),
            pl.BlockSpec((tm, LANE), lambda i: (i, 0)),
        ],
        out_specs=[pl.BlockSpec((tm, nw), lambda i: (i, 0)), pl.BlockSpec((tm, nw), lambda i: (i, 0))],
        out_shape=[jax.ShapeDtypeStruct((t, nw), BF16), jax.ShapeDtypeStruct((t, nw), BF16)],
        compiler_params=_cparams(("parallel",)),
        name="q_prep",
    )(p, g.reshape(1, -1), wq, cos, sin)


def _kv_prep_body(ckv_ref, kr_ref, krsw_ref, g_ref, cos_ref, sin_ref, w_ref, rows_ref, kv_ref, krp_ref):
    cn = _rms(ckv_ref[...], g_ref[...])
    kr = kr_ref[...] * cos_ref[...] + krsw_ref[...] * sin_ref[...]
    rows_ref[:, :MLA_KV_RANK] = cn
    rows_ref[:, MLA_KV_RANK:] = kr[:, :MLA_ROPE]
    krp_ref[...] = kr.astype(BF16)
    kv_ref[...] = _dot(cn.astype(BF16), w_ref[...].astype(BF16)).astype(BF16)


def kv_prep(p, g, w_ukv, cos, sin, *, tm):
    t = p.shape[0]
    nkv = w_ukv.shape[1]
    return pl.pallas_call(
        _kv_prep_body,
        grid=(t // tm,),
        in_specs=[
            pl.BlockSpec((tm, MLA_KV_RANK), lambda i: (i, P_CKV // MLA_KV_RANK)),
            pl.BlockSpec((tm, LANE), lambda i: (i, P_KR // LANE)),
            pl.BlockSpec((tm, LANE), lambda i: (i, P_KRSW // LANE)),
            pl.BlockSpec((1, MLA_KV_RANK), lambda i: (0, 0)),
            pl.BlockSpec((tm, LANE), lambda i: (i, 0)),
            pl.BlockSpec((tm, LANE), lambda i: (i, 0)),
            pl.BlockSpec(w_ukv.shape, lambda i: (0, 0)),
        ],
        out_specs=[
            pl.BlockSpec((tm, ROW_W), lambda i: (i, 0)),
            pl.BlockSpec((tm, nkv), lambda i: (i, 0)),
            pl.BlockSpec((tm, LANE), lambda i: (i, 0)),
        ],
        out_shape=[
            jax.ShapeDtypeStruct((t, ROW_W), F32),
            jax.ShapeDtypeStruct((t, nkv), BF16),
            jax.ShapeDtypeStruct((t, LANE), BF16),
        ],
        compiler_params=_cparams(("parallel",)),
        name="kv_prep",
    )(p, p, p, g.reshape(1, -1), cos, sin, w_ukv)


def _attn_body(qn_ref, qr_ref, kn_ref, kr_ref, v_ref, o_ref, m_ref, l_ref, acc_ref, *, tq, tk):
    qi, ki = pl.program_id(2), pl.program_id(3)

    @pl.when(ki == 0)
    def _():
        m_ref[...] = jnp.full(m_ref.shape, -jnp.inf, F32)
        l_ref[...] = jnp.zeros(l_ref.shape, F32)
        acc_ref[...] = jnp.zeros(acc_ref.shape, F32)

    @pl.when(ki * tk <= qi * tq + (tq - 1))
    def _():
        q = jnp.concatenate([qn_ref[...], qr_ref[...]], axis=1)
        k = jnp.concatenate([kn_ref[...], kr_ref[...]], axis=1)
        s = _dot_nt(q, k)
        qpos = qi * tq + lax.broadcasted_iota(jnp.int32, (tq, tk), 0)
        kpos = ki * tk + lax.broadcasted_iota(jnp.int32, (tq, tk), 1)
        s = jnp.where(kpos <= qpos, s, -jnp.inf)
        m_old = m_ref[...]
        m_new = jnp.maximum(m_old, jnp.max(s, axis=-1, keepdims=True))
        corr = jnp.exp(m_old - m_new)
        p = jnp.exp(s - m_new)
        l_ref[...] = l_ref[...] * corr + jnp.sum(p, axis=-1, keepdims=True)
        acc_ref[...] = acc_ref[...] * corr + _dot(p.astype(BF16), v_ref[...])
        m_ref[...] = m_new

    @pl.when(ki == pl.num_programs(3) - 1)
    def _():
        o_ref[...] = (acc_ref[...] / l_ref[...]).astype(o_ref.dtype)


def mla_prompt_attention(qn, qr, kv, krp, *, bsz, seq, tq, tk):
    nq, nk = seq // tq, seq // tk

    def kidx(qi, ki):
        return jnp.minimum(ki, (qi * tq + tq - 1) // tk)

    return pl.pallas_call(
        functools.partial(_attn_body, tq=tq, tk=tk),
        grid=(bsz, MLA_HEADS, nq, nk),
        in_specs=[
            pl.BlockSpec((tq, LANE), lambda b, h, qi, ki: (b * nq + qi, h)),
            pl.BlockSpec((tq, LANE), lambda b, h, qi, ki: (b * nq + qi, h)),
            pl.BlockSpec((tk, LANE), lambda b, h, qi, ki: (b * nk + kidx(qi, ki), 2 * h)),
            pl.BlockSpec((tk, LANE), lambda b, h, qi, ki: (b * nk + kidx(qi, ki), 0)),
            pl.BlockSpec((tk, LANE), lambda b, h, qi, ki: (b * nk + kidx(qi, ki), 2 * h + 1)),
        ],
        out_specs=pl.BlockSpec((tq, LANE), lambda b, h, qi, ki: (b * nq + qi, h)),
        out_shape=jax.ShapeDtypeStruct((bsz * seq, MLA_HEADS * MLA_V), BF16),
        scratch_shapes=[pltpu.VMEM((tq, 1), F32), pltpu.VMEM((tq, 1), F32), pltpu.VMEM((tq, MLA_V), F32)],
        compiler_params=_cparams(("parallel", "parallel", "parallel", "arbitrary")),
        name="mla_prompt_attention",
    )(qn, qr, kv, krp, kv)


def _paged_body(pt_ref, qn_ref, qr_ref, rows_ref, w_ref, *rest, gp, n_valid):
    page_refs = rest[:gp]
    o_ref, ql_ref, qrr_ref, m_ref, l_ref, acc_ref = rest[gp:]
    del pt_ref
    g = pl.program_id(1)
    rows_q = MLA_HEADS * SAMPLE_PAD
    kvw = MLA_NOPE + MLA_V

    @pl.when(g == 0)
    def _():
        for h in range(MLA_HEADS):
            w_uk = w_ref[:, h * kvw: h * kvw + MLA_NOPE]
            ql = _dot_nt(qn_ref[:, h * LANE:(h + 1) * LANE], w_uk)
            ql_ref[h * SAMPLE_PAD:(h + 1) * SAMPLE_PAD, :] = ql.astype(BF16)
            qrr_ref[h * SAMPLE_PAD:(h + 1) * SAMPLE_PAD, :] = qr_ref[:, h * LANE:(h + 1) * LANE]
        m_ref[...] = jnp.full(m_ref.shape, -jnp.inf, F32)
        l_ref[...] = jnp.zeros(l_ref.shape, F32)
        acc_ref[...] = jnp.zeros(acc_ref.shape, F32)

    ql = ql_ref[...]
    qrr = qrr_ref[...][:, :MLA_ROPE]

    def accumulate(s_list, lat_list):
        m_old = m_ref[...]
        m_new = m_old
        for s in s_list:
            m_new = jnp.maximum(m_new, jnp.max(s, axis=-1, keepdims=True))
        corr = jnp.exp(m_old - m_new)
        l_new = l_ref[...] * corr
        acc = acc_ref[...] * corr
        for s, lat in zip(s_list, lat_list):
            p = jnp.exp(s - m_new)
            l_new = l_new + jnp.sum(p, axis=-1, keepdims=True)
            acc = acc + _dot(p.astype(BF16), lat)
        m_ref[...] = m_new
        l_ref[...] = l_new
        acc_ref[...] = acc

    s_list, lat_list = [], []
    for pr in page_refs:
        lat = pr[:, :MLA_KV_RANK].astype(BF16)
        rk = pr[:, MLA_KV_RANK:].astype(BF16)
        s_list.append(_dot_nt(ql, lat) + _dot_nt(qrr, rk))
        lat_list.append(lat)
    accumulate(s_list, lat_list)

    @pl.when(g == pl.num_programs(1) - 1)
    def _():
        lat = rows_ref[:, :MLA_KV_RANK].astype(BF16)
        rk = rows_ref[:, MLA_KV_RANK:].astype(BF16)
        s = _dot_nt(ql, lat) + _dot_nt(qrr, rk)
        tq = lax.broadcasted_iota(jnp.int32, (rows_q, SAMPLE_PAD), 0) % SAMPLE_PAD
        tk = lax.broadcasted_iota(jnp.int32, (rows_q, SAMPLE_PAD), 1)
        s = jnp.where((tk <= tq) & (tk < n_valid), s, -jnp.inf)
        accumulate([s], [lat])
        out_lat = (acc_ref[...] / l_ref[...]).astype(BF16)
        for h in range(MLA_HEADS):
            w_uv = w_ref[:, h * kvw + MLA_NOPE:(h + 1) * kvw]
            o = _dot(out_lat[h * SAMPLE_PAD:(h + 1) * SAMPLE_PAD, :], w_uv)
            o_ref[:, h * LANE:(h + 1) * LANE] = o.astype(o_ref.dtype)


def mla_paged_attention(qn, qr, rows, w_ukv, cache, page_table, *, row0, n_valid, gp):
    bsz, n_pages = page_table.shape
    blk0 = row0 // SAMPLE_PAD
    nw = MLA_HEADS * LANE
    npg = n_pages // gp

    def page_spec(i):
        return pl.BlockSpec((None, PAGE_SIZE, ROW_W),
                            lambda b, g, pt: (pt[b * n_pages + g * gp + i], 0, 0))

    grid_spec = pltpu.PrefetchScalarGridSpec(
        num_scalar_prefetch=1,
        grid=(bsz, npg),
        in_specs=[
            pl.BlockSpec((SAMPLE_PAD, nw), lambda b, g, pt: (blk0 + b, 0)),
            pl.BlockSpec((SAMPLE_PAD, nw), lambda b, g, pt: (blk0 + b, 0)),
            pl.BlockSpec((SAMPLE_PAD, ROW_W), lambda b, g, pt: (blk0 + b, 0)),
            pl.BlockSpec(w_ukv.shape, lambda b, g, pt: (0, 0)),
        ] + [page_spec(i) for i in range(gp)],
        out_specs=pl.BlockSpec((SAMPLE_PAD, nw), lambda b, g, pt: (b, 0)),
        scratch_shapes=[
            pltpu.VMEM((MLA_HEADS * SAMPLE_PAD, MLA_KV_RANK), BF16),
            pltpu.VMEM((MLA_HEADS * SAMPLE_PAD, LANE), BF16),
            pltpu.VMEM((MLA_HEADS * SAMPLE_PAD, 1), F32),
            pltpu.VMEM((MLA_HEADS * SAMPLE_PAD, 1), F32),
            pltpu.VMEM((MLA_HEADS * SAMPLE_PAD, MLA_KV_RANK), F32),
        ],
    )
    return pl.pallas_call(
        functools.partial(_paged_body, gp=gp, n_valid=n_valid),
        grid_spec=grid_spec,
        out_shape=jax.ShapeDtypeStruct((bsz * SAMPLE_PAD, nw), BF16),
        compiler_params=_cparams(("parallel", "arbitrary")),
        name="mla_paged_attention",
    )(page_table.reshape(-1), qn, qr, rows, w_ukv, *([cache] * gp))


def _mlstm_body(q_ref, k_ref, v_ref, og_ref, gc_ref, gr_ref, bc_ref, br_ref, ng_ref, c0_ref, n0_ref, m0_ref,
                h_ref, c_ref, n_ref, m_ref, *, lc, n_valid):
    ci = pl.program_id(1)

    @pl.when(ci == 0)
    def _():
        c_ref[...] = c0_ref[...]
        n_ref[...] = n0_ref[...]
        m_ref[...] = m0_ref[...]

    nh = ML_HEADS
    gc = gc_ref[...] + bc_ref[...]
    gr = gr_ref[...] + br_ref[:, 0:1]
    li_c, lf_c = gc[:, 0:nh], _log_sigmoid(gc[:, nh:2 * nh])
    li_r, lf_r = gr[0:nh, :], _log_sigmoid(gr[nh:2 * nh, :])
    if n_valid < lc:
        vc = lax.broadcasted_iota(jnp.int32, (lc, nh), 0) < n_valid
        vr = lax.broadcasted_iota(jnp.int32, (nh, lc), 1) < n_valid
        li_c, lf_c = jnp.where(vc, li_c, NEG_BIG), jnp.where(vc, lf_c, 0.0)
        li_r, lf_r = jnp.where(vr, li_r, NEG_BIG), jnp.where(vr, lf_r, 0.0)
    row = lax.broadcasted_iota(jnp.int32, (lc, lc), 0)
    col = lax.broadcasted_iota(jnp.int32, (lc, lc), 1)
    causal = col <= row
    tril = causal.astype(F32)
    triu = (row <= col).astype(F32)
    b_c = jnp.dot(tril, lf_c, precision=HIGHEST, preferred_element_type=F32)
    b_r = jnp.dot(lf_r, triu, precision=HIGHEST, preferred_element_type=F32)

    for h in range(nh):
        bc, br = b_c[:, h:h + 1], b_r[h:h + 1, :]
        ic, ir = li_c[:, h:h + 1], li_r[h:h + 1, :]
        m_prev = m_ref[h:h + 1, 0:1]
        qh = (q_ref[:, h * ML_DQK:(h + 1) * ML_DQK] * (ML_DQK ** -0.5))
        qb = qh.astype(BF16)
        kh = k_ref[:, h * ML_DQK:(h + 1) * ML_DQK]
        kb = kh.astype(BF16)
        vh = v_ref[:, h * ML_DV:(h + 1) * ML_DV]
        c_h = c_ref[h]
        n_h = n_ref[h:h + 1, :]

        dmat = jnp.where(causal, bc - br + ir, -jnp.inf)
        inter = bc + m_prev
        m_t = jnp.maximum(inter, jnp.max(dmat, axis=-1, keepdims=True))
        w_inter = jnp.exp(inter - m_t)
        s = _dot_nt(qb, kb) * jnp.exp(dmat - m_t)
        num = _dot(s.astype(BF16), vh.astype(BF16)) + w_inter * _dot_nt(qb, c_h.astype(BF16))
        den = jnp.sum(s, axis=-1, keepdims=True) + w_inter * jnp.sum(qh * n_h, axis=-1, keepdims=True)
        hout = num / jnp.maximum(jnp.abs(den), jnp.exp(-m_t))
        hn = _rms(hout, ng_ref[:, h * ML_DV:(h + 1) * ML_DV])
        og = og_ref[:, h * ML_DV:(h + 1) * ML_DV]
        h_ref[:, h * ML_DV:(h + 1) * ML_DV] = (_sigmoid(og) * hn).astype(h_ref.dtype)

        b_last = bc[lc - 1:lc, :]
        g_c = b_last - bc + ic
        g_r = b_last - br + ir
        m_new = jnp.maximum(b_last + m_prev, jnp.max(g_r, axis=-1, keepdims=True))
        decay = jnp.exp(b_last + m_prev - m_new)
        wg = jnp.exp(g_c - m_new)
        c_ref[h] = decay * c_h + _dot_tn((vh * wg).astype(BF16), kb)
        n_ref[h:h + 1, :] = decay * n_h + jnp.sum(kh * wg, axis=0, keepdims=True)
        m_ref[h:h + 1, :] = jnp.broadcast_to(m_new, (1, LANE))


def mlstm(p, gates_rows, b_ig, b_fg, norm_g, c0, n0, m0, *, bsz, row0, lc, nchunk, n_valid):
    blk0 = row0 // lc
    nh = ML_HEADS
    bias_c = jnp.concatenate([b_ig, b_fg, jnp.zeros((LANE - 2 * nh,), F32)]).reshape(1, LANE)
    bias_r = jnp.broadcast_to(jnp.concatenate([b_ig, b_fg]).reshape(2 * nh, 1), (2 * nh, LANE))

    def rows(cb):
        return lambda b, c: (blk0 + b * nchunk + c, cb)

    const2 = lambda b, c: (0, 0)
    return pl.pallas_call(
        functools.partial(_mlstm_body, lc=lc, n_valid=n_valid),
        grid=(bsz, nchunk),
        in_specs=[
            pl.BlockSpec((lc, nh * ML_DQK), rows(P_Q // (nh * ML_DQK))),
            pl.BlockSpec((lc, nh * ML_DQK), rows(P_K // (nh * ML_DQK))),
            pl.BlockSpec((lc, nh * ML_DV), rows(P_V // (nh * ML_DV))),
            pl.BlockSpec((lc, nh * ML_DV), rows(P_OG // (nh * ML_DV))),
            pl.BlockSpec((lc, LANE), rows(P_GATE // LANE)),
            pl.BlockSpec((None, 2 * nh, lc), lambda b, c: (b * nchunk + c, 0, 0)),
            pl.BlockSpec((1, LANE), const2),
            pl.BlockSpec((2 * nh, LANE), const2),
            pl.BlockSpec((1, nh * ML_DV), const2),
            pl.BlockSpec((None, nh, ML_DV, ML_DQK), lambda b, c: (b, 0, 0, 0)),
            pl.BlockSpec((None, nh, ML_DQK), lambda b, c: (b, 0, 0)),
            pl.BlockSpec((None, 2 * nh, LANE), lambda b, c: (b, 0, 0)),
        ],
        out_specs=[
            pl.BlockSpec((lc, nh * ML_DV), lambda b, c: (b * nchunk + c, 0)),
            pl.BlockSpec((None, nh, ML_DV, ML_DQK), lambda b, c: (b, 0, 0, 0)),
            pl.BlockSpec((None, nh, ML_DQK), lambda b, c: (b, 0, 0)),
            pl.BlockSpec((None, 2 * nh, LANE), lambda b, c: (b, 0, 0)),
        ],
        out_shape=[
            jax.ShapeDtypeStruct((bsz * nchunk * lc, nh * ML_DV), BF16),
            jax.ShapeDtypeStruct((bsz, nh, ML_DV, ML_DQK), F32),
            jax.ShapeDtypeStruct((bsz, nh, ML_DQK), F32),
            jax.ShapeDtypeStruct((bsz, 2 * nh, LANE), F32),
        ],
        compiler_params=_cparams(("parallel", "arbitrary")),
        name="mlstm",
    )(p, p, p, p, p, gates_rows, bias_c, bias_r, norm_g.reshape(1, -1), c0, n0, m0)


def _gates_rows(p, row0, nrows, lc):
    g = lax.slice(p, (row0, P_GATE), (row0 + nrows, P_GATE + 2 * ML_HEADS))
    return g.reshape(nrows // lc, lc, 2 * ML_HEADS).transpose(0, 2, 1)


def _m_pack(m):
    b = m.shape[0]
    mp = jnp.pad(m.astype(F32), ((0, 0), (0, 2 * ML_HEADS - m.shape[1])))
    return jnp.broadcast_to(mp[:, :, None], (b, 2 * ML_HEADS, LANE))


def _ssd_body(z_ref, x_ref, b_ref, c_ref, dtc_ref, dtr_ref, cwx_ref, cwb_ref, cwc_ref, cbx_ref, cbb_ref, cbc_ref,
              csx_ref, csb_ref, csc_ref, pc_ref, pr_ref, dsk_ref, ng_ref, h0_ref,
              y_ref, h_ref, bufx, bufb, bufc, *, lc, n_valid):
    ci = pl.program_id(2)
    tail = SAMPLE_PAD

    @pl.when(ci == 0)
    def _():
        h_ref[...] = h0_ref[...]
        bufx[0:tail, :] = csx_ref[...]
        bufb[0:tail, :] = csb_ref[...]
        bufc[0:tail, :] = csc_ref[...]

    def conv(raw_ref, buf, cw_ref, cb_ref):
        raw = raw_ref[...]
        buf[tail:tail + lc, :] = raw
        acc = cb_ref[...] + cw_ref[SSM_CONV - 1:SSM_CONV, :] * raw
        for j in range(SSM_CONV - 1):
            off = tail - (SSM_CONV - 1) + j
            acc = acc + cw_ref[j:j + 1, :] * buf[off:off + lc, :]
        buf[0:tail, :] = raw[lc - tail:lc, :]
        return _silu(acc)

    xa = conv(x_ref, bufx, cwx_ref, cbx_ref)
    ba = conv(b_ref, bufb, cwb_ref, cbb_ref)
    ca = conv(c_ref, bufc, cwc_ref, cbc_ref)

    hg = SSM_HG
    dt_c = _softplus(dtc_ref[:, 0:hg] + pc_ref[0:1, 0:hg])
    dt_r = _softplus(dtr_ref[...] + pr_ref[:, 0:1])
    if n_valid < lc:
        dt_c = jnp.where(lax.broadcasted_iota(jnp.int32, (lc, hg), 0) < n_valid, dt_c, 0.0)
        dt_r = jnp.where(lax.broadcasted_iota(jnp.int32, (hg, lc), 1) < n_valid, dt_r, 0.0)
    a_c = dt_c * (-jnp.exp(pc_ref[1:2, 0:hg]))
    a_r = dt_r * (-jnp.exp(pr_ref[:, 1:2]))
    row = lax.broadcasted_iota(jnp.int32, (lc, lc), 0)
    col = lax.broadcasted_iota(jnp.int32, (lc, lc), 1)
    causal = col <= row
    cum_c = jnp.dot(causal.astype(F32), a_c, precision=HIGHEST, preferred_element_type=F32)
    cum_r = jnp.dot(a_r, (row <= col).astype(F32), precision=HIGHEST, preferred_element_type=F32)

    expand = (lax.broadcasted_iota(jnp.int32, (hg, SSM_GW), 1) // SSM_HEADDIM
              == lax.broadcasted_iota(jnp.int32, (hg, SSM_GW), 0)).astype(F32)
    cum_last_c = cum_c[lc - 1:lc, :]
    e_in = jnp.dot(jnp.exp(cum_c), expand, precision=HIGHEST, preferred_element_type=F32)
    e_st = jnp.dot(jnp.exp(cum_last_c - cum_c) * dt_c, expand, precision=HIGHEST, preferred_element_type=F32)

    cb16 = ca.astype(BF16)
    bb16 = ba.astype(BF16)
    cbm = _dot_nt(cb16, bb16)
    h_old = h_ref[...]
    y = _dot_nt(cb16, h_old.astype(BF16)) * e_in
    lane_head = lax.broadcasted_iota(jnp.int32, (lc, LANE), 1) // SSM_HEADDIM
    pieces = []
    for pair in range(hg // 2):
        xp = xa[:, pair * LANE:(pair + 1) * LANE]
        yp = jnp.zeros((lc, LANE), F32)
        for sub in range(2):
            j = 2 * pair + sub
            seg = jnp.where(causal, cum_c[:, j:j + 1] - cum_r[j:j + 1, :], -jnp.inf)
            w = cbm * jnp.exp(seg) * dt_r[j:j + 1, :]
            xm = jnp.where(lane_head == sub, xp, 0.0)
            yp = yp + _dot(w.astype(BF16), xm.astype(BF16))
        pieces.append(yp)
    y = y + jnp.concatenate(pieces, axis=1)

    upd = _dot_tn((xa * e_st).astype(BF16), bb16)
    dec_r = jnp.exp(cum_r[:, lc - 1:lc])
    for j in range(hg):
        sl = slice(j * SSM_HEADDIM, (j + 1) * SSM_HEADDIM)
        h_ref[sl, :] = dec_r[j:j + 1, :] * h_old[sl, :] + upd[sl, :]

    y = y + dsk_ref[...] * xa
    y = y * _silu(z_ref[...])
    y_ref[...] = _rms(y, ng_ref[...]).astype(y_ref.dtype)


def ssd(zp, dt_rows, conv_w, conv_b, conv_state8, dt_bias, a_log, d_skip, norm_g, h0, *, bsz, row0, lc, nchunk,
        n_valid):
    blk0 = row0 // lc
    g_, hg, gw, ns = SSM_GROUPS, SSM_HG, SSM_GW, SSM_STATE

    def rows(off, width):
        return lambda b, g, c: (blk0 + b * nchunk + c, off // width + g)

    pcol = jnp.zeros((g_, 8, LANE), F32)
    pcol = pcol.at[:, 0, :hg].set(dt_bias.reshape(g_, hg)).at[:, 1, :hg].set(a_log.reshape(g_, hg))
    prow = jnp.zeros((g_, hg, LANE), F32)
    prow = prow.at[:, :, 0].set(dt_bias.reshape(g_, hg)).at[:, :, 1].set(a_log.reshape(g_, hg))
    dsk = jnp.repeat(d_skip, SSM_HEADDIM).reshape(1, SSM_D_INNER)
    cb2 = conv_b.reshape(1, -1)
    xo, bo, co = 0, SSM_D_INNER // ns, SSM_D_INNER // ns + g_

    in_specs = [
        pl.BlockSpec((lc, gw), rows(0, gw)),
        pl.BlockSpec((lc, gw), rows(Z_X, gw)),
        pl.BlockSpec((lc, ns), rows(Z_B, ns)),
        pl.BlockSpec((lc, ns), rows(Z_C, ns)),
        pl.BlockSpec((lc, LANE), rows(Z_DT, LANE)),
        pl.BlockSpec((None, hg, lc), lambda b, g, c: (b * nchunk + c, g, 0)),
        pl.BlockSpec((SSM_CONV, gw), lambda b, g, c: (0, g)),
        pl.BlockSpec((SSM_CONV, ns), lambda b, g, c: (0, bo + g)),
        pl.BlockSpec((SSM_CONV, ns), lambda b, g, c: (0, co + g)),
        pl.BlockSpec((1, gw), lambda b, g, c: (0, g)),
        pl.BlockSpec((1, ns), lambda b, g, c: (0, bo + g)),
        pl.BlockSpec((1, ns), lambda b, g, c: (0, co + g)),
        pl.BlockSpec((None, 8, gw), lambda b, g, c: (b, 0, g)),
        pl.BlockSpec((None, 8, ns), lambda b, g, c: (b, 0, bo + g)),
        pl.BlockSpec((None, 8, ns), lambda b, g, c: (b, 0, co + g)),
        pl.BlockSpec((None, 8, LANE), lambda b, g, c: (g, 0, 0)),
        pl.BlockSpec((None, hg, LANE), lambda b, g, c: (g, 0, 0)),
        pl.BlockSpec((1, gw), lambda b, g, c: (0, g)),
        pl.BlockSpec((1, gw), lambda b, g, c: (0, g)),
        pl.BlockSpec((None, None, gw, ns), lambda b, g, c: (b, g, 0, 0)),
    ]
    return pl.pallas_call(
        functools.partial(_ssd_body, lc=lc, n_valid=n_valid),
        grid=(bsz, g_, nchunk),
        in_specs=in_specs,
        out_specs=[
            pl.BlockSpec((lc, gw), lambda b, g, c: (b * nchunk + c, g)),
            pl.BlockSpec((None, None, gw, ns), lambda b, g, c: (b, g, 0, 0)),
        ],
        out_shape=[
            jax.ShapeDtypeStruct((bsz * nchunk * lc, SSM_D_INNER), BF16),
            jax.ShapeDtypeStruct((bsz, g_, gw, ns), F32),
        ],
        scratch_shapes=[
            pltpu.VMEM((lc + SAMPLE_PAD, gw), F32),
            pltpu.VMEM((lc + SAMPLE_PAD, ns), F32),
            pltpu.VMEM((lc + SAMPLE_PAD, ns), F32),
        ],
        compiler_params=_cparams(("parallel", "parallel", "arbitrary")),
        name="ssd",
    )(zp, zp, zp, zp, zp, dt_rows, conv_w, conv_w, conv_w, cb2, cb2, cb2, conv_state8, conv_state8, conv_state8,
      pcol, prow, dsk, norm_g.reshape(1, -1), h0)


def _dt_rows(zp, row0, nrows, lc):
    dt = lax.slice(zp, (row0, Z_DT), (row0 + nrows, Z_WIDTH))
    dt = dt.reshape(nrows, SSM_GROUPS, LANE)[:, :, :SSM_HG].reshape(nrows // lc, lc, SSM_HEADS)
    return dt.transpose(0, 2, 1)


def _conv_state8(state):
    return jnp.pad(state, ((0, 0), (SAMPLE_PAD - (SSM_CONV - 1), 0), (0, 0)))


def _trunk(x_prompt, x_sample, cache_mla, state_mlstm_C, state_mlstm_n, state_mlstm_m, state_ssm, state_conv,
           page_table, norm_g, ffn_w_gate, ffn_w_up, ffn_w_down, w_in_even, b_ig, b_fg, mlstm_norm_g, mla_q_norm_g,
           mla_kv_norm_g, w_uq, w_ukv, w_out_even, w_in_ssm, conv_w, conv_b, dt_bias, A_log, D_skip, ssm_norm_g,
           w_out_ssm, final_norm_g, *, tm, tf, tn, tmp, lc_ml, lc_ssd, tq, tk, gp):
    bp, sp, d = x_prompt.shape
    bs, ts, _ = x_sample.shape
    tp = bp * sp
    tsp = bs * SAMPLE_PAD
    x = jnp.concatenate([x_prompt.reshape(tp, d),
                         jnp.pad(x_sample, ((0, 0), (0, SAMPLE_PAD - ts), (0, 0))).reshape(tsp, d)], axis=0)

    pos_p = jnp.tile(jnp.arange(sp), bp)
    pos_s = jnp.tile(PAST_LEN + jnp.arange(SAMPLE_PAD), bs)
    cos, sin = _rope_tables(jnp.concatenate([pos_p, pos_s]))

    mm = functools.partial(matmul_residual, tm=tm, tn=tn)
    ff = functools.partial(ffn, w_gate=ffn_w_gate, w_up=ffn_w_up, w_down=ffn_w_down, tm=tm, tf=tf)

    x = ff(x, norm_g[0, 0], layer=0, which=0)
    p = norm_matmul(x, norm_g[0, 1], _prep_w_in_even(w_in_even), tm=tm, tn=tn)

    zeros = functools.partial(jnp.zeros, dtype=F32)
    hm_p, c_p, n_p, m_p = mlstm(
        p, _gates_rows(p, 0, tp, lc_ml), b_ig, b_fg, mlstm_norm_g,
        zeros((bp, ML_HEADS, ML_DV, ML_DQK)), zeros((bp, ML_HEADS, ML_DQK)), zeros((bp, 2 * ML_HEADS, LANE)),
        bsz=bp, row0=0, lc=lc_ml, nchunk=sp // lc_ml, n_valid=lc_ml)
    hm_s, c_s, n_s, m_s = mlstm(
        p, _gates_rows(p, tp, tsp, SAMPLE_PAD), b_ig, b_fg, mlstm_norm_g,
        state_mlstm_C, state_mlstm_n, _m_pack(state_mlstm_m),
        bsz=bs, row0=tp, lc=SAMPLE_PAD, nchunk=1, n_valid=ts)

    w_ukv16 = w_ukv.astype(BF16)
    qn, qr = q_prep(p, mla_q_norm_g, _prep_w_uq(w_uq), cos, sin, tm=tmp)
    rows, kv, krp = kv_prep(p, mla_kv_norm_g, w_ukv16, cos, sin, tm=tmp)
    ha_p = mla_prompt_attention(qn, qr, kv, krp, bsz=bp, seq=sp, tq=tq, tk=tk)
    ha_s = mla_paged_attention(qn, qr, rows, w_ukv16, cache_mla, page_table, row0=tp, n_valid=ts, gp=gp)

    hm = jnp.concatenate([hm_p, hm_s], axis=0)
    ha = jnp.concatenate([ha_p, ha_s], axis=0)
    x = mm([hm, ha], w_out_even, x)
    x = ff(x, norm_g[0, 2], layer=0, which=1)

    x = ff(x, norm_g[1, 0], layer=1, which=0)
    zp = norm_matmul(x, norm_g[1, 1], _prep_w_in_ssm(w_in_ssm), tm=tm, tn=tn)
    y_p, ssm_p = ssd(
        zp, _dt_rows(zp, 0, tp, lc_ssd), conv_w, conv_b, zeros((bp, SAMPLE_PAD, SSM_CONV_DIM)), dt_bias, A_log,
        D_skip, ssm_norm_g, zeros((bp, SSM_GROUPS, SSM_GW, SSM_STATE)),
        bsz=bp, row0=0, lc=lc_ssd, nchunk=sp // lc_ssd, n_valid=lc_ssd)
    y_s, ssm_s = ssd(
        zp, _dt_rows(zp, tp, tsp, SAMPLE_PAD), conv_w, conv_b, _conv_state8(state_conv), dt_bias, A_log,
        D_skip, ssm_norm_g, state_ssm.reshape(bs, SSM_GROUPS, SSM_GW, SSM_STATE),
        bsz=bs, row0=tp, lc=SAMPLE_PAD, nchunk=1, n_valid=ts)
    x = mm([jnp.concatenate([y_p, y_s], axis=0)], w_out_ssm, x)
    x = ff(x, norm_g[1, 2], layer=1, which=1, final_g=final_norm_g)

    def split_rows(a):
        w = a.shape[1]
        return a[:tp].reshape(bp, sp, w), a[tp:].reshape(bs, SAMPLE_PAD, w)[:, :ts]

    y_prompt, y_sample = split_rows(x)
    rows_p, rows_s = split_rows(rows)
    keep = SSM_CONV - 1
    assert sp >= keep and ts >= keep
    conv_p = zp[:tp].reshape(bp, sp, Z_WIDTH)[:, sp - keep:sp, Z_X:Z_DT]
    conv_s = zp[tp:].reshape(bs, SAMPLE_PAD, Z_WIDTH)[:, ts - keep:ts, Z_X:Z_DT]
    hshape = (SSM_HEADS, SSM_HEADDIM, SSM_STATE)
    return (y_prompt, y_sample, rows_p, rows_s, c_p, c_s, n_p, n_s,
            m_p[:, :ML_HEADS, 0], m_s[:, :ML_HEADS, 0],
            ssm_p.reshape((bp,) + hshape), ssm_s.reshape((bs,) + hshape), conv_p, conv_s)


def kernel(x_prompt, x_sample, cache_mla, state_mlstm_C, state_mlstm_n, state_mlstm_m, state_ssm, state_conv, page_table, norm_g, ffn_w_gate, ffn_w_up, ffn_w_down, w_in_even, b_ig, b_fg, mlstm_norm_g, mla_q_norm_g, mla_kv_norm_g, w_uq, w_ukv, w_out_even, w_in_ssm, conv_w, conv_b, dt_bias, A_log, D_skip, ssm_norm_g, w_out_ssm, final_norm_g):
    return _trunk(x_prompt, x_sample, cache_mla, state_mlstm_C, state_mlstm_n, state_mlstm_m, state_ssm, state_conv,
                  page_table, norm_g, ffn_w_gate, ffn_w_up, ffn_w_down, w_in_even, b_ig, b_fg, mlstm_norm_g,
                  mla_q_norm_g, mla_kv_norm_g, w_uq, w_ukv, w_out_even, w_in_ssm, conv_w, conv_b, dt_bias, A_log,
                  D_skip, ssm_norm_g, w_out_ssm, final_norm_g,
                  tm=1024, tf=256, tn=512, tmp=512, lc_ml=256, lc_ssd=128, tq=512, tk=512, gp=16)
```

```python
import functools
import math

import jax
import jax.numpy as jnp
from jax import lax
from jax.experimental import pallas as pl
from jax.experimental.pallas import tpu as pltpu

F32 = jnp.float32
BF16 = jnp.bfloat16
HIGHEST = lax.Precision.HIGHEST

D_MODEL = 2048
D_FF = 5632
EPS = 1e-6
PAST_LEN = 8192
PAGE_SIZE = 128

ML_HEADS = 4
ML_DQK = 128
ML_DV = 256

MLA_HEADS = 8
MLA_NOPE = 128
MLA_ROPE = 64
MLA_V = 128
MLA_Q_RANK = 512
MLA_KV_RANK = 512
MLA_SCALE = (MLA_NOPE + MLA_ROPE) ** -0.5
ROPE_BASE = 10000.0
ROW_W = MLA_KV_RANK + MLA_ROPE

SSM_D_INNER = 4096
SSM_HEADDIM = 64
SSM_HEADS = 64
SSM_GROUPS = 8
SSM_HG = SSM_HEADS // SSM_GROUPS
SSM_GW = SSM_D_INNER // SSM_GROUPS
SSM_STATE = 128
SSM_CONV = 4
SSM_CONV_DIM = SSM_D_INNER + 2 * SSM_GROUPS * SSM_STATE

SAMPLE_PAD = 8
LANE = 128
VMEM_LIMIT = 56 * 1024 * 1024

P_Q, P_K, P_V, P_OG, P_CQ, P_CKV = 0, 512, 1024, 2048, 3072, 3584
P_KR, P_KRSW, P_GATE, P_WIDTH = 4096, 4224, 4352, 4608
Z_X, Z_B, Z_C, Z_DT, Z_WIDTH = 4096, 8192, 9216, 10240, 11264

NEG_BIG = -1e30


def _cparams(sem):
    return pltpu.CompilerParams(dimension_semantics=sem, vmem_limit_bytes=VMEM_LIMIT)


def _rms(x, g):
    return x * lax.rsqrt(jnp.mean(x * x, axis=-1, keepdims=True) + EPS) * g


def _sigmoid(x):
    return 1.0 / (1.0 + jnp.exp(-x))


def _silu(x):
    return x * _sigmoid(x)


def _log_sigmoid(x):
    return jnp.minimum(x, 0.0) - jnp.log1p(jnp.exp(-jnp.abs(x)))


def _softplus(x):
    return jnp.maximum(x, 0.0) + jnp.log1p(jnp.exp(-jnp.abs(x)))


def _dot(a, b):
    return jnp.dot(a, b, preferred_element_type=F32)


def _dot_nt(a, b):
    return lax.dot_general(a, b, (((1,), (1,)), ((), ())), preferred_element_type=F32)


def _dot_tn(a, b):
    return lax.dot_general(a, b, (((0,), (0,)), ((), ())), preferred_element_type=F32)


def _ffn_body(x_ref, g_ref, wg_ref, wu_ref, wd_ref, *rest, final_norm):
    if final_norm:
        gf_ref, o_ref, xn_ref = rest
    else:
        o_ref, xn_ref = rest
    j = pl.program_id(1)

    @pl.when(j == 0)
    def _():
        x = x_ref[...]
        xn_ref[...] = _rms(x, g_ref[...]).astype(BF16)
        o_ref[...] = x

    xn = xn_ref[...]
    hg = _dot(xn, wg_ref[...].astype(BF16))
    hu = _dot(xn, wu_ref[...].astype(BF16))
    h = _silu(hg) * (0.5 * hu)
    o_ref[...] += _dot(h.astype(BF16), wd_ref[...].astype(BF16))

    if final_norm:
        @pl.when(j == pl.num_programs(1) - 1)
        def _():
            o_ref[...] = _rms(o_ref[...], gf_ref[...])


def ffn(x, g, w_gate, w_up, w_down, layer, which, *, tm, tf, final_g=None):
    t, d = x.shape
    dff = w_gate.shape[-1]
    final_norm = final_g is not None
    in_specs = [
        pl.BlockSpec((tm, d), lambda i, j: (i, 0), pipeline_mode=pl.Buffered(1)),
        pl.BlockSpec((1, d), lambda i, j: (0, 0)),
        pl.BlockSpec((None, None, d, tf), lambda i, j: (layer, which, 0, j)),
        pl.BlockSpec((None, None, d, tf), lambda i, j: (layer, which, 0, j)),
        pl.BlockSpec((None, None, tf, d), lambda i, j: (layer, which, j, 0)),
    ]
    args = [x, g.reshape(1, d), w_gate, w_up, w_down]
    if final_norm:
        in_specs.append(pl.BlockSpec((1, d), lambda i, j: (0, 0)))
        args.append(final_g.reshape(1, d))
    return pl.pallas_call(
        functools.partial(_ffn_body, final_norm=final_norm),
        grid=(t // tm, dff // tf),
        in_specs=in_specs,
        out_specs=pl.BlockSpec((tm, d), lambda i, j: (i, 0)),
        out_shape=jax.ShapeDtypeStruct((t, d), F32),
        scratch_shapes=[pltpu.VMEM((tm, d), BF16)],
        compiler_params=_cparams(("parallel", "arbitrary")),
        name="ffn",
    )(*args)


def _norm_mm_body(x_ref, g_ref, w_ref, o_ref, xn_ref):
    @pl.when(pl.program_id(1) == 0)
    def _():
        xn_ref[...] = _rms(x_ref[...], g_ref[...]).astype(BF16)

    o_ref[...] = _dot(xn_ref[...], w_ref[...].astype(BF16))


def norm_matmul(x, g, w, *, tm, tn):
    t, d = x.shape
    n = w.shape[1]
    return pl.pallas_call(
        _norm_mm_body,
        grid=(t // tm, n // tn),
        in_specs=[
            pl.BlockSpec((tm, d), lambda i, j: (i, 0)),
            pl.BlockSpec((1, d), lambda i, j: (0, 0)),
            pl.BlockSpec((d, tn), lambda i, j: (0, j)),
        ],
        out_specs=pl.BlockSpec((tm, tn), lambda i, j: (i, j)),
        out_shape=jax.ShapeDtypeStruct((t, n), F32),
        scratch_shapes=[pltpu.VMEM((tm, d), BF16)],
        compiler_params=_cparams(("parallel", "arbitrary")),
        name="norm_matmul",
    )(x, g.reshape(1, d), w)


def _mm_res_body(*refs, n_a):
    a_refs, w_refs = refs[:n_a], refs[n_a:2 * n_a]
    res_ref, o_ref = refs[2 * n_a], refs[2 * n_a + 1]
    acc = res_ref[...]
    for a_ref, w_ref in zip(a_refs, w_refs):
        acc = acc + _dot(a_ref[...].astype(BF16), w_ref[...].astype(BF16))
    o_ref[...] = acc


def matmul_residual(a_list, w, res, *, tm, tn):
    n_a = len(a_list)
    t, ka = a_list[0].shape
    n = w.shape[1]
    in_specs = [pl.BlockSpec((tm, ka), lambda i, j: (i, 0)) for _ in a_list]
    in_specs += [pl.BlockSpec((ka, tn), lambda i, j, s=s: (s, j)) for s in range(n_a)]
    in_specs.append(pl.BlockSpec((tm, tn), lambda i, j: (i, j)))
    return pl.pallas_call(
        functools.partial(_mm_res_body, n_a=n_a),
        grid=(t // tm, n // tn),
        in_specs=in_specs,
        out_specs=pl.BlockSpec((tm, tn), lambda i, j: (i, j)),
        out_shape=jax.ShapeDtypeStruct((t, n), F32),
        compiler_params=_cparams(("parallel", "parallel")),
        name="matmul_residual",
    )(*a_list, *([w] * n_a), res)


def _swap_halves(w):
    half = w.shape[-1] // 2
    return jnp.concatenate([w[..., half:], w[..., :half]], axis=-1)


def _prep_w_in_even(w):
    d = w.shape[0]
    sizes = (512, 512, 1024, 4, 4, 1024, MLA_Q_RANK, MLA_KV_RANK, MLA_ROPE)
    idx = [0]
    for s in sizes:
        idx.append(idx[-1] + s)
    q, k, v, ig, fg, og, cq, ckv, kr = [w[:, idx[i]:idx[i + 1]] for i in range(len(sizes))]
    z64 = jnp.zeros((d, LANE - MLA_ROPE), w.dtype)
    gates = jnp.concatenate([ig, fg, jnp.zeros((d, LANE - 2 * ML_HEADS), w.dtype)], axis=1)
    pad = jnp.zeros((d, P_WIDTH - P_GATE - LANE), w.dtype)
    return jnp.concatenate([q, k, v, og, cq, ckv, kr, z64, _swap_halves(kr), z64, gates, pad], axis=1).astype(BF16)


def _prep_w_uq(w):
    r = w.shape[0]
    w3 = w.reshape(r, MLA_HEADS, MLA_NOPE + MLA_ROPE)
    nope = w3[..., :MLA_NOPE].reshape(r, MLA_HEADS * MLA_NOPE)
    rope = w3[..., MLA_NOPE:]
    zpad = jnp.zeros((r, MLA_HEADS, LANE - MLA_ROPE), w.dtype)
    ra = jnp.concatenate([rope, zpad], axis=-1).reshape(r, MLA_HEADS * LANE)
    rb = jnp.concatenate([_swap_halves(rope), zpad], axis=-1).reshape(r, MLA_HEADS * LANE)
    return jnp.concatenate([nope, ra, rb], axis=1).astype(BF16)


def _prep_w_in_ssm(w):
    d = w.shape[0]
    zx = w[:, :Z_DT]
    dt = w[:, Z_DT:].reshape(d, SSM_GROUPS, SSM_HG)
    dt = jnp.pad(dt, ((0, 0), (0, 0), (0, LANE - SSM_HG))).reshape(d, SSM_GROUPS * LANE)
    return jnp.concatenate([zx, dt], axis=1).astype(BF16)


def _rope_tables(pos):
    half = MLA_ROPE // 2
    inv = ROPE_BASE ** (-jnp.arange(half, dtype=F32) / half)
    ang = pos.astype(F32)[:, None] * inv[None, :]
    c, s = jnp.cos(ang), jnp.sin(ang)
    z = jnp.zeros((pos.shape[0], LANE - MLA_ROPE), F32)
    return jnp.concatenate([c, c, z], axis=1), jnp.concatenate([-s, s, z], axis=1)


def _q_prep_body(cq_ref, g_ref, w_ref, cos_ref, sin_ref, qn_ref, qr_ref):
    cn = _rms(cq_ref[...], g_ref[...]).astype(BF16)
    a = _dot(cn, w_ref[...])
    nw = MLA_HEADS * MLA_NOPE
    qn_ref[...] = (a[:, :nw] * MLA_SCALE).astype(BF16)
    cos, sin = cos_ref[...], sin_ref[...]
    for h in range(MLA_HEADS):
        ra = a[:, nw + h * LANE: nw + (h + 1) * LANE]
        rb = a[:, 2 * nw + h * LANE: 2 * nw + (h + 1) * LANE]
        qr_ref[:, h * LANE:(h + 1) * LANE] = ((ra * cos + rb * sin) * MLA_SCALE).astype(BF16)


def q_prep(p, g, wq, cos, sin, *, tm):
    t = p.shape[0]
    nw = MLA_HEADS * LANE
    return pl.pallas_call(
        _q_prep_body,
        grid=(t // tm,),
        in_specs=[
            pl.BlockSpec((tm, MLA_Q_RANK), lambda i: (i, P_CQ // MLA_Q_RANK)),
            pl.BlockSpec((1, MLA_Q_RANK), lambda i: (0, 0)),
            pl.BlockSpec(wq.shape, lambda i: (0, 0)),
            pl.BlockSpec((tm, LANE), lambda i: (i, 0)),
            pl.BlockSpec((tm, LANE), lambda i: (i, 0)),
        ],
        out_specs=[pl.BlockSpec((tm, nw), lambda i: (i, 0)), pl.BlockSpec((tm, nw), lambda i: (i, 0))],
        out_shape=[jax.ShapeDtypeStruct((t, nw), BF16), jax.ShapeDtypeStruct((t, nw), BF16)],
        compiler_params=_cparams(("parallel",)),
        name="q_prep",
    )(p, g.reshape(1, -1), wq, cos, sin)


def _kv_prep_body(ckv_ref, kr_ref, krsw_ref, g_ref, cos_ref, sin_ref, w_ref, rows_ref, kv_ref, krp_ref):
    cn = _rms(ckv_ref[...], g_ref[...])
    kr = kr_ref[...] * cos_ref[...] + krsw_ref[...] * sin_ref[...]
    rows_ref[:, :MLA_KV_RANK] = cn
    rows_ref[:, MLA_KV_RANK:] = kr[:, :MLA_ROPE]
    krp_ref[...] = kr.astype(BF16)
    kv_ref[...] = _dot(cn.astype(BF16), w_ref[...].astype(BF16)).astype(BF16)


def kv_prep(p, g, w_ukv, cos, sin, *, tm):
    t = p.shape[0]
    nkv = w_ukv.shape[1]
    return pl.pallas_call(
        _kv_prep_body,
        grid=(t // tm,),
        in_specs=[
            pl.BlockSpec((tm, MLA_KV_RANK), lambda i: (i, P_CKV // MLA_KV_RANK)),
            pl.BlockSpec((tm, LANE), lambda i: (i, P_KR // LANE)),
            pl.BlockSpec((tm, LANE), lambda i: (i, P_KRSW // LANE)),
            pl.BlockSpec((1, MLA_KV_RANK), lambda i: (0, 0)),
            pl.BlockSpec((tm, LANE), lambda i: (i, 0)),
            pl.BlockSpec((tm, LANE), lambda i: (i, 0)),
            pl.BlockSpec(w_ukv.shape, lambda i: (0, 0)),
        ],
        out_specs=[
            pl.BlockSpec((tm, ROW_W), lambda i: (i, 0)),
            pl.BlockSpec((tm, nkv), lambda i: (i, 0)),
            pl.BlockSpec((tm, LANE), lambda i: (i, 0)),
        ],
        out_shape=[
            jax.ShapeDtypeStruct((t, ROW_W), F32),
            jax.ShapeDtypeStruct((t, nkv), BF16),
            jax.ShapeDtypeStruct((t, LANE), BF16),
        ],
        compiler_params=_cparams(("parallel",)),
        name="kv_prep",
    )(p, p, p, g.reshape(1, -1), cos, sin, w_ukv)


def _attn_body(qn_ref, qr_ref, kn_ref, kr_ref, v_ref, o_ref, m_ref, l_ref, acc_ref, *, tq):
    qi = pl.program_id(2)
    m_ref[...] = jnp.full(m_ref.shape, -jnp.inf, F32)
    l_ref[...] = jnp.zeros(l_ref.shape, F32)
    acc_ref[...] = jnp.zeros(acc_ref.shape, F32)
    q = jnp.concatenate([qn_ref[...], qr_ref[...]], axis=1)

    def block(ki, on_diagonal):
        rows = pl.ds(pl.multiple_of(ki * tq, tq), tq)
        k = jnp.concatenate([kn_ref[rows, :], kr_ref[rows, :]], axis=1)
        s = _dot_nt(q, k)
        if on_diagonal:
            causal = (lax.broadcasted_iota(jnp.int32, (tq, tq), 1) <= lax.broadcasted_iota(jnp.int32, (tq, tq), 0))
            s = jnp.where(causal, s, -jnp.inf)
        m_old = m_ref[...]
        m_new = jnp.maximum(m_old, jnp.max(s, axis=-1, keepdims=True))
        corr = jnp.exp(m_old - m_new)
        p = jnp.exp(s - m_new)
        l_ref[...] = l_ref[...] * corr + jnp.sum(p, axis=-1, keepdims=True)
        acc_ref[...] = acc_ref[...] * corr + _dot(p.astype(BF16), v_ref[rows, :])
        m_ref[...] = m_new

    def below_diagonal(ki, carry):
        block(ki, False)
        return carry

    lax.fori_loop(0, qi, below_diagonal, 0)
    block(qi, True)
    o_ref[...] = (acc_ref[...] / l_ref[...]).astype(o_ref.dtype)


def mla_prompt_attention(qn, qr, kv, krp, *, bsz, seq, tq):
    nq = seq // tq
    return pl.pallas_call(
        functools.partial(_attn_body, tq=tq),
        grid=(bsz, MLA_HEADS, nq),
        in_specs=[
            pl.BlockSpec((tq, LANE), lambda b, h, qi: (b * nq + qi, h)),
            pl.BlockSpec((tq, LANE), lambda b, h, qi: (b * nq + qi, h)),
            pl.BlockSpec((seq, LANE), lambda b, h, qi: (b, 2 * h)),
            pl.BlockSpec((seq, LANE), lambda b, h, qi: (b, 0)),
            pl.BlockSpec((seq, LANE), lambda b, h, qi: (b, 2 * h + 1)),
        ],
        out_specs=pl.BlockSpec((tq, LANE), lambda b, h, qi: (b * nq + qi, h)),
        out_shape=jax.ShapeDtypeStruct((bsz * seq, MLA_HEADS * MLA_V), BF16),
        scratch_shapes=[pltpu.VMEM((tq, 1), F32), pltpu.VMEM((tq, 1), F32), pltpu.VMEM((tq, MLA_V), F32)],
        compiler_params=_cparams(("parallel", "parallel", "arbitrary")),
        name="mla_prompt_attention",
    )(qn, qr, kv, krp, kv)


def _paged_body(pt_ref, qn_ref, qr_ref, rows_ref, w_ref, *rest, gp, n_valid):
    page_refs = rest[:gp]
    o_ref, ql_ref, qrr_ref, m_ref, l_ref, acc_ref = rest[gp:]
    del pt_ref
    g = pl.program_id(1)
    rows_q = MLA_HEADS * SAMPLE_PAD
    kvw = MLA_NOPE + MLA_V

    @pl.when(g == 0)
    def _():
        for h in range(MLA_HEADS):
            w_uk = w_ref[:, h * kvw: h * kvw + MLA_NOPE]
            ql = _dot_nt(qn_ref[:, h * LANE:(h + 1) * LANE], w_uk)
            ql_ref[h * SAMPLE_PAD:(h + 1) * SAMPLE_PAD, :] = ql.astype(BF16)
            qrr_ref[h * SAMPLE_PAD:(h + 1) * SAMPLE_PAD, :] = qr_ref[:, h * LANE:(h + 1) * LANE]
        m_ref[...] = jnp.full(m_ref.shape, -jnp.inf, F32)
        l_ref[...] = jnp.zeros(l_ref.shape, F32)
        acc_ref[...] = jnp.zeros(acc_ref.shape, F32)

    ql = ql_ref[...]
    qrr = qrr_ref[...][:, :MLA_ROPE]

    def accumulate(s, pv):
        m_old = m_ref[...]
        m_new = jnp.maximum(m_old, jnp.max(s, axis=-1, keepdims=True))
        corr = jnp.exp(m_old - m_new)
        p = jnp.exp(s - m_new)
        l_ref[...] = l_ref[...] * corr + jnp.sum(p, axis=-1, keepdims=True)
        acc_ref[...] = acc_ref[...] * corr + pv(p.astype(BF16))
        m_ref[...] = m_new

    lat_t = jnp.concatenate([pr[:MLA_KV_RANK, :].astype(BF16) for pr in page_refs], axis=1)
    rk_t = jnp.concatenate([pr[MLA_KV_RANK:, :].astype(BF16) for pr in page_refs], axis=1)
    accumulate(_dot(ql, lat_t) + _dot(qrr, rk_t), lambda p: _dot_nt(p, lat_t))

    @pl.when(g == pl.num_programs(1) - 1)
    def _():
        lat = rows_ref[:, :MLA_KV_RANK].astype(BF16)
        rk = rows_ref[:, MLA_KV_RANK:].astype(BF16)
        s = _dot_nt(ql, lat) + _dot_nt(qrr, rk)
        tq = lax.broadcasted_iota(jnp.int32, (rows_q, SAMPLE_PAD), 0) % SAMPLE_PAD
        tk = lax.broadcasted_iota(jnp.int32, (rows_q, SAMPLE_PAD), 1)
        s = jnp.where((tk <= tq) & (tk < n_valid), s, -jnp.inf)
        accumulate(s, lambda p: _dot(p, lat))
        out_lat = (acc_ref[...] / l_ref[...]).astype(BF16)
        for h in range(MLA_HEADS):
            w_uv = w_ref[:, h * kvw + MLA_NOPE:(h + 1) * kvw]
            o = _dot(out_lat[h * SAMPLE_PAD:(h + 1) * SAMPLE_PAD, :], w_uv)
            o_ref[:, h * LANE:(h + 1) * LANE] = o.astype(o_ref.dtype)


def mla_paged_attention(qn, qr, rows, w_ukv, cache_t, page_table, *, row0, n_valid, gp):
    bsz, n_pages = page_table.shape
    blk0 = row0 // SAMPLE_PAD
    nw = MLA_HEADS * LANE
    npg = n_pages // gp

    def page_spec(i):
        return pl.BlockSpec((None, ROW_W, PAGE_SIZE),
                            lambda b, g, pt: (pt[b * n_pages + g * gp + i], 0, 0))

    grid_spec = pltpu.PrefetchScalarGridSpec(
        num_scalar_prefetch=1,
        grid=(bsz, npg),
        in_specs=[
            pl.BlockSpec((SAMPLE_PAD, nw), lambda b, g, pt: (blk0 + b, 0)),
            pl.BlockSpec((SAMPLE_PAD, nw), lambda b, g, pt: (blk0 + b, 0)),
            pl.BlockSpec((SAMPLE_PAD, ROW_W), lambda b, g, pt: (blk0 + b, 0)),
            pl.BlockSpec(w_ukv.shape, lambda b, g, pt: (0, 0)),
        ] + [page_spec(i) for i in range(gp)],
        out_specs=pl.BlockSpec((SAMPLE_PAD, nw), lambda b, g, pt: (b, 0)),
        scratch_shapes=[
            pltpu.VMEM((MLA_HEADS * SAMPLE_PAD, MLA_KV_RANK), BF16),
            pltpu.VMEM((MLA_HEADS * SAMPLE_PAD, LANE), BF16),
            pltpu.VMEM((MLA_HEADS * SAMPLE_PAD, 1), F32),
            pltpu.VMEM((MLA_HEADS * SAMPLE_PAD, 1), F32),
            pltpu.VMEM((MLA_HEADS * SAMPLE_PAD, MLA_KV_RANK), F32),
        ],
    )
    return pl.pallas_call(
        functools.partial(_paged_body, gp=gp, n_valid=n_valid),
        grid_spec=grid_spec,
        out_shape=jax.ShapeDtypeStruct((bsz * SAMPLE_PAD, nw), BF16),
        compiler_params=_cparams(("parallel", "arbitrary")),
        name="mla_paged_attention",
    )(page_table.reshape(-1), qn, qr, rows, w_ukv, *([cache_t] * gp))


def _mlstm_body(q_ref, k_ref, v_ref, og_ref, gc_ref, gr_ref, bc_ref, br_ref, ng_ref, c0_ref, n0_ref, m0_ref,
                h_ref, c_ref, n_ref, m_ref, *, lc, n_valid):
    ci = pl.program_id(1)

    @pl.when(ci == 0)
    def _():
        c_ref[...] = c0_ref[...]
        n_ref[...] = n0_ref[...]
        m_ref[...] = m0_ref[...]

    nh = ML_HEADS
    gc = gc_ref[...] + bc_ref[...]
    gr = gr_ref[...] + br_ref[:, 0:1]
    li_c, lf_c = gc[:, 0:nh], _log_sigmoid(gc[:, nh:2 * nh])
    li_r, lf_r = gr[0:nh, :], _log_sigmoid(gr[nh:2 * nh, :])
    if n_valid < lc:
        vc = lax.broadcasted_iota(jnp.int32, (lc, nh), 0) < n_valid
        vr = lax.broadcasted_iota(jnp.int32, (nh, lc), 1) < n_valid
        li_c, lf_c = jnp.where(vc, li_c, NEG_BIG), jnp.where(vc, lf_c, 0.0)
        li_r, lf_r = jnp.where(vr, li_r, NEG_BIG), jnp.where(vr, lf_r, 0.0)
    row = lax.broadcasted_iota(jnp.int32, (lc, lc), 0)
    col = lax.broadcasted_iota(jnp.int32, (lc, lc), 1)
    causal = col <= row
    tril = causal.astype(F32)
    triu = (row <= col).astype(F32)
    b_c = jnp.dot(tril, lf_c, precision=HIGHEST, preferred_element_type=F32)
    b_r = jnp.dot(lf_r, triu, precision=HIGHEST, preferred_element_type=F32)

    for h in range(nh):
        bc, br = b_c[:, h:h + 1], b_r[h:h + 1, :]
        ic, ir = li_c[:, h:h + 1], li_r[h:h + 1, :]
        m_prev = m_ref[h:h + 1, 0:1]
        qh = (q_ref[:, h * ML_DQK:(h + 1) * ML_DQK] * (ML_DQK ** -0.5))
        qb = qh.astype(BF16)
        kh = k_ref[:, h * ML_DQK:(h + 1) * ML_DQK]
        kb = kh.astype(BF16)
        vh = v_ref[:, h * ML_DV:(h + 1) * ML_DV]
        c_h = c_ref[h]
        n_h = n_ref[h:h + 1, :]

        dmat = jnp.where(causal, bc - br + ir, -jnp.inf)
        inter = bc + m_prev
        m_t = jnp.maximum(inter, jnp.max(dmat, axis=-1, keepdims=True))
        w_inter = jnp.exp(inter - m_t)
        s = _dot_nt(qb, kb) * jnp.exp(dmat - m_t)
        num = _dot(s.astype(BF16), vh.astype(BF16)) + w_inter * _dot_nt(qb, c_h.astype(BF16))
        den = jnp.sum(s, axis=-1, keepdims=True) + w_inter * jnp.sum(qh * n_h, axis=-1, keepdims=True)
        hout = num / jnp.maximum(jnp.abs(den), jnp.exp(-m_t))
        hn = _rms(hout, ng_ref[:, h * ML_DV:(h + 1) * ML_DV])
        og = og_ref[:, h * ML_DV:(h + 1) * ML_DV]
        h_ref[:, h * ML_DV:(h + 1) * ML_DV] = (_sigmoid(og) * hn).astype(h_ref.dtype)

        b_last = bc[lc - 1:lc, :]
        g_c = b_last - bc + ic
        g_r = b_last - br + ir
        m_new = jnp.maximum(b_last + m_prev, jnp.max(g_r, axis=-1, keepdims=True))
        decay = jnp.exp(b_last + m_prev - m_new)
        wg = jnp.exp(g_c - m_new)
        c_ref[h] = decay * c_h + _dot_tn((vh * wg).astype(BF16), kb)
        n_ref[h:h + 1, :] = decay * n_h + jnp.sum(kh * wg, axis=0, keepdims=True)
        m_ref[h:h + 1, :] = jnp.broadcast_to(m_new, (1, LANE))


def mlstm(p, gates_rows, b_ig, b_fg, norm_g, c0, n0, m0, *, bsz, row0, lc, nchunk, n_valid):
    blk0 = row0 // lc
    nh = ML_HEADS
    bias_c = jnp.concatenate([b_ig, b_fg, jnp.zeros((LANE - 2 * nh,), F32)]).reshape(1, LANE)
    bias_r = jnp.broadcast_to(jnp.concatenate([b_ig, b_fg]).reshape(2 * nh, 1), (2 * nh, LANE))

    def rows(cb):
        return lambda b, c: (blk0 + b * nchunk + c, cb)

    const2 = lambda b, c: (0, 0)
    return pl.pallas_call(
        functools.partial(_mlstm_body, lc=lc, n_valid=n_valid),
        grid=(bsz, nchunk),
        in_specs=[
            pl.BlockSpec((lc, nh * ML_DQK), rows(P_Q // (nh * ML_DQK))),
            pl.BlockSpec((lc, nh * ML_DQK), rows(P_K // (nh * ML_DQK))),
            pl.BlockSpec((lc, nh * ML_DV), rows(P_V // (nh * ML_DV))),
            pl.BlockSpec((lc, nh * ML_DV), rows(P_OG // (nh * ML_DV))),
            pl.BlockSpec((lc, LANE), rows(P_GATE // LANE)),
            pl.BlockSpec((None, 2 * nh, lc), lambda b, c: (b * nchunk + c, 0, 0)),
            pl.BlockSpec((1, LANE), const2),
            pl.BlockSpec((2 * nh, LANE), const2),
            pl.BlockSpec((1, nh * ML_DV), const2),
            pl.BlockSpec((None, nh, ML_DV, ML_DQK), lambda b, c: (b, 0, 0, 0)),
            pl.BlockSpec((None, nh, ML_DQK), lambda b, c: (b, 0, 0)),
            pl.BlockSpec((None, 2 * nh, LANE), lambda b, c: (b, 0, 0)),
        ],
        out_specs=[
            pl.BlockSpec((lc, nh * ML_DV), lambda b, c: (b * nchunk + c, 0)),
            pl.BlockSpec((None, nh, ML_DV, ML_DQK), lambda b, c: (b, 0, 0, 0)),
            pl.BlockSpec((None, nh, ML_DQK), lambda b, c: (b, 0, 0)),
            pl.BlockSpec((None, 2 * nh, LANE), lambda b, c: (b, 0, 0)),
        ],
        out_shape=[
            jax.ShapeDtypeStruct((bsz * nchunk * lc, nh * ML_DV), BF16),
            jax.ShapeDtypeStruct((bsz, nh, ML_DV, ML_DQK), F32),
            jax.ShapeDtypeStruct((bsz, nh, ML_DQK), F32),
            jax.ShapeDtypeStruct((bsz, 2 * nh, LANE), F32),
        ],
        compiler_params=_cparams(("parallel", "arbitrary")),
        name="mlstm",
    )(p, p, p, p, p, gates_rows, bias_c, bias_r, norm_g.reshape(1, -1), c0, n0, m0)


def _gates_rows(p, row0, nrows, lc):
    g = lax.slice(p, (row0, P_GATE), (row0 + nrows, P_GATE + 2 * ML_HEADS))
    return g.reshape(nrows // lc, lc, 2 * ML_HEADS).transpose(0, 2, 1)


def _m_pack(m):
    b = m.shape[0]
    mp = jnp.pad(m.astype(F32), ((0, 0), (0, 2 * ML_HEADS - m.shape[1])))
    return jnp.broadcast_to(mp[:, :, None], (b, 2 * ML_HEADS, LANE))


def _ssd_body(z_ref, x_ref, b_ref, c_ref, dtc_ref, dtr_ref, cwx_ref, cwb_ref, cwc_ref, cbx_ref, cbb_ref, cbc_ref,
              csx_ref, csb_ref, csc_ref, pc_ref, pr_ref, dsk_ref, ng_ref, h0_ref,
              y_ref, h_ref, bufx, bufb, bufc, *, lc, n_valid, gs):
    for gi in range(gs):
        cx = slice(gi * SSM_GW, (gi + 1) * SSM_GW)
        cn = slice(gi * SSM_STATE, (gi + 1) * SSM_STATE)
        cl = slice(gi * LANE, (gi + 1) * LANE)
        _ssd_group(z_ref.at[:, cx], x_ref.at[:, cx], b_ref.at[:, cn], c_ref.at[:, cn], dtc_ref.at[:, cl],
                   dtr_ref.at[gi * SSM_HG:(gi + 1) * SSM_HG, :],
                   cwx_ref.at[:, cx], cwb_ref.at[:, cn], cwc_ref.at[:, cn],
                   cbx_ref.at[:, cx], cbb_ref.at[:, cn], cbc_ref.at[:, cn],
                   csx_ref.at[:, cx], csb_ref.at[:, cn], csc_ref.at[:, cn],
                   pc_ref.at[gi], pr_ref.at[gi], dsk_ref.at[:, cx], ng_ref.at[:, cx], h0_ref.at[gi],
                   y_ref.at[:, cx], h_ref.at[gi], bufx.at[gi], bufb.at[gi], bufc.at[gi],
                   lc=lc, n_valid=n_valid)


def _ssd_group(z_ref, x_ref, b_ref, c_ref, dtc_ref, dtr_ref, cwx_ref, cwb_ref, cwc_ref, cbx_ref, cbb_ref, cbc_ref,
               csx_ref, csb_ref, csc_ref, pc_ref, pr_ref, dsk_ref, ng_ref, h0_ref,
               y_ref, h_ref, bufx, bufb, bufc, *, lc, n_valid):
    ci = pl.program_id(2)
    tail = SAMPLE_PAD

    @pl.when(ci == 0)
    def _():
        h_ref[...] = h0_ref[...]
        bufx[0:tail, :] = csx_ref[...]
        bufb[0:tail, :] = csb_ref[...]
        bufc[0:tail, :] = csc_ref[...]

    def conv(raw_ref, buf, cw_ref, cb_ref):
        raw = raw_ref[...]
        buf[tail:tail + lc, :] = raw
        acc = cb_ref[...] + cw_ref[SSM_CONV - 1:SSM_CONV, :] * raw
        for j in range(SSM_CONV - 1):
            off = tail - (SSM_CONV - 1) + j
            acc = acc + cw_ref[j:j + 1, :] * buf[off:off + lc, :]
        buf[0:tail, :] = raw[lc - tail:lc, :]
        return _silu(acc)

    xa = conv(x_ref, bufx, cwx_ref, cbx_ref)
    ba = conv(b_ref, bufb, cwb_ref, cbb_ref)
    ca = conv(c_ref, bufc, cwc_ref, cbc_ref)

    hg = SSM_HG
    dt_c = _softplus(dtc_ref[:, 0:hg] + pc_ref[0:1, 0:hg])
    dt_r = _softplus(dtr_ref[...] + pr_ref[:, 0:1])
    if n_valid < lc:
        dt_c = jnp.where(lax.broadcasted_iota(jnp.int32, (lc, hg), 0) < n_valid, dt_c, 0.0)
        dt_r = jnp.where(lax.broadcasted_iota(jnp.int32, (hg, lc), 1) < n_valid, dt_r, 0.0)
    a_c = dt_c * (-jnp.exp(pc_ref[1:2, 0:hg]))
    a_r = dt_r * (-jnp.exp(pr_ref[:, 1:2]))
    row = lax.broadcasted_iota(jnp.int32, (lc, lc), 0)
    col = lax.broadcasted_iota(jnp.int32, (lc, lc), 1)
    causal = col <= row
    cum_c = jnp.dot(causal.astype(F32), a_c, precision=HIGHEST, preferred_element_type=F32)
    cum_r = jnp.dot(a_r, (row <= col).astype(F32), precision=HIGHEST, preferred_element_type=F32)

    expand = (lax.broadcasted_iota(jnp.int32, (hg, SSM_GW), 1) // SSM_HEADDIM
              == lax.broadcasted_iota(jnp.int32, (hg, SSM_GW), 0)).astype(F32)
    cum_last_c = cum_c[lc - 1:lc, :]
    e_in = jnp.dot(jnp.exp(cum_c), expand, precision=HIGHEST, preferred_element_type=F32)
    e_st = jnp.dot(jnp.exp(cum_last_c - cum_c) * dt_c, expand, precision=HIGHEST, preferred_element_type=F32)

    cb16 = ca.astype(BF16)
    bb16 = ba.astype(BF16)
    cbm = _dot_nt(cb16, bb16)
    h_old = h_ref[...]
    y = _dot_nt(cb16, h_old.astype(BF16)) * e_in
    lane_head = lax.broadcasted_iota(jnp.int32, (lc, LANE), 1) // SSM_HEADDIM
    pieces = []
    for pair in range(hg // 2):
        xp = xa[:, pair * LANE:(pair + 1) * LANE]
        yp = jnp.zeros((lc, LANE), F32)
        for sub in range(2):
            j = 2 * pair + sub
            seg = jnp.where(causal, cum_c[:, j:j + 1] - cum_r[j:j + 1, :], -jnp.inf)
            w = cbm * jnp.exp(seg) * dt_r[j:j + 1, :]
            xm = jnp.where(lane_head == sub, xp, 0.0)
            yp = yp + _dot(w.astype(BF16), xm.astype(BF16))
        pieces.append(yp)
    y = y + jnp.concatenate(pieces, axis=1)

    upd = _dot_tn((xa * e_st).astype(BF16), bb16)
    dec_r = jnp.exp(cum_r[:, lc - 1:lc])
    for j in range(hg):
        sl = slice(j * SSM_HEADDIM, (j + 1) * SSM_HEADDIM)
        h_ref[sl, :] = dec_r[j:j + 1, :] * h_old[sl, :] + upd[sl, :]

    y = y + dsk_ref[...] * xa
    y = y * _silu(z_ref[...])
    y_ref[...] = _rms(y, ng_ref[...]).astype(y_ref.dtype)


def ssd(zp, dt_rows, conv_w, conv_b, conv_state8, dt_bias, a_log, d_skip, norm_g, h0, *, bsz, row0, lc, nchunk,
        n_valid, gs):
    blk0 = row0 // lc
    g_, hg = SSM_GROUPS // gs, SSM_HG * gs
    gw, ns, lw = SSM_GW * gs, SSM_STATE * gs, LANE * gs

    def rows(off, width):
        assert off % width == 0
        return lambda b, g, c: (blk0 + b * nchunk + c, off // width + g)

    ng_all, nh1 = SSM_GROUPS, SSM_HG
    pcol = jnp.zeros((ng_all, 8, LANE), F32)
    pcol = pcol.at[:, 0, :nh1].set(dt_bias.reshape(ng_all, nh1)).at[:, 1, :nh1].set(a_log.reshape(ng_all, nh1))
    prow = jnp.zeros((ng_all, nh1, LANE), F32)
    prow = prow.at[:, :, 0].set(dt_bias.reshape(ng_all, nh1)).at[:, :, 1].set(a_log.reshape(ng_all, nh1))
    dsk = jnp.repeat(d_skip, SSM_HEADDIM).reshape(1, SSM_D_INNER)
    cb2 = conv_b.reshape(1, -1)
    bo, co = SSM_D_INNER // ns, SSM_D_INNER // ns + g_
    assert SSM_D_INNER % ns == 0

    in_specs = [
        pl.BlockSpec((lc, gw), rows(0, gw)),
        pl.BlockSpec((lc, gw), rows(Z_X, gw)),
        pl.BlockSpec((lc, ns), rows(Z_B, ns)),
        pl.BlockSpec((lc, ns), rows(Z_C, ns)),
        pl.BlockSpec((lc, lw), rows(Z_DT, lw)),
        pl.BlockSpec((None, hg, lc), lambda b, g, c: (b * nchunk + c, g, 0)),
        pl.BlockSpec((SSM_CONV, gw), lambda b, g, c: (0, g)),
        pl.BlockSpec((SSM_CONV, ns), lambda b, g, c: (0, bo + g)),
        pl.BlockSpec((SSM_CONV, ns), lambda b, g, c: (0, co + g)),
        pl.BlockSpec((1, gw), lambda b, g, c: (0, g)),
        pl.BlockSpec((1, ns), lambda b, g, c: (0, bo + g)),
        pl.BlockSpec((1, ns), lambda b, g, c: (0, co + g)),
        pl.BlockSpec((None, 8, gw), lambda b, g, c: (b, 0, g)),
        pl.BlockSpec((None, 8, ns), lambda b, g, c: (b, 0, bo + g)),
        pl.BlockSpec((None, 8, ns), lambda b, g, c: (b, 0, co + g)),
        pl.BlockSpec((gs, 8, LANE), lambda b, g, c: (g, 0, 0)),
        pl.BlockSpec((gs, nh1, LANE), lambda b, g, c: (g, 0, 0)),
        pl.BlockSpec((1, gw), lambda b, g, c: (0, g)),
        pl.BlockSpec((1, gw), lambda b, g, c: (0, g)),
        pl.BlockSpec((None, gs, SSM_GW, SSM_STATE), lambda b, g, c: (b, g, 0, 0)),
    ]
    return pl.pallas_call(
        functools.partial(_ssd_body, lc=lc, n_valid=n_valid, gs=gs),
        grid=(bsz, g_, nchunk),
        in_specs=in_specs,
        out_specs=[
            pl.BlockSpec((lc, gw), lambda b, g, c: (b * nchunk + c, g)),
            pl.BlockSpec((None, gs, SSM_GW, SSM_STATE), lambda b, g, c: (b, g, 0, 0)),
        ],
        out_shape=[
            jax.ShapeDtypeStruct((bsz * nchunk * lc, SSM_D_INNER), BF16),
            jax.ShapeDtypeStruct((bsz, SSM_GROUPS, SSM_GW, SSM_STATE), F32),
        ],
        scratch_shapes=[
            pltpu.VMEM((gs, lc + SAMPLE_PAD, SSM_GW), F32),
            pltpu.VMEM((gs, lc + SAMPLE_PAD, SSM_STATE), F32),
            pltpu.VMEM((gs, lc + SAMPLE_PAD, SSM_STATE), F32),
        ],
        compiler_params=_cparams(("parallel", "parallel", "arbitrary")),
        name="ssd",
    )(zp, zp, zp, zp, zp, dt_rows, conv_w, conv_w, conv_w, cb2, cb2, cb2, conv_state8, conv_state8, conv_state8,
      pcol, prow, dsk, norm_g.reshape(1, -1), h0)


def _dt_rows(zp, row0, nrows, lc):
    dt = lax.slice(zp, (row0, Z_DT), (row0 + nrows, Z_WIDTH))
    dt = dt.reshape(nrows, SSM_GROUPS, LANE)[:, :, :SSM_HG].reshape(nrows // lc, lc, SSM_HEADS)
    return dt.transpose(0, 2, 1)


def _conv_state8(state):
    return jnp.pad(state, ((0, 0), (SAMPLE_PAD - (SSM_CONV - 1), 0), (0, 0)))


def _trunk(x_prompt, x_sample, cache_mla, state_mlstm_C, state_mlstm_n, state_mlstm_m, state_ssm, state_conv,
           page_table, norm_g, ffn_w_gate, ffn_w_up, ffn_w_down, w_in_even, b_ig, b_fg, mlstm_norm_g, mla_q_norm_g,
           mla_kv_norm_g, w_uq, w_ukv, w_out_even, w_in_ssm, conv_w, conv_b, dt_bias, A_log, D_skip, ssm_norm_g,
           w_out_ssm, final_norm_g, *, tm, tf, tn, tmp, lc_ml, lc_ssd, gs_p, gs_s, tq, gp):
    bp, sp, d = x_prompt.shape
    bs, ts, _ = x_sample.shape
    tp = bp * sp
    tsp = bs * SAMPLE_PAD
    x = jnp.concatenate([x_prompt.reshape(tp, d),
                         jnp.pad(x_sample, ((0, 0), (0, SAMPLE_PAD - ts), (0, 0))).reshape(tsp, d)], axis=0)

    pos_p = jnp.tile(jnp.arange(sp), bp)
    pos_s = jnp.tile(PAST_LEN + jnp.arange(SAMPLE_PAD), bs)
    cos, sin = _rope_tables(jnp.concatenate([pos_p, pos_s]))

    mm = functools.partial(matmul_residual, tm=tm, tn=tn)
    ff = functools.partial(ffn, w_gate=ffn_w_gate.astype(BF16), w_up=ffn_w_up.astype(BF16),
                           w_down=ffn_w_down.astype(BF16), tm=tm, tf=tf)

    x = ff(x, norm_g[0, 0], layer=0, which=0)
    p = norm_matmul(x, norm_g[0, 1], _prep_w_in_even(w_in_even), tm=tm, tn=tn)

    zeros = functools.partial(jnp.zeros, dtype=F32)
    hm_p, c_p, n_p, m_p = mlstm(
        p, _gates_rows(p, 0, tp, lc_ml), b_ig, b_fg, mlstm_norm_g,
        zeros((bp, ML_HEADS, ML_DV, ML_DQK)), zeros((bp, ML_HEADS, ML_DQK)), zeros((bp, 2 * ML_HEADS, LANE)),
        bsz=bp, row0=0, lc=lc_ml, nchunk=sp // lc_ml, n_valid=lc_ml)
    hm_s, c_s, n_s, m_s = mlstm(
        p, _gates_rows(p, tp, tsp, SAMPLE_PAD), b_ig, b_fg, mlstm_norm_g,
        state_mlstm_C, state_mlstm_n, _m_pack(state_mlstm_m),
        bsz=bs, row0=tp, lc=SAMPLE_PAD, nchunk=1, n_valid=ts)

    w_ukv16 = w_ukv.astype(BF16)
    qn, qr = q_prep(p, mla_q_norm_g, _prep_w_uq(w_uq), cos, sin, tm=tmp)
    rows, kv, krp = kv_prep(p, mla_kv_norm_g, w_ukv16, cos, sin, tm=tmp)
    ha_p = mla_prompt_attention(qn, qr, kv, krp, bsz=bp, seq=sp, tq=tq)
    cache_t = jnp.swapaxes(cache_mla, 1, 2)
    ha_s = mla_paged_attention(qn, qr, rows, w_ukv16, cache_t, page_table, row0=tp, n_valid=ts, gp=gp)

    hm = jnp.concatenate([hm_p, hm_s], axis=0)
    ha = jnp.concatenate([ha_p, ha_s], axis=0)
    x = mm([hm, ha], w_out_even.astype(BF16), x)
    x = ff(x, norm_g[0, 2], layer=0, which=1)

    x = ff(x, norm_g[1, 0], layer=1, which=0)
    zp = norm_matmul(x, norm_g[1, 1], _prep_w_in_ssm(w_in_ssm), tm=tm, tn=tn)
    y_p, ssm_p = ssd(
        zp, _dt_rows(zp, 0, tp, lc_ssd), conv_w, conv_b, zeros((bp, SAMPLE_PAD, SSM_CONV_DIM)), dt_bias, A_log,
        D_skip, ssm_norm_g, zeros((bp, SSM_GROUPS, SSM_GW, SSM_STATE)),
        bsz=bp, row0=0, lc=lc_ssd, nchunk=sp // lc_ssd, n_valid=lc_ssd, gs=gs_p)
    y_s, ssm_s = ssd(
        zp, _dt_rows(zp, tp, tsp, SAMPLE_PAD), conv_w, conv_b, _conv_state8(state_conv), dt_bias, A_log,
        D_skip, ssm_norm_g, state_ssm.reshape(bs, SSM_GROUPS, SSM_GW, SSM_STATE),
        bsz=bs, row0=tp, lc=SAMPLE_PAD, nchunk=1, n_valid=ts, gs=gs_s)
    x = mm([jnp.concatenate([y_p, y_s], axis=0)], w_out_ssm.astype(BF16), x)
    x = ff(x, norm_g[1, 2], layer=1, which=1, final_g=final_norm_g)

    def split_rows(a):
        w = a.shape[1]
        return a[:tp].reshape(bp, sp, w), a[tp:].reshape(bs, SAMPLE_PAD, w)[:, :ts]

    y_prompt, y_sample = split_rows(x)
    rows_p, rows_s = split_rows(rows)
    keep = SSM_CONV - 1
    assert sp >= keep and ts >= keep
    conv_p = jnp.stack([lax.slice(zp, (b * sp + sp - keep, Z_X), (b * sp + sp, Z_DT)) for b in range(bp)])
    conv_s = lax.slice(zp, (tp, Z_X), (tp + tsp, Z_DT)).reshape(bs, SAMPLE_PAD, SSM_CONV_DIM)[:, ts - keep:ts]
    hshape = (SSM_HEADS, SSM_HEADDIM, SSM_STATE)
    return (y_prompt, y_sample, rows_p, rows_s, c_p, c_s, n_p, n_s,
            m_p[:, :ML_HEADS, 0], m_s[:, :ML_HEADS, 0],
            ssm_p.reshape((bp,) + hshape), ssm_s.reshape((bs,) + hshape), conv_p, conv_s)


def kernel(x_prompt, x_sample, cache_mla, state_mlstm_C, state_mlstm_n, state_mlstm_m, state_ssm, state_conv, page_table, norm_g, ffn_w_gate, ffn_w_up, ffn_w_down, w_in_even, b_ig, b_fg, mlstm_norm_g, mla_q_norm_g, mla_kv_norm_g, w_uq, w_ukv, w_out_even, w_in_ssm, conv_w, conv_b, dt_bias, A_log, D_skip, ssm_norm_g, w_out_ssm, final_norm_g):
    return _trunk(x_prompt, x_sample, cache_mla, state_mlstm_C, state_mlstm_n, state_mlstm_m, state_ssm, state_conv,
                  page_table, norm_g, ffn_w_gate, ffn_w_up, ffn_w_down, w_in_even, b_ig, b_fg, mlstm_norm_g,
                  mla_q_norm_g, mla_kv_norm_g, w_uq, w_ukv, w_out_even, w_in_ssm, conv_w, conv_b, dt_bias, A_log,
                  D_skip, ssm_norm_g, w_out_ssm, final_norm_g,
                  tm=1024, tf=512, tn=512, tmp=512, lc_ml=256, lc_ssd=128, gs_p=1, gs_s=8, tq=512, gp=16)
```

```python
import functools
import math

import jax
import jax.numpy as jnp
from jax import lax
from jax.experimental import pallas as pl
from jax.experimental.pallas import tpu as pltpu

F32 = jnp.float32
BF16 = jnp.bfloat16
HIGHEST = lax.Precision.HIGHEST

D_MODEL = 2048
D_FF = 5632
EPS = 1e-6
PAST_LEN = 8192
PAGE_SIZE = 128

ML_HEADS = 4
ML_DQK = 128
ML_DV = 256

MLA_HEADS = 8
MLA_NOPE = 128
MLA_ROPE = 64
MLA_V = 128
MLA_Q_RANK = 512
MLA_KV_RANK = 512
MLA_SCALE = (MLA_NOPE + MLA_ROPE) ** -0.5
ROPE_BASE = 10000.0
ROW_W = MLA_KV_RANK + MLA_ROPE

SSM_D_INNER = 4096
SSM_HEADDIM = 64
SSM_HEADS = 64
SSM_GROUPS = 8
SSM_HG = SSM_HEADS // SSM_GROUPS
SSM_GW = SSM_D_INNER // SSM_GROUPS
SSM_STATE = 128
SSM_CONV = 4
SSM_CONV_DIM = SSM_D_INNER + 2 * SSM_GROUPS * SSM_STATE

SAMPLE_PAD = 8
LANE = 128
VMEM_LIMIT = 56 * 1024 * 1024

P_Q, P_K, P_V, P_OG, P_CQ, P_CKV = 0, 512, 1024, 2048, 3072, 3584
P_KR, P_KRSW, P_GATE, P_WIDTH = 4096, 4224, 4352, 4608
Z_X, Z_B, Z_C, Z_DT, Z_WIDTH = 4096, 8192, 9216, 10240, 11264

NEG_BIG = -1e30


def _cparams(sem):
    return pltpu.CompilerParams(dimension_semantics=sem, vmem_limit_bytes=VMEM_LIMIT)


def _rms(x, g):
    return x * lax.rsqrt(jnp.mean(x * x, axis=-1, keepdims=True) + EPS) * g


def _sigmoid(x):
    return 1.0 / (1.0 + jnp.exp(-x))


def _silu(x):
    return x * _sigmoid(x)


def _log_sigmoid(x):
    return jnp.minimum(x, 0.0) - jnp.log1p(jnp.exp(-jnp.abs(x)))


def _softplus(x):
    return jnp.maximum(x, 0.0) + jnp.log1p(jnp.exp(-jnp.abs(x)))


def _dot(a, b):
    return jnp.dot(a, b, preferred_element_type=F32)


def _dot_nt(a, b):
    return lax.dot_general(a, b, (((1,), (1,)), ((), ())), preferred_element_type=F32)


def _dot_tn(a, b):
    return lax.dot_general(a, b, (((0,), (0,)), ((), ())), preferred_element_type=F32)


def _ffn_body(x_ref, g_ref, wg_ref, wu_ref, wd_ref, *rest, final_norm):
    if final_norm:
        gf_ref, o_ref, xn_ref = rest
    else:
        o_ref, xn_ref = rest
    j = pl.program_id(1)

    @pl.when(j == 0)
    def _():
        x = x_ref[...]
        xn_ref[...] = _rms(x, g_ref[...]).astype(BF16)
        o_ref[...] = x

    xn = xn_ref[...]
    hg = _dot(xn, wg_ref[...].astype(BF16))
    hu = _dot(xn, wu_ref[...].astype(BF16))
    h = _silu(hg) * (0.5 * hu)
    o_ref[...] += _dot(h.astype(BF16), wd_ref[...].astype(BF16))

    if final_norm:
        @pl.when(j == pl.num_programs(1) - 1)
        def _():
            o_ref[...] = _rms(o_ref[...], gf_ref[...])


def ffn(x, g, w_gate, w_up, w_down, layer, which, *, tm, tf, final_g=None):
    t, d = x.shape
    dff = w_gate.shape[-1]
    final_norm = final_g is not None
    in_specs = [
        pl.BlockSpec((tm, d), lambda i, j: (i, 0), pipeline_mode=pl.Buffered(1)),
        pl.BlockSpec((1, d), lambda i, j: (0, 0)),
        pl.BlockSpec((None, None, d, tf), lambda i, j: (layer, which, 0, j)),
        pl.BlockSpec((None, None, d, tf), lambda i, j: (layer, which, 0, j)),
        pl.BlockSpec((None, None, tf, d), lambda i, j: (layer, which, j, 0)),
    ]
    args = [x, g.reshape(1, d), w_gate, w_up, w_down]
    if final_norm:
        in_specs.append(pl.BlockSpec((1, d), lambda i, j: (0, 0)))
        args.append(final_g.reshape(1, d))
    return pl.pallas_call(
        functools.partial(_ffn_body, final_norm=final_norm),
        grid=(t // tm, dff // tf),
        in_specs=in_specs,
        out_specs=pl.BlockSpec((tm, d), lambda i, j: (i, 0)),
        out_shape=jax.ShapeDtypeStruct((t, d), F32),
        scratch_shapes=[pltpu.VMEM((tm, d), BF16)],
        compiler_params=_cparams(("parallel", "arbitrary")),
        name="ffn",
    )(*args)


def _norm_mm_body(x_ref, g_ref, w_ref, o_ref, xn_ref):
    @pl.when(pl.program_id(1) == 0)
    def _():
        xn_ref[...] = _rms(x_ref[...], g_ref[...]).astype(BF16)

    o_ref[...] = _dot(xn_ref[...], w_ref[...].astype(BF16))


def norm_matmul(x, g, w, *, tm, tn):
    t, d = x.shape
    n = w.shape[1]
    return pl.pallas_call(
        _norm_mm_body,
        grid=(t // tm, n // tn),
        in_specs=[
            pl.BlockSpec((tm, d), lambda i, j: (i, 0)),
            pl.BlockSpec((1, d), lambda i, j: (0, 0)),
            pl.BlockSpec((d, tn), lambda i, j: (0, j)),
        ],
        out_specs=pl.BlockSpec((tm, tn), lambda i, j: (i, j)),
        out_shape=jax.ShapeDtypeStruct((t, n), F32),
        scratch_shapes=[pltpu.VMEM((tm, d), BF16)],
        compiler_params=_cparams(("parallel", "arbitrary")),
        name="norm_matmul",
    )(x, g.reshape(1, d), w)


def _mm_res_body(*refs, n_a):
    a_refs, w_refs = refs[:n_a], refs[n_a:2 * n_a]
    res_ref, o_ref = refs[2 * n_a], refs[2 * n_a + 1]
    acc = res_ref[...]
    for a_ref, w_ref in zip(a_refs, w_refs):
        acc = acc + _dot(a_ref[...].astype(BF16), w_ref[...].astype(BF16))
    o_ref[...] = acc


def matmul_residual(a_list, w, res, *, tm, tn):
    n_a = len(a_list)
    t, ka = a_list[0].shape
    n = w.shape[1]
    in_specs = [pl.BlockSpec((tm, ka), lambda i, j: (i, 0)) for _ in a_list]
    in_specs += [pl.BlockSpec((ka, tn), lambda i, j, s=s: (s, j)) for s in range(n_a)]
    in_specs.append(pl.BlockSpec((tm, tn), lambda i, j: (i, j)))
    return pl.pallas_call(
        functools.partial(_mm_res_body, n_a=n_a),
        grid=(t // tm, n // tn),
        in_specs=in_specs,
        out_specs=pl.BlockSpec((tm, tn), lambda i, j: (i, j)),
        out_shape=jax.ShapeDtypeStruct((t, n), F32),
        compiler_params=_cparams(("parallel", "parallel")),
        name="matmul_residual",
    )(*a_list, *([w] * n_a), res)


def _swap_halves(w):
    half = w.shape[-1] // 2
    return jnp.concatenate([w[..., half:], w[..., :half]], axis=-1)


def _prep_w_in_even(w):
    d = w.shape[0]
    sizes = (512, 512, 1024, 4, 4, 1024, MLA_Q_RANK, MLA_KV_RANK, MLA_ROPE)
    idx = [0]
    for s in sizes:
        idx.append(idx[-1] + s)
    q, k, v, ig, fg, og, cq, ckv, kr = [w[:, idx[i]:idx[i + 1]] for i in range(len(sizes))]
    z64 = jnp.zeros((d, LANE - MLA_ROPE), w.dtype)
    gates = jnp.concatenate([ig, fg, jnp.zeros((d, LANE - 2 * ML_HEADS), w.dtype)], axis=1)
    pad = jnp.zeros((d, P_WIDTH - P_GATE - LANE), w.dtype)
    return jnp.concatenate([q, k, v, og, cq, ckv, kr, z64, _swap_halves(kr), z64, gates, pad], axis=1).astype(BF16)


def _prep_w_uq(w):
    r = w.shape[0]
    w3 = w.reshape(r, MLA_HEADS, MLA_NOPE + MLA_ROPE)
    nope = w3[..., :MLA_NOPE].reshape(r, MLA_HEADS * MLA_NOPE)
    rope = w3[..., MLA_NOPE:]
    zpad = jnp.zeros((r, MLA_HEADS, LANE - MLA_ROPE), w.dtype)
    ra = jnp.concatenate([rope, zpad], axis=-1).reshape(r, MLA_HEADS * LANE)
    rb = jnp.concatenate([_swap_halves(rope), zpad], axis=-1).reshape(r, MLA_HEADS * LANE)
    return jnp.concatenate([nope, ra, rb], axis=1).astype(BF16)


def _prep_w_in_ssm(w):
    d = w.shape[0]
    zx = w[:, :Z_DT]
    dt = w[:, Z_DT:].reshape(d, SSM_GROUPS, SSM_HG)
    dt = jnp.pad(dt, ((0, 0), (0, 0), (0, LANE - SSM_HG))).reshape(d, SSM_GROUPS * LANE)
    return jnp.concatenate([zx, dt], axis=1).astype(BF16)


def _rope_tables(pos):
    half = MLA_ROPE // 2
    inv = ROPE_BASE ** (-jnp.arange(half, dtype=F32) / half)
    ang = pos.astype(F32)[:, None] * inv[None, :]
    c, s = jnp.cos(ang), jnp.sin(ang)
    z = jnp.zeros((pos.shape[0], LANE - MLA_ROPE), F32)
    return jnp.concatenate([c, c, z], axis=1), jnp.concatenate([-s, s, z], axis=1)


def _q_prep_body(cq_ref, g_ref, w_ref, cos_ref, sin_ref, qn_ref, qr_ref):
    cn = _rms(cq_ref[...], g_ref[...]).astype(BF16)
    a = _dot(cn, w_ref[...])
    nw = MLA_HEADS * MLA_NOPE
    qn_ref[...] = (a[:, :nw] * MLA_SCALE).astype(BF16)
    cos, sin = cos_ref[...], sin_ref[...]
    for h in range(MLA_HEADS):
        ra = a[:, nw + h * LANE: nw + (h + 1) * LANE]
        rb = a[:, 2 * nw + h * LANE: 2 * nw + (h + 1) * LANE]
        qr_ref[:, h * LANE:(h + 1) * LANE] = ((ra * cos + rb * sin) * MLA_SCALE).astype(BF16)


def q_prep(p, g, wq, cos, sin, *, tm):
    t = p.shape[0]
    nw = MLA_HEADS * LANE
    return pl.pallas_call(
        _q_prep_body,
        grid=(t // tm,),
        in_specs=[
            pl.BlockSpec((tm, MLA_Q_RANK), lambda i: (i, P_CQ // MLA_Q_RANK)),
            pl.BlockSpec((1, MLA_Q_RANK), lambda i: (0, 0)),
            pl.BlockSpec(wq.shape, lambda i: (0, 0)),
            pl.BlockSpec((tm, LANE), lambda i: (i, 0)),
            pl.BlockSpec((tm, LANE), lambda i: (i, 0)),
        ],
        out_specs=[pl.BlockSpec((tm, nw), lambda i: (i, 0)), pl.BlockSpec((tm, nw), lambda i: (i, 0))],
        out_shape=[jax.ShapeDtypeStruct((t, nw), BF16), jax.ShapeDtypeStruct((t, nw), BF16)],
        compiler_params=_cparams(("parallel",)),
        name="q_prep",
    )(p, g.reshape(1, -1), wq, cos, sin)


def _kv_prep_body(ckv_ref, kr_ref, krsw_ref, g_ref, cos_ref, sin_ref, *rest, expand):
    cn = _rms(ckv_ref[...], g_ref[...])
    kr = kr_ref[...] * cos_ref[...] + krsw_ref[...] * sin_ref[...]
    if expand:
        w_ref, rows_ref, kv_ref, krp_ref = rest
        krp_ref[...] = kr.astype(BF16)
        kv_ref[...] = _dot(cn.astype(BF16), w_ref[...]).astype(BF16)
    else:
        (rows_ref,) = rest
    rows_ref[:, :MLA_KV_RANK] = cn
    rows_ref[:, MLA_KV_RANK:] = kr[:, :MLA_ROPE]


def kv_prep(p, g, cos, sin, *, tm, w_ukv=None):
    t = p.shape[0]
    expand = w_ukv is not None
    in_specs = [
        pl.BlockSpec((tm, MLA_KV_RANK), lambda i: (i, P_CKV // MLA_KV_RANK)),
        pl.BlockSpec((tm, LANE), lambda i: (i, P_KR // LANE)),
        pl.BlockSpec((tm, LANE), lambda i: (i, P_KRSW // LANE)),
        pl.BlockSpec((1, MLA_KV_RANK), lambda i: (0, 0)),
        pl.BlockSpec((tm, LANE), lambda i: (i, 0)),
        pl.BlockSpec((tm, LANE), lambda i: (i, 0)),
    ]
    args = [p, p, p, g.reshape(1, -1), cos, sin]
    out_specs = [pl.BlockSpec((tm, ROW_W), lambda i: (i, 0))]
    out_shape = [jax.ShapeDtypeStruct((t, ROW_W), F32)]
    if expand:
        nkv = w_ukv.shape[1]
        in_specs.append(pl.BlockSpec(w_ukv.shape, lambda i: (0, 0)))
        args.append(w_ukv)
        out_specs += [pl.BlockSpec((tm, nkv), lambda i: (i, 0)), pl.BlockSpec((tm, LANE), lambda i: (i, 0))]
        out_shape += [jax.ShapeDtypeStruct((t, nkv), BF16), jax.ShapeDtypeStruct((t, LANE), BF16)]
    return pl.pallas_call(
        functools.partial(_kv_prep_body, expand=expand),
        grid=(t // tm,),
        in_specs=in_specs,
        out_specs=out_specs,
        out_shape=out_shape,
        compiler_params=_cparams(("parallel",)),
        name="kv_prep",
    )(*args)


def _attn_body(qn_ref, qr_ref, kv_ref, kr_ref, o_ref, m_ref, l_ref, acc_ref, *, tq, hp):
    qi = pl.program_id(2)
    m_ref[...] = jnp.full(m_ref.shape, -jnp.inf, F32)
    l_ref[...] = jnp.zeros(l_ref.shape, F32)
    acc_ref[...] = jnp.zeros(acc_ref.shape, F32)
    kvw = MLA_NOPE + MLA_V
    qs = [jnp.concatenate([qn_ref[:, j * LANE:(j + 1) * LANE], qr_ref[:, j * LANE:(j + 1) * LANE]], axis=1)
          for j in range(hp)]

    def block(ki, on_diagonal):
        rows = pl.ds(pl.multiple_of(ki * tq, tq), tq)
        kr = kr_ref[rows, :]
        hs = range(hp)
        s = [_dot_nt(qs[j], jnp.concatenate([kv_ref[rows, j * kvw:j * kvw + MLA_NOPE], kr], axis=1)) for j in hs]
        if on_diagonal:
            causal = (lax.broadcasted_iota(jnp.int32, (tq, tq), 1) <= lax.broadcasted_iota(jnp.int32, (tq, tq), 0))
            s = [jnp.where(causal, sj, -jnp.inf) for sj in s]
        m_old = [m_ref[j] for j in hs]
        m_new = [jnp.maximum(m_old[j], jnp.max(s[j], axis=-1, keepdims=True)) for j in hs]
        p = [jnp.exp(s[j] - m_new[j]) for j in hs]
        pv = [_dot(p[j].astype(BF16), kv_ref[rows, j * kvw + MLA_NOPE:(j + 1) * kvw]) for j in hs]
        corr = [jnp.exp(m_old[j] - m_new[j]) for j in hs]
        l_new = [l_ref[j] * corr[j] + jnp.sum(p[j], axis=-1, keepdims=True) for j in hs]
        acc_new = [acc_ref[j] * corr[j] + pv[j] for j in hs]
        for j in hs:
            m_ref[j] = m_new[j]
            l_ref[j] = l_new[j]
            acc_ref[j] = acc_new[j]

    def below_diagonal(ki, carry):
        block(ki, False)
        return carry

    lax.fori_loop(0, qi, below_diagonal, 0)
    block(qi, True)
    for j in range(hp):
        o_ref[:, j * LANE:(j + 1) * LANE] = (acc_ref[j] / l_ref[j]).astype(o_ref.dtype)


def mla_prompt_attention(qn, qr, kv, krp, *, bsz, seq, tq, hp):
    nq = seq // tq
    kvw = MLA_NOPE + MLA_V
    return pl.pallas_call(
        functools.partial(_attn_body, tq=tq, hp=hp),
        grid=(bsz, MLA_HEADS // hp, nq),
        in_specs=[
            pl.BlockSpec((tq, hp * LANE), lambda b, h, qi: (b * nq + qi, h)),
            pl.BlockSpec((tq, hp * LANE), lambda b, h, qi: (b * nq + qi, h)),
            pl.BlockSpec((seq, hp * kvw), lambda b, h, qi: (b, h)),
            pl.BlockSpec((seq, LANE), lambda b, h, qi: (b, 0)),
        ],
        out_specs=pl.BlockSpec((tq, hp * LANE), lambda b, h, qi: (b * nq + qi, h)),
        out_shape=jax.ShapeDtypeStruct((bsz * seq, MLA_HEADS * MLA_V), BF16),
        scratch_shapes=[pltpu.VMEM((hp, tq, 1), F32), pltpu.VMEM((hp, tq, 1), F32),
                        pltpu.VMEM((hp, tq, MLA_V), F32)],
        compiler_params=_cparams(("parallel", "parallel", "arbitrary")),
        name="mla_prompt_attention",
    )(qn, qr, kv, krp)


def _paged_body(pt_ref, qn_ref, qr_ref, rows_ref, w_ref, *rest, gp, nb, n_valid):
    page_refs = rest[:nb * gp]
    o_ref, ql_ref, qrr_ref, m_ref, l_ref, acc_ref = rest[nb * gp:]
    del pt_ref
    g = pl.program_id(1)
    rows_q = MLA_HEADS * SAMPLE_PAD
    kvw = MLA_NOPE + MLA_V

    @pl.when(g == 0)
    def _():
        for sb in range(nb):
            tok = slice(sb * SAMPLE_PAD, (sb + 1) * SAMPLE_PAD)
            for h in range(MLA_HEADS):
                w_uk = w_ref[:, h * kvw: h * kvw + MLA_NOPE]
                ql = _dot_nt(qn_ref[tok, h * LANE:(h + 1) * LANE], w_uk)
                ql_ref[sb, h * SAMPLE_PAD:(h + 1) * SAMPLE_PAD, :] = ql.astype(BF16)
                qrr_ref[sb, h * SAMPLE_PAD:(h + 1) * SAMPLE_PAD, :] = qr_ref[tok, h * LANE:(h + 1) * LANE]
        m_ref[...] = jnp.full(m_ref.shape, -jnp.inf, F32)
        l_ref[...] = jnp.zeros(l_ref.shape, F32)
        acc_ref[...] = jnp.zeros(acc_ref.shape, F32)

    def update(scores, pvs):
        sbs = range(nb)
        m_old = [m_ref[sb] for sb in sbs]
        m_new = [jnp.maximum(m_old[sb], jnp.max(scores[sb], axis=-1, keepdims=True)) for sb in sbs]
        p = [jnp.exp(scores[sb] - m_new[sb]) for sb in sbs]
        pv = [pvs[sb](p[sb].astype(BF16)) for sb in sbs]
        corr = [jnp.exp(m_old[sb] - m_new[sb]) for sb in sbs]
        return [(m_new[sb], l_ref[sb] * corr[sb] + jnp.sum(p[sb], axis=-1, keepdims=True),
                 acc_ref[sb] * corr[sb] + pv[sb]) for sb in sbs]

    def commit(new):
        for sb, (m_new, l_new, acc_new) in enumerate(new):
            m_ref[sb] = m_new
            l_ref[sb] = l_new
            acc_ref[sb] = acc_new

    lat_t = [jnp.concatenate([pr[:MLA_KV_RANK, :].astype(BF16) for pr in page_refs[sb * gp:(sb + 1) * gp]], axis=1)
             for sb in range(nb)]
    rk_t = [jnp.concatenate([pr[MLA_KV_RANK:, :].astype(BF16) for pr in page_refs[sb * gp:(sb + 1) * gp]], axis=1)
            for sb in range(nb)]
    scores = [_dot(ql_ref[sb], lat_t[sb]) + _dot(qrr_ref[sb][:, :MLA_ROPE], rk_t[sb]) for sb in range(nb)]
    commit(update(scores, [lambda p, lt=lt: _dot_nt(p, lt) for lt in lat_t]))

    @pl.when(g == pl.num_programs(1) - 1)
    def _():
        tq = lax.broadcasted_iota(jnp.int32, (rows_q, SAMPLE_PAD), 0) % SAMPLE_PAD
        tk = lax.broadcasted_iota(jnp.int32, (rows_q, SAMPLE_PAD), 1)
        visible = (tk <= tq) & (tk < n_valid)
        scores, pvs = [], []
        for sb in range(nb):
            tok = slice(sb * SAMPLE_PAD, (sb + 1) * SAMPLE_PAD)
            lat = rows_ref[tok, :MLA_KV_RANK].astype(BF16)
            rk = rows_ref[tok, MLA_KV_RANK:].astype(BF16)
            s = _dot_nt(ql_ref[sb], lat) + _dot_nt(qrr_ref[sb][:, :MLA_ROPE], rk)
            scores.append(jnp.where(visible, s, -jnp.inf))
            pvs.append(lambda p, lat=lat: _dot(p, lat))
        for sb, (_, l_new, acc_new) in enumerate(update(scores, pvs)):
            tok = slice(sb * SAMPLE_PAD, (sb + 1) * SAMPLE_PAD)
            out_lat = (acc_new / l_new).astype(BF16)
            for h in range(MLA_HEADS):
                w_uv = w_ref[:, h * kvw + MLA_NOPE:(h + 1) * kvw]
                o = _dot(out_lat[h * SAMPLE_PAD:(h + 1) * SAMPLE_PAD, :], w_uv)
                o_ref[tok, h * LANE:(h + 1) * LANE] = o.astype(o_ref.dtype)


def mla_paged_attention(qn, qr, rows, w_ukv, cache_t, page_table, *, n_valid, gp, nb):
    bsz, n_pages = page_table.shape
    nw = MLA_HEADS * LANE
    npg = n_pages // gp
    rq = MLA_HEADS * SAMPLE_PAD

    def page_spec(sb, i):
        return pl.BlockSpec((None, ROW_W, PAGE_SIZE),
                            lambda b, g, pt: (pt[(b * nb + sb) * n_pages + g * gp + i], 0, 0))

    grid_spec = pltpu.PrefetchScalarGridSpec(
        num_scalar_prefetch=1,
        grid=(bsz // nb, npg),
        in_specs=[
            pl.BlockSpec((nb * SAMPLE_PAD, nw), lambda b, g, pt: (b, 0)),
            pl.BlockSpec((nb * SAMPLE_PAD, nw), lambda b, g, pt: (b, 0)),
            pl.BlockSpec((nb * SAMPLE_PAD, ROW_W), lambda b, g, pt: (b, 0)),
            pl.BlockSpec(w_ukv.shape, lambda b, g, pt: (0, 0)),
        ] + [page_spec(sb, i) for sb in range(nb) for i in range(gp)],
        out_specs=pl.BlockSpec((nb * SAMPLE_PAD, nw), lambda b, g, pt: (b, 0)),
        scratch_shapes=[
            pltpu.VMEM((nb, rq, MLA_KV_RANK), BF16),
            pltpu.VMEM((nb, rq, LANE), BF16),
            pltpu.VMEM((nb, rq, 1), F32),
            pltpu.VMEM((nb, rq, 1), F32),
            pltpu.VMEM((nb, rq, MLA_KV_RANK), F32),
        ],
    )
    return pl.pallas_call(
        functools.partial(_paged_body, gp=gp, nb=nb, n_valid=n_valid),
        grid_spec=grid_spec,
        out_shape=jax.ShapeDtypeStruct((bsz * SAMPLE_PAD, nw), BF16),
        compiler_params=_cparams(("parallel", "arbitrary")),
        name="mla_paged_attention",
    )(page_table.reshape(-1), qn, qr, rows, w_ukv, *([cache_t] * (nb * gp)))


def _mlstm_body(q_ref, k_ref, v_ref, og_ref, gc_ref, gr_ref, bc_ref, br_ref, ng_ref, c0_ref, n0_ref, m0_ref,
                h_ref, c_ref, n_ref, m_ref, *, lc, n_valid, nb, single_chunk):
    if single_chunk:
        cs_ref, ns_ref, ms_ref = c0_ref, n0_ref, m0_ref
        m_ref[...] = m0_ref[...]
    else:
        cs_ref, ns_ref, ms_ref = c_ref, n_ref, m_ref

        @pl.when(pl.program_id(1) == 0)
        def _():
            c_ref[...] = c0_ref[...]
            n_ref[...] = n0_ref[...]
            m_ref[...] = m0_ref[...]

    nh = ML_HEADS
    row = lax.broadcasted_iota(jnp.int32, (lc, lc), 0)
    col = lax.broadcasted_iota(jnp.int32, (lc, lc), 1)
    causal = col <= row
    tril = causal.astype(F32)
    triu = (row <= col).astype(F32)
    seqs = []
    for sb in range(nb):
        tok = slice(sb * lc, (sb + 1) * lc)
        gc = gc_ref[tok, :] + bc_ref[...]
        gr = gr_ref[sb] + br_ref[:, 0:1]
        li_c, lf_c = gc[:, 0:nh], _log_sigmoid(gc[:, nh:2 * nh])
        li_r, lf_r = gr[0:nh, :], _log_sigmoid(gr[nh:2 * nh, :])
        if n_valid < lc:
            vc = lax.broadcasted_iota(jnp.int32, (lc, nh), 0) < n_valid
            vr = lax.broadcasted_iota(jnp.int32, (nh, lc), 1) < n_valid
            li_c, lf_c = jnp.where(vc, li_c, NEG_BIG), jnp.where(vc, lf_c, 0.0)
            li_r, lf_r = jnp.where(vr, li_r, NEG_BIG), jnp.where(vr, lf_r, 0.0)
        seqs.append((li_c, li_r, lf_c, lf_r))
    b_cs = [jnp.dot(tril, s[2], precision=HIGHEST, preferred_element_type=F32) for s in seqs]
    b_rs = [jnp.dot(s[3], triu, precision=HIGHEST, preferred_element_type=F32) for s in seqs]

    items = [(sb, h) for sb in range(nb) for h in range(nh)]
    st = []
    for sb, h in items:
        tok = slice(sb * lc, (sb + 1) * lc)
        qh = q_ref[tok, h * ML_DQK:(h + 1) * ML_DQK] * (ML_DQK ** -0.5)
        kh = k_ref[tok, h * ML_DQK:(h + 1) * ML_DQK]
        st.append(dict(
            bc=b_cs[sb][:, h:h + 1], br=b_rs[sb][h:h + 1, :],
            ic=seqs[sb][0][:, h:h + 1], ir=seqs[sb][1][h:h + 1, :],
            m_prev=ms_ref[sb, h:h + 1, 0:1], qh=qh, qb=qh.astype(BF16), kh=kh, kb=kh.astype(BF16),
            vh=v_ref[tok, h * ML_DV:(h + 1) * ML_DV], og=og_ref[tok, h * ML_DV:(h + 1) * ML_DV],
            c=cs_ref[sb, h], n=ns_ref[sb, h:h + 1, :], ng=ng_ref[:, h * ML_DV:(h + 1) * ML_DV]))
    for d in st:
        d["qk"] = _dot_nt(d["qb"], d["kb"])
        d["qc"] = _dot_nt(d["qb"], d["c"].astype(BF16))
    for d in st:
        dmat = jnp.where(causal, d["bc"] - d["br"] + d["ir"], -jnp.inf)
        inter = d["bc"] + d["m_prev"]
        d["m_t"] = jnp.maximum(inter, jnp.max(dmat, axis=-1, keepdims=True))
        d["w_inter"] = jnp.exp(inter - d["m_t"])
        d["s"] = d["qk"] * jnp.exp(dmat - d["m_t"])
        b_last = d["bc"][lc - 1:lc, :]
        g_c = b_last - d["bc"] + d["ic"]
        g_r = b_last - d["br"] + d["ir"]
        d["m_new"] = jnp.maximum(b_last + d["m_prev"], jnp.max(g_r, axis=-1, keepdims=True))
        d["decay"] = jnp.exp(b_last + d["m_prev"] - d["m_new"])
        d["wg"] = jnp.exp(g_c - d["m_new"])
    for d in st:
        d["sv"] = _dot(d["s"].astype(BF16), d["vh"].astype(BF16))
        d["vk"] = _dot_tn((d["vh"] * d["wg"]).astype(BF16), d["kb"])
    for d in st:
        num = d["sv"] + d["w_inter"] * d["qc"]
        den = (jnp.sum(d["s"], axis=-1, keepdims=True)
               + d["w_inter"] * jnp.sum(d["qh"] * d["n"], axis=-1, keepdims=True))
        hout = num / jnp.maximum(jnp.abs(den), jnp.exp(-d["m_t"]))
        d["h_new"] = (_sigmoid(d["og"]) * _rms(hout, d["ng"])).astype(h_ref.dtype)
        d["c_new"] = d["decay"] * d["c"] + d["vk"]
        d["n_new"] = d["decay"] * d["n"] + jnp.sum(d["kh"] * d["wg"], axis=0, keepdims=True)

    for (sb, h), d in zip(items, st):
        h_ref[sb * lc:(sb + 1) * lc, h * ML_DV:(h + 1) * ML_DV] = d["h_new"]
        c_ref[sb, h] = d["c_new"]
        n_ref[sb, h:h + 1, :] = d["n_new"]
        m_ref[sb, h:h + 1, :] = jnp.broadcast_to(d["m_new"], (1, LANE))


def mlstm(p, gates_rows, b_ig, b_fg, norm_g, c0, n0, m0, *, bsz, lc, nchunk, n_valid, nb):
    assert nb == 1 or nchunk == 1
    nh = ML_HEADS
    bias_c = jnp.concatenate([b_ig, b_fg, jnp.zeros((LANE - 2 * nh,), F32)]).reshape(1, LANE)
    bias_r = jnp.broadcast_to(jnp.concatenate([b_ig, b_fg]).reshape(2 * nh, 1), (2 * nh, LANE))
    tl = nb * lc

    def rows(cb):
        return lambda b, c: (b * nchunk + c, cb)

    const2 = lambda b, c: (0, 0)
    return pl.pallas_call(
        functools.partial(_mlstm_body, lc=lc, n_valid=n_valid, nb=nb, single_chunk=nchunk == 1),
        grid=(bsz // nb, nchunk),
        in_specs=[
            pl.BlockSpec((tl, nh * ML_DQK), rows(P_Q // (nh * ML_DQK))),
            pl.BlockSpec((tl, nh * ML_DQK), rows(P_K // (nh * ML_DQK))),
            pl.BlockSpec((tl, nh * ML_DV), rows(P_V // (nh * ML_DV))),
            pl.BlockSpec((tl, nh * ML_DV), rows(P_OG // (nh * ML_DV))),
            pl.BlockSpec((tl, LANE), rows(P_GATE // LANE)),
            pl.BlockSpec((nb, 2 * nh, lc), lambda b, c: (b * nchunk + c, 0, 0)),
            pl.BlockSpec((1, LANE), const2),
            pl.BlockSpec((2 * nh, LANE), const2),
            pl.BlockSpec((1, nh * ML_DV), const2),
            pl.BlockSpec((nb, nh, ML_DV, ML_DQK), lambda b, c: (b, 0, 0, 0)),
            pl.BlockSpec((nb, nh, ML_DQK), lambda b, c: (b, 0, 0)),
            pl.BlockSpec((nb, 2 * nh, LANE), lambda b, c: (b, 0, 0)),
        ],
        out_specs=[
            pl.BlockSpec((tl, nh * ML_DV), lambda b, c: (b * nchunk + c, 0)),
            pl.BlockSpec((nb, nh, ML_DV, ML_DQK), lambda b, c: (b, 0, 0, 0)),
            pl.BlockSpec((nb, nh, ML_DQK), lambda b, c: (b, 0, 0)),
            pl.BlockSpec((nb, 2 * nh, LANE), lambda b, c: (b, 0, 0)),
        ],
        out_shape=[
            jax.ShapeDtypeStruct((bsz * nchunk * lc, nh * ML_DV), BF16),
            jax.ShapeDtypeStruct((bsz, nh, ML_DV, ML_DQK), F32),
            jax.ShapeDtypeStruct((bsz, nh, ML_DQK), F32),
            jax.ShapeDtypeStruct((bsz, 2 * nh, LANE), F32),
        ],
        compiler_params=_cparams(("parallel", "arbitrary")),
        name="mlstm",
    )(p, p, p, p, p, gates_rows, bias_c, bias_r, norm_g.reshape(1, -1), c0, n0, m0)


def _gates_rows(p, lc):
    nrows = p.shape[0]
    g = p[:, P_GATE:P_GATE + 2 * ML_HEADS]
    return g.reshape(nrows // lc, lc, 2 * ML_HEADS).transpose(0, 2, 1)


def _m_pack(m):
    b = m.shape[0]
    mp = jnp.pad(m.astype(F32), ((0, 0), (0, 2 * ML_HEADS - m.shape[1])))
    return jnp.broadcast_to(mp[:, :, None], (b, 2 * ML_HEADS, LANE))


def _ssd_body(z_ref, x_ref, b_ref, c_ref, dtc_ref, dtr_ref, cwx_ref, cwb_ref, cwc_ref, cbx_ref, cbb_ref, cbc_ref,
              csx_ref, csb_ref, csc_ref, pc_ref, pr_ref, dsk_ref, ng_ref, h0_ref,
              y_ref, h_ref, bufx, bufb, bufc, *, lc, n_valid, gs, single_chunk):
    tail = SAMPLE_PAD
    cxs = [slice(gi * SSM_GW, (gi + 1) * SSM_GW) for gi in range(gs)]
    cns = [slice(gi * SSM_STATE, (gi + 1) * SSM_STATE) for gi in range(gs)]

    def init_history():
        for gi in range(gs):
            bufx[gi, 0:tail, :] = csx_ref[:, cxs[gi]]
            bufb[gi, 0:tail, :] = csb_ref[:, cns[gi]]
            bufc[gi, 0:tail, :] = csc_ref[:, cns[gi]]

    if single_chunk:
        init_history()
        hs_ref = h0_ref
    else:
        hs_ref = h_ref

        @pl.when(pl.program_id(2) == 0)
        def _():
            h_ref[...] = h0_ref[...]
            init_history()

    def conv(raw, buf, cw_ref, cb_ref, cols):
        buf[tail:tail + lc, :] = raw
        acc = cb_ref[:, cols] + cw_ref[SSM_CONV - 1:SSM_CONV, cols] * raw
        for j in range(SSM_CONV - 1):
            off = tail - (SSM_CONV - 1) + j
            acc = acc + cw_ref[j:j + 1, cols] * buf[off:off + lc, :]
        if not single_chunk:
            buf[0:tail, :] = raw[lc - tail:lc, :]
        return _silu(acc)

    convs = []
    for gi in range(gs):
        convs.append((conv(x_ref[:, cxs[gi]], bufx.at[gi], cwx_ref, cbx_ref, cxs[gi]),
                      conv(b_ref[:, cns[gi]], bufb.at[gi], cwb_ref, cbb_ref, cns[gi]),
                      conv(c_ref[:, cns[gi]], bufc.at[gi], cwc_ref, cbc_ref, cns[gi])))

    outs = _ssd_groups(
        convs,
        [dict(z=z_ref[:, cxs[gi]], dtc=dtc_ref[:, gi * LANE:gi * LANE + SSM_HG],
              dtr=dtr_ref[gi * SSM_HG:(gi + 1) * SSM_HG, :], pc=pc_ref[gi], pr=pr_ref[gi],
              dsk=dsk_ref[:, cxs[gi]], ng=ng_ref[:, cxs[gi]], h=hs_ref[gi]) for gi in range(gs)],
        lc=lc, n_valid=n_valid)

    for gi, (y, h_new) in enumerate(outs):
        y_ref[:, cxs[gi]] = y.astype(y_ref.dtype)
        h_ref[gi] = h_new


def _ssd_groups(convs, ins, *, lc, n_valid):
    hg = SSM_HG
    gs = range(len(ins))
    row = lax.broadcasted_iota(jnp.int32, (lc, lc), 0)
    col = lax.broadcasted_iota(jnp.int32, (lc, lc), 1)
    causal = col <= row
    tril = causal.astype(F32)
    triu = (row <= col).astype(F32)
    expand = (lax.broadcasted_iota(jnp.int32, (hg, SSM_GW), 1) // SSM_HEADDIM
              == lax.broadcasted_iota(jnp.int32, (hg, SSM_GW), 0)).astype(F32)
    lane_head = lax.broadcasted_iota(jnp.int32, (lc, LANE), 1) // SSM_HEADDIM

    def hi(a, b):
        return jnp.dot(a, b, precision=HIGHEST, preferred_element_type=F32)

    dt_c, dt_r = [], []
    for g in gs:
        c = _softplus(ins[g]["dtc"] + ins[g]["pc"][0:1, 0:hg])
        r = _softplus(ins[g]["dtr"] + ins[g]["pr"][:, 0:1])
        if n_valid < lc:
            c = jnp.where(lax.broadcasted_iota(jnp.int32, (lc, hg), 0) < n_valid, c, 0.0)
            r = jnp.where(lax.broadcasted_iota(jnp.int32, (hg, lc), 1) < n_valid, r, 0.0)
        dt_c.append(c)
        dt_r.append(r)
    cum_c = [hi(tril, dt_c[g] * (-jnp.exp(ins[g]["pc"][1:2, 0:hg]))) for g in gs]
    cum_r = [hi(dt_r[g] * (-jnp.exp(ins[g]["pr"][:, 1:2])), triu) for g in gs]
    cb16 = [convs[g][2].astype(BF16) for g in gs]
    bb16 = [convs[g][1].astype(BF16) for g in gs]
    cbm = [_dot_nt(cb16[g], bb16[g]) for g in gs]
    yh = [_dot_nt(cb16[g], ins[g]["h"].astype(BF16)) for g in gs]
    e_in = [hi(jnp.exp(cum_c[g]), expand) for g in gs]
    e_st = [hi(jnp.exp(cum_c[g][lc - 1:lc, :] - cum_c[g]) * dt_c[g], expand) for g in gs]
    upd = [_dot_tn((convs[g][0] * e_st[g]).astype(BF16), bb16[g]) for g in gs]

    pieces = [[] for _ in gs]
    for pair in range(hg // 2):
        parts = [[] for _ in gs]
        for sub in range(2):
            j = 2 * pair + sub
            for g in gs:
                seg = jnp.where(causal, cum_c[g][:, j:j + 1] - cum_r[g][j:j + 1, :], -jnp.inf)
                w = cbm[g] * jnp.exp(seg) * dt_r[g][j:j + 1, :]
                xm = jnp.where(lane_head == sub, convs[g][0][:, pair * LANE:(pair + 1) * LANE], 0.0)
                parts[g].append(_dot(w.astype(BF16), xm.astype(BF16)))
        for g in gs:
            pieces[g].append(parts[g][0] + parts[g][1])

    outs = []
    for g in gs:
        xa, h_old = convs[g][0], ins[g]["h"]
        y = yh[g] * e_in[g] + jnp.concatenate(pieces[g], axis=1)
        dec_r = jnp.exp(cum_r[g][:, lc - 1:lc])
        h_new = jnp.concatenate(
            [dec_r[j:j + 1, :] * h_old[j * SSM_HEADDIM:(j + 1) * SSM_HEADDIM, :]
             + upd[g][j * SSM_HEADDIM:(j + 1) * SSM_HEADDIM, :] for j in range(hg)], axis=0)
        y = (y + ins[g]["dsk"] * xa) * _silu(ins[g]["z"])
        outs.append((_rms(y, ins[g]["ng"]), h_new))
    return outs


def ssd(zp, dt_rows, conv_w, conv_b, conv_state8, dt_bias, a_log, d_skip, norm_g, h0, *, bsz, lc, nchunk,
        n_valid, gs):
    g_, hg = SSM_GROUPS // gs, SSM_HG * gs
    gw, ns, lw = SSM_GW * gs, SSM_STATE * gs, LANE * gs

    def rows(off, width):
        assert off % width == 0
        return lambda b, g, c: (b * nchunk + c, off // width + g)

    ng_all, nh1 = SSM_GROUPS, SSM_HG
    pcol = jnp.zeros((ng_all, 8, LANE), F32)
    pcol = pcol.at[:, 0, :nh1].set(dt_bias.reshape(ng_all, nh1)).at[:, 1, :nh1].set(a_log.reshape(ng_all, nh1))
    prow = jnp.zeros((ng_all, nh1, LANE), F32)
    prow = prow.at[:, :, 0].set(dt_bias.reshape(ng_all, nh1)).at[:, :, 1].set(a_log.reshape(ng_all, nh1))
    dsk = jnp.repeat(d_skip, SSM_HEADDIM).reshape(1, SSM_D_INNER)
    cb2 = conv_b.reshape(1, -1)
    bo, co = SSM_D_INNER // ns, SSM_D_INNER // ns + g_
    assert SSM_D_INNER % ns == 0

    in_specs = [
        pl.BlockSpec((lc, gw), rows(0, gw)),
        pl.BlockSpec((lc, gw), rows(Z_X, gw)),
        pl.BlockSpec((lc, ns), rows(Z_B, ns)),
        pl.BlockSpec((lc, ns), rows(Z_C, ns)),
        pl.BlockSpec((lc, lw), rows(Z_DT, lw)),
        pl.BlockSpec((None, hg, lc), lambda b, g, c: (b * nchunk + c, g, 0)),
        pl.BlockSpec((SSM_CONV, gw), lambda b, g, c: (0, g)),
        pl.BlockSpec((SSM_CONV, ns), lambda b, g, c: (0, bo + g)),
        pl.BlockSpec((SSM_CONV, ns), lambda b, g, c: (0, co + g)),
        pl.BlockSpec((1, gw), lambda b, g, c: (0, g)),
        pl.BlockSpec((1, ns), lambda b, g, c: (0, bo + g)),
        pl.BlockSpec((1, ns), lambda b, g, c: (0, co + g)),
        pl.BlockSpec((None, 8, gw), lambda b, g, c: (b, 0, g)),
        pl.BlockSpec((None, 8, ns), lambda b, g, c: (b, 0, bo + g)),
        pl.BlockSpec((None, 8, ns), lambda b, g, c: (b, 0, co + g)),
        pl.BlockSpec((gs, 8, LANE), lambda b, g, c: (g, 0, 0)),
        pl.BlockSpec((gs, nh1, LANE), lambda b, g, c: (g, 0, 0)),
        pl.BlockSpec((1, gw), lambda b, g, c: (0, g)),
        pl.BlockSpec((1, gw), lambda b, g, c: (0, g)),
        pl.BlockSpec((None, gs, SSM_GW, SSM_STATE), lambda b, g, c: (b, g, 0, 0)),
    ]
    return pl.pallas_call(
        functools.partial(_ssd_body, lc=lc, n_valid=n_valid, gs=gs, single_chunk=nchunk == 1),
        grid=(bsz, g_, nchunk),
        in_specs=in_specs,
        out_specs=[
            pl.BlockSpec((lc, gw), lambda b, g, c: (b * nchunk + c, g)),
            pl.BlockSpec((None, gs, SSM_GW, SSM_STATE), lambda b, g, c: (b, g, 0, 0)),
        ],
        out_shape=[
            jax.ShapeDtypeStruct((bsz * nchunk * lc, SSM_D_INNER), BF16),
            jax.ShapeDtypeStruct((bsz, SSM_GROUPS, SSM_GW, SSM_STATE), F32),
        ],
        scratch_shapes=[
            pltpu.VMEM((gs, lc + SAMPLE_PAD, SSM_GW), F32),
            pltpu.VMEM((gs, lc + SAMPLE_PAD, SSM_STATE), F32),
            pltpu.VMEM((gs, lc + SAMPLE_PAD, SSM_STATE), F32),
        ],
        compiler_params=_cparams(("parallel", "parallel", "arbitrary")),
        name="ssd",
    )(zp, zp, zp, zp, zp, dt_rows, conv_w, conv_w, conv_w, cb2, cb2, cb2, conv_state8, conv_state8, conv_state8,
      pcol, prow, dsk, norm_g.reshape(1, -1), h0)


def _dt_rows(zp, lc):
    nrows = zp.shape[0]
    dt = zp[:, Z_DT:].reshape(nrows, SSM_GROUPS, LANE)[:, :, :SSM_HG].reshape(nrows // lc, lc, SSM_HEADS)
    return dt.transpose(0, 2, 1)


def _conv_state8(state):
    return jnp.pad(state, ((0, 0), (SAMPLE_PAD - (SSM_CONV - 1), 0), (0, 0)))


def _trunk(x_prompt, x_sample, cache_mla, state_mlstm_C, state_mlstm_n, state_mlstm_m, state_ssm, state_conv,
           page_table, norm_g, ffn_w_gate, ffn_w_up, ffn_w_down, w_in_even, b_ig, b_fg, mlstm_norm_g, mla_q_norm_g,
           mla_kv_norm_g, w_uq, w_ukv, w_out_even, w_in_ssm, conv_w, conv_b, dt_bias, A_log, D_skip, ssm_norm_g,
           w_out_ssm, final_norm_g, *, tm, tf, tn, tmp, lc_ml, nb_ml, lc_ssd, gs_p, gs_s, tq, hp, gp, nb_pg):
    bp, sp, d = x_prompt.shape
    bs, ts, _ = x_sample.shape
    tp = bp * sp
    tsp = bs * SAMPLE_PAD
    xs = [x_prompt.reshape(tp, d), jnp.pad(x_sample, ((0, 0), (0, SAMPLE_PAD - ts), (0, 0))).reshape(tsp, d)]
    tms = [min(tm, tp), min(tm, tsp)]
    tmps = [min(tmp, tp), min(tmp, tsp)]
    tables = [_rope_tables(jnp.tile(jnp.arange(sp), bp)),
              _rope_tables(jnp.tile(PAST_LEN + jnp.arange(SAMPLE_PAD), bs))]

    def ff(xs, g, layer, which, final_g=None):
        return [ffn(x, g, ffn_w_gate, ffn_w_up, ffn_w_down, layer, which, tm=t, tf=tf, final_g=final_g)
                for x, t in zip(xs, tms)]

    xs = ff(xs, norm_g[0, 0], 0, 0)
    w_in = _prep_w_in_even(w_in_even)
    p_p, p_s = [norm_matmul(x, norm_g[0, 1], w_in, tm=t, tn=tn) for x, t in zip(xs, tms)]

    zeros = functools.partial(jnp.zeros, dtype=F32)
    hm_p, c_p, n_p, m_p = mlstm(
        p_p, _gates_rows(p_p, lc_ml), b_ig, b_fg, mlstm_norm_g,
        zeros((bp, ML_HEADS, ML_DV, ML_DQK)), zeros((bp, ML_HEADS, ML_DQK)), zeros((bp, 2 * ML_HEADS, LANE)),
        bsz=bp, lc=lc_ml, nchunk=sp // lc_ml, n_valid=lc_ml, nb=1)
    hm_s, c_s, n_s, m_s = mlstm(
        p_s, _gates_rows(p_s, SAMPLE_PAD), b_ig, b_fg, mlstm_norm_g,
        state_mlstm_C, state_mlstm_n, _m_pack(state_mlstm_m),
        bsz=bs, lc=SAMPLE_PAD, nchunk=1, n_valid=ts, nb=nb_ml)

    w_ukv16 = w_ukv.astype(BF16)
    w_q = _prep_w_uq(w_uq)
    (qn_p, qr_p), (qn_s, qr_s) = [q_prep(p, mla_q_norm_g, w_q, c, s, tm=t)
                                  for p, (c, s), t in zip((p_p, p_s), tables, tmps)]
    rows_p, kv, krp = kv_prep(p_p, mla_kv_norm_g, *tables[0], tm=tmps[0], w_ukv=w_ukv16)
    (rows_s,) = kv_prep(p_s, mla_kv_norm_g, *tables[1], tm=tmps[1])
    ha_p = mla_prompt_attention(qn_p, qr_p, kv, krp, bsz=bp, seq=sp, tq=tq, hp=hp)
    cache_t = jnp.swapaxes(cache_mla, 1, 2)
    ha_s = mla_paged_attention(qn_s, qr_s, rows_s, w_ukv16, cache_t, page_table, n_valid=ts, gp=gp, nb=nb_pg)

    w_oe = w_out_even.astype(BF16)
    xs = [matmul_residual([hm, ha], w_oe, x, tm=t, tn=tn)
          for hm, ha, x, t in zip((hm_p, hm_s), (ha_p, ha_s), xs, tms)]
    xs = ff(xs, norm_g[0, 2], 0, 1)

    xs = ff(xs, norm_g[1, 0], 1, 0)
    w_is = _prep_w_in_ssm(w_in_ssm)
    z_p, z_s = [norm_matmul(x, norm_g[1, 1], w_is, tm=t, tn=tn) for x, t in zip(xs, tms)]
    y_p, ssm_p = ssd(
        z_p, _dt_rows(z_p, lc_ssd), conv_w, conv_b, zeros((bp, SAMPLE_PAD, SSM_CONV_DIM)), dt_bias, A_log,
        D_skip, ssm_norm_g, zeros((bp, SSM_GROUPS, SSM_GW, SSM_STATE)),
        bsz=bp, lc=lc_ssd, nchunk=sp // lc_ssd, n_valid=lc_ssd, gs=gs_p)
    y_s, ssm_s = ssd(
        z_s, _dt_rows(z_s, SAMPLE_PAD), conv_w, conv_b, _conv_state8(state_conv), dt_bias, A_log,
        D_skip, ssm_norm_g, state_ssm.reshape(bs, SSM_GROUPS, SSM_GW, SSM_STATE),
        bsz=bs, lc=SAMPLE_PAD, nchunk=1, n_valid=ts, gs=gs_s)
    w_os = w_out_ssm.astype(BF16)
    xs = [matmul_residual([y], w_os, x, tm=t, tn=tn) for y, x, t in zip((y_p, y_s), xs, tms)]
    out_p, out_s = ff(xs, norm_g[1, 2], 1, 1, final_g=final_norm_g)

    def sample_rows(a):
        return a.reshape(bs, SAMPLE_PAD, a.shape[1])[:, :ts]

    keep = SSM_CONV - 1
    assert sp >= keep and ts >= keep
    conv_p = jnp.stack([lax.slice(z_p, (b * sp + sp - keep, Z_X), (b * sp + sp, Z_DT)) for b in range(bp)])
    conv_s = z_s[:, Z_X:Z_DT].reshape(bs, SAMPLE_PAD, SSM_CONV_DIM)[:, ts - keep:ts]
    hshape = (SSM_HEADS, SSM_HEADDIM, SSM_STATE)
    return (out_p.reshape(bp, sp, d), sample_rows(out_s), rows_p.reshape(bp, sp, ROW_W), sample_rows(rows_s),
            c_p, c_s, n_p, n_s, m_p[:, :ML_HEADS, 0], m_s[:, :ML_HEADS, 0],
            ssm_p.reshape((bp,) + hshape), ssm_s.reshape((bs,) + hshape), conv_p, conv_s)


def kernel(x_prompt, x_sample, cache_mla, state_mlstm_C, state_mlstm_n, state_mlstm_m, state_ssm, state_conv, page_table, norm_g, ffn_w_gate, ffn_w_up, ffn_w_down, w_in_even, b_ig, b_fg, mlstm_norm_g, mla_q_norm_g, mla_kv_norm_g, w_uq, w_ukv, w_out_even, w_in_ssm, conv_w, conv_b, dt_bias, A_log, D_skip, ssm_norm_g, w_out_ssm, final_norm_g):
    return _trunk(x_prompt, x_sample, cache_mla, state_mlstm_C, state_mlstm_n, state_mlstm_m, state_ssm, state_conv,
                  page_table, norm_g, ffn_w_gate, ffn_w_up, ffn_w_down, w_in_even, b_ig, b_fg, mlstm_norm_g,
                  mla_q_norm_g, mla_kv_norm_g, w_uq, w_ukv, w_out_even, w_in_ssm, conv_w, conv_b, dt_bias, A_log,
                  D_skip, ssm_norm_g, w_out_ssm, final_norm_g,
                  tm=1024, tf=256, tn=512, tmp=512, lc_ml=256, nb_ml=4, lc_ssd=128, gs_p=2, gs_s=8,
                  tq=512, hp=2, gp=16, nb_pg=2)
```

```python
import functools
import math

import jax
import jax.numpy as jnp
from jax import lax
from jax.experimental import pallas as pl
from jax.experimental.pallas import tpu as pltpu

F32 = jnp.float32
BF16 = jnp.bfloat16
HIGHEST = lax.Precision.HIGHEST

D_MODEL = 2048
D_FF = 5632
EPS = 1e-6
PAST_LEN = 8192
PAGE_SIZE = 128

ML_HEADS = 4
ML_DQK = 128
ML_DV = 256

MLA_HEADS = 8
MLA_NOPE = 128
MLA_ROPE = 64
MLA_V = 128
MLA_Q_RANK = 512
MLA_KV_RANK = 512
MLA_SCALE = (MLA_NOPE + MLA_ROPE) ** -0.5
ROPE_BASE = 10000.0
ROW_W = MLA_KV_RANK + MLA_ROPE

SSM_D_INNER = 4096
SSM_HEADDIM = 64
SSM_HEADS = 64
SSM_GROUPS = 8
SSM_HG = SSM_HEADS // SSM_GROUPS
SSM_GW = SSM_D_INNER // SSM_GROUPS
SSM_STATE = 128
SSM_CONV = 4
SSM_CONV_DIM = SSM_D_INNER + 2 * SSM_GROUPS * SSM_STATE

SAMPLE_PAD = 8
LANE = 128
VMEM_LIMIT = 56 * 1024 * 1024

P_Q, P_K, P_V, P_OG, P_CQ, P_CKV = 0, 512, 1024, 2048, 3072, 3584
P_KR, P_KRSW, P_GATE, P_WIDTH = 4096, 4224, 4352, 4608
Z_X, Z_B, Z_C, Z_DT, Z_WIDTH = 4096, 8192, 9216, 10240, 11264

NEG_BIG = -1e30


def _cparams(sem):
    return pltpu.CompilerParams(dimension_semantics=sem, vmem_limit_bytes=VMEM_LIMIT)


def _rms(x, g):
    return x * lax.rsqrt(jnp.mean(x * x, axis=-1, keepdims=True) + EPS) * g


def _sigmoid(x):
    return 1.0 / (1.0 + jnp.exp(-x))


def _silu(x):
    return x * _sigmoid(x)


def _log_sigmoid(x):
    return jnp.minimum(x, 0.0) - jnp.log1p(jnp.exp(-jnp.abs(x)))


def _softplus(x):
    return jnp.maximum(x, 0.0) + jnp.log1p(jnp.exp(-jnp.abs(x)))


def _dot(a, b):
    return jnp.dot(a, b, preferred_element_type=F32)


def _dot_nt(a, b):
    return lax.dot_general(a, b, (((1,), (1,)), ((), ())), preferred_element_type=F32)


def _dot_tn(a, b):
    return lax.dot_general(a, b, (((0,), (0,)), ((), ())), preferred_element_type=F32)


def _ffn_body(x_ref, g_ref, wg_ref, wu_ref, wd_ref, *rest, final_norm):
    if final_norm:
        gf_ref, o_ref, xn_ref = rest
    else:
        o_ref, xn_ref = rest
    j = pl.program_id(1)

    @pl.when(j == 0)
    def _():
        x = x_ref[...]
        xn_ref[...] = _rms(x, g_ref[...]).astype(BF16)
        o_ref[...] = x

    xn = xn_ref[...]
    hg = _dot(xn, wg_ref[...].astype(BF16))
    hu = _dot(xn, wu_ref[...].astype(BF16))
    h = _silu(hg) * (0.5 * hu)
    o_ref[...] += _dot(h.astype(BF16), wd_ref[...].astype(BF16))

    if final_norm:
        @pl.when(j == pl.num_programs(1) - 1)
        def _():
            o_ref[...] = _rms(o_ref[...], gf_ref[...])


def ffn(x, g, w_gate, w_up, w_down, layer, which, *, tm, tf, final_g=None):
    t, d = x.shape
    dff = w_gate.shape[-1]
    final_norm = final_g is not None
    in_specs = [
        pl.BlockSpec((tm, d), lambda i, j: (i, 0), pipeline_mode=pl.Buffered(1)),
        pl.BlockSpec((1, d), lambda i, j: (0, 0)),
        pl.BlockSpec((None, None, d, tf), lambda i, j: (layer, which, 0, j)),
        pl.BlockSpec((None, None, d, tf), lambda i, j: (layer, which, 0, j)),
        pl.BlockSpec((None, None, tf, d), lambda i, j: (layer, which, j, 0)),
    ]
    args = [x, g.reshape(1, d), w_gate, w_up, w_down]
    if final_norm:
        in_specs.append(pl.BlockSpec((1, d), lambda i, j: (0, 0)))
        args.append(final_g.reshape(1, d))
    return pl.pallas_call(
        functools.partial(_ffn_body, final_norm=final_norm),
        grid=(t // tm, dff // tf),
        in_specs=in_specs,
        out_specs=pl.BlockSpec((tm, d), lambda i, j: (i, 0)),
        out_shape=jax.ShapeDtypeStruct((t, d), F32),
        scratch_shapes=[pltpu.VMEM((tm, d), BF16)],
        compiler_params=_cparams(("parallel", "arbitrary")),
        name="ffn",
    )(*args)


def _norm_mm_body(x_ref, g_ref, w_ref, o_ref, xn_ref):
    @pl.when(pl.program_id(1) == 0)
    def _():
        xn_ref[...] = _rms(x_ref[...], g_ref[...]).astype(BF16)

    o_ref[...] = _dot(xn_ref[...], w_ref[...].astype(BF16))


def norm_matmul(x, g, w, *, tm, tn):
    t, d = x.shape
    n = w.shape[1]
    return pl.pallas_call(
        _norm_mm_body,
        grid=(t // tm, n // tn),
        in_specs=[
            pl.BlockSpec((tm, d), lambda i, j: (i, 0)),
            pl.BlockSpec((1, d), lambda i, j: (0, 0)),
            pl.BlockSpec((d, tn), lambda i, j: (0, j)),
        ],
        out_specs=pl.BlockSpec((tm, tn), lambda i, j: (i, j)),
        out_shape=jax.ShapeDtypeStruct((t, n), F32),
        scratch_shapes=[pltpu.VMEM((tm, d), BF16)],
        compiler_params=_cparams(("parallel", "arbitrary")),
        name="norm_matmul",
    )(x, g.reshape(1, d), w)


def _mm_res_body(*refs, n_a):
    a_refs, w_refs = refs[:n_a], refs[n_a:2 * n_a]
    res_ref, o_ref = refs[2 * n_a], refs[2 * n_a + 1]
    acc = res_ref[...]
    for a_ref, w_ref in zip(a_refs, w_refs):
        acc = acc + _dot(a_ref[...].astype(BF16), w_ref[...].astype(BF16))
    o_ref[...] = acc


def matmul_residual(a_list, w, res, *, tm, tn):
    n_a = len(a_list)
    t, ka = a_list[0].shape
    n = w.shape[1]
    in_specs = [pl.BlockSpec((tm, ka), lambda i, j: (i, 0)) for _ in a_list]
    in_specs += [pl.BlockSpec((ka, tn), lambda i, j, s=s: (s, j)) for s in range(n_a)]
    in_specs.append(pl.BlockSpec((tm, tn), lambda i, j: (i, j)))
    return pl.pallas_call(
        functools.partial(_mm_res_body, n_a=n_a),
        grid=(t // tm, n // tn),
        in_specs=in_specs,
        out_specs=pl.BlockSpec((tm, tn), lambda i, j: (i, j)),
        out_shape=jax.ShapeDtypeStruct((t, n), F32),
        compiler_params=_cparams(("parallel", "parallel")),
        name="matmul_residual",
    )(*a_list, *([w] * n_a), res)


def _swap_halves(w):
    half = w.shape[-1] // 2
    return jnp.concatenate([w[..., half:], w[..., :half]], axis=-1)


def _prep_w_in_even(w):
    d = w.shape[0]
    sizes = (512, 512, 1024, 4, 4, 1024, MLA_Q_RANK, MLA_KV_RANK, MLA_ROPE)
    idx = [0]
    for s in sizes:
        idx.append(idx[-1] + s)
    q, k, v, ig, fg, og, cq, ckv, kr = [w[:, idx[i]:idx[i + 1]] for i in range(len(sizes))]
    z64 = jnp.zeros((d, LANE - MLA_ROPE), w.dtype)
    gates = jnp.concatenate([ig, fg, jnp.zeros((d, LANE - 2 * ML_HEADS), w.dtype)], axis=1)
    pad = jnp.zeros((d, P_WIDTH - P_GATE - LANE), w.dtype)
    return jnp.concatenate([q, k, v, og, cq, ckv, kr, z64, _swap_halves(kr), z64, gates, pad], axis=1).astype(BF16)


def _prep_w_uq(w):
    r = w.shape[0]
    w3 = w.reshape(r, MLA_HEADS, MLA_NOPE + MLA_ROPE)
    nope = w3[..., :MLA_NOPE].reshape(r, MLA_HEADS * MLA_NOPE)
    rope = w3[..., MLA_NOPE:]
    zpad = jnp.zeros((r, MLA_HEADS, LANE - MLA_ROPE), w.dtype)
    ra = jnp.concatenate([rope, zpad], axis=-1).reshape(r, MLA_HEADS * LANE)
    rb = jnp.concatenate([_swap_halves(rope), zpad], axis=-1).reshape(r, MLA_HEADS * LANE)
    return jnp.concatenate([nope, ra, rb], axis=1).astype(BF16)


def _prep_w_in_ssm(w):
    d = w.shape[0]
    zx = w[:, :Z_DT]
    dt = w[:, Z_DT:].reshape(d, SSM_GROUPS, SSM_HG)
    dt = jnp.pad(dt, ((0, 0), (0, 0), (0, LANE - SSM_HG))).reshape(d, SSM_GROUPS * LANE)
    return jnp.concatenate([zx, dt], axis=1).astype(BF16)


def _rope_tables(pos):
    half = MLA_ROPE // 2
    inv = ROPE_BASE ** (-jnp.arange(half, dtype=F32) / half)
    ang = pos.astype(F32)[:, None] * inv[None, :]
    c, s = jnp.cos(ang), jnp.sin(ang)
    z = jnp.zeros((pos.shape[0], LANE - MLA_ROPE), F32)
    return jnp.concatenate([c, c, z], axis=1), jnp.concatenate([-s, s, z], axis=1)


def _q_prep_body(cq_ref, g_ref, w_ref, cos_ref, sin_ref, *outs, transposed):
    cn = _rms(cq_ref[...], g_ref[...]).astype(BF16)
    a = _dot(cn, w_ref[...])
    nw = MLA_HEADS * MLA_NOPE
    cos, sin = cos_ref[...], sin_ref[...]
    for h in range(MLA_HEADS):
        qn = a[:, h * LANE:(h + 1) * LANE] * MLA_SCALE
        ra = a[:, nw + h * LANE: nw + (h + 1) * LANE]
        rb = a[:, 2 * nw + h * LANE: 2 * nw + (h + 1) * LANE]
        qr = (ra * cos + rb * sin) * MLA_SCALE
        if transposed:
            (qt_ref,) = outs
            qt_ref[h, 0:MLA_NOPE, :] = qn.T.astype(BF16)
            qt_ref[h, MLA_NOPE:MLA_NOPE + LANE, :] = qr.T.astype(BF16)
        else:
            qn_ref, qr_ref = outs
            qn_ref[:, h * LANE:(h + 1) * LANE] = qn.astype(BF16)
            qr_ref[:, h * LANE:(h + 1) * LANE] = qr.astype(BF16)


def q_prep(p, g, wq, cos, sin, *, tm, transposed=False):
    t = p.shape[0]
    nw = MLA_HEADS * LANE
    if transposed:
        qw = MLA_NOPE + LANE
        out_specs = [pl.BlockSpec((MLA_HEADS, None, qw, tm), lambda i: (0, i, 0, 0))]
        out_shape = [jax.ShapeDtypeStruct((MLA_HEADS, t // tm, qw, tm), BF16)]
    else:
        out_specs = [pl.BlockSpec((tm, nw), lambda i: (i, 0)), pl.BlockSpec((tm, nw), lambda i: (i, 0))]
        out_shape = [jax.ShapeDtypeStruct((t, nw), BF16), jax.ShapeDtypeStruct((t, nw), BF16)]
    return pl.pallas_call(
        functools.partial(_q_prep_body, transposed=transposed),
        grid=(t // tm,),
        in_specs=[
            pl.BlockSpec((tm, MLA_Q_RANK), lambda i: (i, P_CQ // MLA_Q_RANK)),
            pl.BlockSpec((1, MLA_Q_RANK), lambda i: (0, 0)),
            pl.BlockSpec(wq.shape, lambda i: (0, 0)),
            pl.BlockSpec((tm, LANE), lambda i: (i, 0)),
            pl.BlockSpec((tm, LANE), lambda i: (i, 0)),
        ],
        out_specs=out_specs,
        out_shape=out_shape,
        compiler_params=_cparams(("parallel",)),
        name="q_prep",
    )(p, g.reshape(1, -1), wq, cos, sin)


def _kv_prep_body(ckv_ref, kr_ref, krsw_ref, g_ref, cos_ref, sin_ref, *rest, expand):
    cn = _rms(ckv_ref[...], g_ref[...])
    kr = kr_ref[...] * cos_ref[...] + krsw_ref[...] * sin_ref[...]
    if expand:
        w_ref, rows_ref, k_ref, vt_ref = rest
        kvw = MLA_NOPE + MLA_V
        kv = _dot(cn.astype(BF16), w_ref[...])
        krb = kr.astype(BF16)
        for h in range(MLA_HEADS):
            k_ref[:, h * kvw:h * kvw + MLA_NOPE] = kv[:, h * kvw:h * kvw + MLA_NOPE].astype(BF16)
            k_ref[:, h * kvw + MLA_NOPE:(h + 1) * kvw] = krb
            vt_ref[h] = kv[:, h * kvw + MLA_NOPE:(h + 1) * kvw].T.astype(BF16)
    else:
        (rows_ref,) = rest
    rows_ref[:, :MLA_KV_RANK] = cn
    rows_ref[:, MLA_KV_RANK:] = kr[:, :MLA_ROPE]


def kv_prep(p, g, cos, sin, *, tm, w_ukv=None):
    t = p.shape[0]
    expand = w_ukv is not None
    in_specs = [
        pl.BlockSpec((tm, MLA_KV_RANK), lambda i: (i, P_CKV // MLA_KV_RANK)),
        pl.BlockSpec((tm, LANE), lambda i: (i, P_KR // LANE)),
        pl.BlockSpec((tm, LANE), lambda i: (i, P_KRSW // LANE)),
        pl.BlockSpec((1, MLA_KV_RANK), lambda i: (0, 0)),
        pl.BlockSpec((tm, LANE), lambda i: (i, 0)),
        pl.BlockSpec((tm, LANE), lambda i: (i, 0)),
    ]
    args = [p, p, p, g.reshape(1, -1), cos, sin]
    out_specs = [pl.BlockSpec((tm, ROW_W), lambda i: (i, 0))]
    out_shape = [jax.ShapeDtypeStruct((t, ROW_W), F32)]
    if expand:
        nkv = w_ukv.shape[1]
        in_specs.append(pl.BlockSpec(w_ukv.shape, lambda i: (0, 0)))
        args.append(w_ukv)
        out_specs += [pl.BlockSpec((tm, nkv), lambda i: (i, 0)),
                      pl.BlockSpec((MLA_HEADS, None, MLA_V, tm), lambda i: (0, i, 0, 0))]
        out_shape += [jax.ShapeDtypeStruct((t, nkv), BF16),
                      jax.ShapeDtypeStruct((MLA_HEADS, t // tm, MLA_V, tm), BF16)]
    return pl.pallas_call(
        functools.partial(_kv_prep_body, expand=expand),
        grid=(t // tm,),
        in_specs=in_specs,
        out_specs=out_specs,
        out_shape=out_shape,
        compiler_params=_cparams(("parallel",)),
        name="kv_prep",
    )(*args)


def _attn_body(qt_ref, k_ref, vt_ref, o_ref, m_ref, l_ref, acc_ref, *, tq, hp):
    qi = pl.program_id(2)
    m_ref[...] = jnp.full(m_ref.shape, -jnp.inf, F32)
    l_ref[...] = jnp.zeros(l_ref.shape, F32)
    acc_ref[...] = jnp.zeros(acc_ref.shape, F32)
    kw = MLA_NOPE + LANE
    hs = range(hp)

    def block(ki, on_diagonal):
        rows = pl.ds(pl.multiple_of(ki * tq, tq), tq)
        s = [_dot(k_ref[rows, j * kw:(j + 1) * kw], qt_ref[j]) for j in hs]
        if on_diagonal:
            visible = (lax.broadcasted_iota(jnp.int32, (tq, tq), 0) <= lax.broadcasted_iota(jnp.int32, (tq, tq), 1))
            s = [jnp.where(visible, sj, -jnp.inf) for sj in s]
        m_old = [m_ref[j] for j in hs]
        m_new = [jnp.maximum(m_old[j], jnp.max(s[j], axis=0, keepdims=True)) for j in hs]
        p = [jnp.exp(s[j] - m_new[j]) for j in hs]
        pv = [_dot(vt_ref[j, ki], p[j].astype(BF16)) for j in hs]
        corr = [jnp.exp(m_old[j] - m_new[j]) for j in hs]
        l_new = [l_ref[j] * corr[j] + jnp.sum(p[j], axis=0, keepdims=True) for j in hs]
        acc_new = [acc_ref[j] * corr[j] + pv[j] for j in hs]
        for j in hs:
            m_ref[j] = m_new[j]
            l_ref[j] = l_new[j]
            acc_ref[j] = acc_new[j]

    def below_diagonal(ki, carry):
        block(ki, False)
        return carry

    lax.fori_loop(0, qi, below_diagonal, 0)
    block(qi, True)
    for j in hs:
        o_ref[:, j * LANE:(j + 1) * LANE] = (acc_ref[j] / l_ref[j]).T.astype(o_ref.dtype)


def mla_prompt_attention(qt, k, vt, *, bsz, seq, tq, hp):
    nq = seq // tq
    kw = MLA_NOPE + LANE
    return pl.pallas_call(
        functools.partial(_attn_body, tq=tq, hp=hp),
        grid=(bsz, MLA_HEADS // hp, nq),
        in_specs=[
            pl.BlockSpec((hp, None, kw, tq), lambda b, h, qi: (h, b * nq + qi, 0, 0)),
            pl.BlockSpec((seq, hp * kw), lambda b, h, qi: (b, h)),
            pl.BlockSpec((hp, nq, MLA_V, tq), lambda b, h, qi: (h, b, 0, 0)),
        ],
        out_specs=pl.BlockSpec((tq, hp * LANE), lambda b, h, qi: (b * nq + qi, h)),
        out_shape=jax.ShapeDtypeStruct((bsz * seq, MLA_HEADS * MLA_V), BF16),
        scratch_shapes=[pltpu.VMEM((hp, 1, tq), F32), pltpu.VMEM((hp, 1, tq), F32),
                        pltpu.VMEM((hp, MLA_V, tq), F32)],
        compiler_params=_cparams(("parallel", "parallel", "arbitrary")),
        name="mla_prompt_attention",
    )(qt, k, vt)


def _paged_body(pt_ref, qn_ref, qr_ref, rows_ref, w_ref, *rest, gp, nb, n_valid):
    page_refs = rest[:nb * gp]
    o_ref, ql_ref, qrr_ref, m_ref, l_ref, acc_ref = rest[nb * gp:]
    del pt_ref
    g = pl.program_id(1)
    rows_q = MLA_HEADS * SAMPLE_PAD
    kvw = MLA_NOPE + MLA_V

    @pl.when(g == 0)
    def _():
        for sb in range(nb):
            tok = slice(sb * SAMPLE_PAD, (sb + 1) * SAMPLE_PAD)
            for h in range(MLA_HEADS):
                w_uk = w_ref[:, h * kvw: h * kvw + MLA_NOPE]
                ql = _dot_nt(qn_ref[tok, h * LANE:(h + 1) * LANE], w_uk)
                ql_ref[sb, h * SAMPLE_PAD:(h + 1) * SAMPLE_PAD, :] = ql.astype(BF16)
                qrr_ref[sb, h * SAMPLE_PAD:(h + 1) * SAMPLE_PAD, :] = qr_ref[tok, h * LANE:(h + 1) * LANE]
        m_ref[...] = jnp.full(m_ref.shape, -jnp.inf, F32)
        l_ref[...] = jnp.zeros(l_ref.shape, F32)
        acc_ref[...] = jnp.zeros(acc_ref.shape, F32)

    def update(scores, pvs):
        sbs = range(nb)
        m_old = [m_ref[sb] for sb in sbs]
        m_new = [jnp.maximum(m_old[sb], jnp.max(scores[sb], axis=-1, keepdims=True)) for sb in sbs]
        p = [jnp.exp(scores[sb] - m_new[sb]) for sb in sbs]
        pv = [pvs[sb](p[sb].astype(BF16)) for sb in sbs]
        corr = [jnp.exp(m_old[sb] - m_new[sb]) for sb in sbs]
        return [(m_new[sb], l_ref[sb] * corr[sb] + jnp.sum(p[sb], axis=-1, keepdims=True),
                 acc_ref[sb] * corr[sb] + pv[sb]) for sb in sbs]

    def commit(new):
        for sb, (m_new, l_new, acc_new) in enumerate(new):
            m_ref[sb] = m_new
            l_ref[sb] = l_new
            acc_ref[sb] = acc_new

    lat_t = [jnp.concatenate([pr[:MLA_KV_RANK, :].astype(BF16) for pr in page_refs[sb * gp:(sb + 1) * gp]], axis=1)
             for sb in range(nb)]
    rk_t = [jnp.concatenate([pr[MLA_KV_RANK:, :].astype(BF16) for pr in page_refs[sb * gp:(sb + 1) * gp]], axis=1)
            for sb in range(nb)]
    scores = [_dot(ql_ref[sb], lat_t[sb]) + _dot(qrr_ref[sb][:, :MLA_ROPE], rk_t[sb]) for sb in range(nb)]
    commit(update(scores, [lambda p, lt=lt: _dot_nt(p, lt) for lt in lat_t]))

    @pl.when(g == pl.num_programs(1) - 1)
    def _():
        tq = lax.broadcasted_iota(jnp.int32, (rows_q, SAMPLE_PAD), 0) % SAMPLE_PAD
        tk = lax.broadcasted_iota(jnp.int32, (rows_q, SAMPLE_PAD), 1)
        visible = (tk <= tq) & (tk < n_valid)
        scores, pvs = [], []
        for sb in range(nb):
            tok = slice(sb * SAMPLE_PAD, (sb + 1) * SAMPLE_PAD)
            lat = rows_ref[tok, :MLA_KV_RANK].astype(BF16)
            rk = rows_ref[tok, MLA_KV_RANK:].astype(BF16)
            s = _dot_nt(ql_ref[sb], lat) + _dot_nt(qrr_ref[sb][:, :MLA_ROPE], rk)
            scores.append(jnp.where(visible, s, -jnp.inf))
            pvs.append(lambda p, lat=lat: _dot(p, lat))
        for sb, (_, l_new, acc_new) in enumerate(update(scores, pvs)):
            tok = slice(sb * SAMPLE_PAD, (sb + 1) * SAMPLE_PAD)
            out_lat = (acc_new / l_new).astype(BF16)
            for h in range(MLA_HEADS):
                w_uv = w_ref[:, h * kvw + MLA_NOPE:(h + 1) * kvw]
                o = _dot(out_lat[h * SAMPLE_PAD:(h + 1) * SAMPLE_PAD, :], w_uv)
                o_ref[tok, h * LANE:(h + 1) * LANE] = o.astype(o_ref.dtype)


def mla_paged_attention(qn, qr, rows, w_ukv, cache_t, page_table, *, n_valid, gp, nb):
    bsz, n_pages = page_table.shape
    nw = MLA_HEADS * LANE
    npg = n_pages // gp
    rq = MLA_HEADS * SAMPLE_PAD

    def page_spec(sb, i):
        return pl.BlockSpec((None, ROW_W, PAGE_SIZE),
                            lambda b, g, pt: (pt[(b * nb + sb) * n_pages + g * gp + i], 0, 0))

    grid_spec = pltpu.PrefetchScalarGridSpec(
        num_scalar_prefetch=1,
        grid=(bsz // nb, npg),
        in_specs=[
            pl.BlockSpec((nb * SAMPLE_PAD, nw), lambda b, g, pt: (b, 0)),
            pl.BlockSpec((nb * SAMPLE_PAD, nw), lambda b, g, pt: (b, 0)),
            pl.BlockSpec((nb * SAMPLE_PAD, ROW_W), lambda b, g, pt: (b, 0)),
            pl.BlockSpec(w_ukv.shape, lambda b, g, pt: (0, 0)),
        ] + [page_spec(sb, i) for sb in range(nb) for i in range(gp)],
        out_specs=pl.BlockSpec((nb * SAMPLE_PAD, nw), lambda b, g, pt: (b, 0)),
        scratch_shapes=[
            pltpu.VMEM((nb, rq, MLA_KV_RANK), BF16),
            pltpu.VMEM((nb, rq, LANE), BF16),
            pltpu.VMEM((nb, rq, 1), F32),
            pltpu.VMEM((nb, rq, 1), F32),
            pltpu.VMEM((nb, rq, MLA_KV_RANK), F32),
        ],
    )
    return pl.pallas_call(
        functools.partial(_paged_body, gp=gp, nb=nb, n_valid=n_valid),
        grid_spec=grid_spec,
        out_shape=jax.ShapeDtypeStruct((bsz * SAMPLE_PAD, nw), BF16),
        compiler_params=_cparams(("parallel", "arbitrary")),
        name="mla_paged_attention",
    )(page_table.reshape(-1), qn, qr, rows, w_ukv, *([cache_t] * (nb * gp)))


def _mlstm_body(q_ref, k_ref, v_ref, og_ref, gc_ref, gr_ref, bc_ref, br_ref, ng_ref, c0_ref, n0_ref, m0_ref,
                h_ref, c_ref, n_ref, m_ref, *, lc, n_valid, nb, single_chunk):
    if single_chunk:
        cs_ref, ns_ref, ms_ref = c0_ref, n0_ref, m0_ref
        m_ref[...] = m0_ref[...]
    else:
        cs_ref, ns_ref, ms_ref = c_ref, n_ref, m_ref

        @pl.when(pl.program_id(1) == 0)
        def _():
            c_ref[...] = c0_ref[...]
            n_ref[...] = n0_ref[...]
            m_ref[...] = m0_ref[...]

    nh = ML_HEADS
    row = lax.broadcasted_iota(jnp.int32, (lc, lc), 0)
    col = lax.broadcasted_iota(jnp.int32, (lc, lc), 1)
    causal = col <= row
    tril = causal.astype(F32)
    triu = (row <= col).astype(F32)
    seqs = []
    for sb in range(nb):
        tok = slice(sb * lc, (sb + 1) * lc)
        gc = gc_ref[tok, :] + bc_ref[...]
        gr = gr_ref[sb] + br_ref[:, 0:1]
        li_c, lf_c = gc[:, 0:nh], _log_sigmoid(gc[:, nh:2 * nh])
        li_r, lf_r = gr[0:nh, :], _log_sigmoid(gr[nh:2 * nh, :])
        if n_valid < lc:
            vc = lax.broadcasted_iota(jnp.int32, (lc, nh), 0) < n_valid
            vr = lax.broadcasted_iota(jnp.int32, (nh, lc), 1) < n_valid
            li_c, lf_c = jnp.where(vc, li_c, NEG_BIG), jnp.where(vc, lf_c, 0.0)
            li_r, lf_r = jnp.where(vr, li_r, NEG_BIG), jnp.where(vr, lf_r, 0.0)
        seqs.append((li_c, li_r, lf_c, lf_r))
    b_cs = [jnp.dot(tril, s[2], precision=HIGHEST, preferred_element_type=F32) for s in seqs]
    b_rs = [jnp.dot(s[3], triu, precision=HIGHEST, preferred_element_type=F32) for s in seqs]

    items = [(sb, h) for sb in range(nb) for h in range(nh)]
    st = []
    for sb, h in items:
        tok = slice(sb * lc, (sb + 1) * lc)
        qh = q_ref[tok, h * ML_DQK:(h + 1) * ML_DQK] * (ML_DQK ** -0.5)
        kh = k_ref[tok, h * ML_DQK:(h + 1) * ML_DQK]
        st.append(dict(
            bc=b_cs[sb][:, h:h + 1], br=b_rs[sb][h:h + 1, :],
            ic=seqs[sb][0][:, h:h + 1], ir=seqs[sb][1][h:h + 1, :],
            m_prev=ms_ref[sb, h:h + 1, 0:1], qh=qh, qb=qh.astype(BF16), kh=kh, kb=kh.astype(BF16),
            vh=v_ref[tok, h * ML_DV:(h + 1) * ML_DV], og=og_ref[tok, h * ML_DV:(h + 1) * ML_DV],
            c=cs_ref[sb, h], n=ns_ref[sb, h:h + 1, :], ng=ng_ref[:, h * ML_DV:(h + 1) * ML_DV]))
    for d in st:
        d["qk"] = _dot_nt(d["qb"], d["kb"])
        d["qc"] = _dot_nt(d["qb"], d["c"].astype(BF16))
    for d in st:
        dmat = jnp.where(causal, d["bc"] - d["br"] + d["ir"], -jnp.inf)
        inter = d["bc"] + d["m_prev"]
        d["m_t"] = jnp.maximum(inter, jnp.max(dmat, axis=-1, keepdims=True))
        d["w_inter"] = jnp.exp(inter - d["m_t"])
        d["s"] = d["qk"] * jnp.exp(dmat - d["m_t"])
        b_last = d["bc"][lc - 1:lc, :]
        g_c = b_last - d["bc"] + d["ic"]
        g_r = b_last - d["br"] + d["ir"]
        d["m_new"] = jnp.maximum(b_last + d["m_prev"], jnp.max(g_r, axis=-1, keepdims=True))
        d["decay"] = jnp.exp(b_last + d["m_prev"] - d["m_new"])
        d["wg"] = jnp.exp(g_c - d["m_new"])
    for d in st:
        d["sv"] = _dot(d["s"].astype(BF16), d["vh"].astype(BF16))
        d["vk"] = _dot_tn((d["vh"] * d["wg"]).astype(BF16), d["kb"])
    for d in st:
        num = d["sv"] + d["w_inter"] * d["qc"]
        den = (jnp.sum(d["s"], axis=-1, keepdims=True)
               + d["w_inter"] * jnp.sum(d["qh"] * d["n"], axis=-1, keepdims=True))
        hout = num / jnp.maximum(jnp.abs(den), jnp.exp(-d["m_t"]))
        d["h_new"] = (_sigmoid(d["og"]) * _rms(hout, d["ng"])).astype(h_ref.dtype)
        d["c_new"] = d["decay"] * d["c"] + d["vk"]
        d["n_new"] = d["decay"] * d["n"] + jnp.sum(d["kh"] * d["wg"], axis=0, keepdims=True)

    for (sb, h), d in zip(items, st):
        h_ref[sb * lc:(sb + 1) * lc, h * ML_DV:(h + 1) * ML_DV] = d["h_new"]
        c_ref[sb, h] = d["c_new"]
        n_ref[sb, h:h + 1, :] = d["n_new"]
        m_ref[sb, h:h + 1, :] = jnp.broadcast_to(d["m_new"], (1, LANE))


def mlstm(p, gates_rows, b_ig, b_fg, norm_g, c0, n0, m0, *, bsz, lc, nchunk, n_valid, nb):
    assert nb == 1 or nchunk == 1
    nh = ML_HEADS
    bias_c = jnp.concatenate([b_ig, b_fg, jnp.zeros((LANE - 2 * nh,), F32)]).reshape(1, LANE)
    bias_r = jnp.broadcast_to(jnp.concatenate([b_ig, b_fg]).reshape(2 * nh, 1), (2 * nh, LANE))
    tl = nb * lc

    def rows(cb):
        return lambda b, c: (b * nchunk + c, cb)

    const2 = lambda b, c: (0, 0)
    return pl.pallas_call(
        functools.partial(_mlstm_body, lc=lc, n_valid=n_valid, nb=nb, single_chunk=nchunk == 1),
        grid=(bsz // nb, nchunk),
        in_specs=[
            pl.BlockSpec((tl, nh * ML_DQK), rows(P_Q // (nh * ML_DQK))),
            pl.BlockSpec((tl, nh * ML_DQK), rows(P_K // (nh * ML_DQK))),
            pl.BlockSpec((tl, nh * ML_DV), rows(P_V // (nh * ML_DV))),
            pl.BlockSpec((tl, nh * ML_DV), rows(P_OG // (nh * ML_DV))),
            pl.BlockSpec((tl, LANE), rows(P_GATE // LANE)),
            pl.BlockSpec((nb, 2 * nh, lc), lambda b, c: (b * nchunk + c, 0, 0)),
            pl.BlockSpec((1, LANE), const2),
            pl.BlockSpec((2 * nh, LANE), const2),
            pl.BlockSpec((1, nh * ML_DV), const2),
            pl.BlockSpec((nb, nh, ML_DV, ML_DQK), lambda b, c: (b, 0, 0, 0)),
            pl.BlockSpec((nb, nh, ML_DQK), lambda b, c: (b, 0, 0)),
            pl.BlockSpec((nb, 2 * nh, LANE), lambda b, c: (b, 0, 0)),
        ],
        out_specs=[
            pl.BlockSpec((tl, nh * ML_DV), lambda b, c: (b * nchunk + c, 0)),
            pl.BlockSpec((nb, nh, ML_DV, ML_DQK), lambda b, c: (b, 0, 0, 0)),
            pl.BlockSpec((nb, nh, ML_DQK), lambda b, c: (b, 0, 0)),
            pl.BlockSpec((nb, 2 * nh, LANE), lambda b, c: (b, 0, 0)),
        ],
        out_shape=[
            jax.ShapeDtypeStruct((bsz * nchunk * lc, nh * ML_DV), BF16),
            jax.ShapeDtypeStruct((bsz, nh, ML_DV, ML_DQK), F32),
            jax.ShapeDtypeStruct((bsz, nh, ML_DQK), F32),
            jax.ShapeDtypeStruct((bsz, 2 * nh, LANE), F32),
        ],
        compiler_params=_cparams(("parallel", "arbitrary")),
        name="mlstm",
    )(p, p, p, p, p, gates_rows, bias_c, bias_r, norm_g.reshape(1, -1), c0, n0, m0)


def _gates_rows(p, lc):
    nrows = p.shape[0]
    g = p[:, P_GATE:P_GATE + 2 * ML_HEADS]
    return g.reshape(nrows // lc, lc, 2 * ML_HEADS).transpose(0, 2, 1)


def _m_pack(m):
    b = m.shape[0]
    mp = jnp.pad(m.astype(F32), ((0, 0), (0, 2 * ML_HEADS - m.shape[1])))
    return jnp.broadcast_to(mp[:, :, None], (b, 2 * ML_HEADS, LANE))


def _ssd_body(z_ref, x_ref, b_ref, c_ref, dtc_ref, dtr_ref, cwx_ref, cwb_ref, cwc_ref, cbx_ref, cbb_ref, cbc_ref,
              csx_ref, csb_ref, csc_ref, pc_ref, pr_ref, dsk_ref, ng_ref, h0_ref,
              y_ref, h_ref, bufx, bufb, bufc, *, lc, n_valid, gs, single_chunk):
    tail = SAMPLE_PAD
    cxs = [slice(gi * SSM_GW, (gi + 1) * SSM_GW) for gi in range(gs)]
    cns = [slice(gi * SSM_STATE, (gi + 1) * SSM_STATE) for gi in range(gs)]

    def init_history():
        for gi in range(gs):
            bufx[gi, 0:tail, :] = csx_ref[:, cxs[gi]]
            bufb[gi, 0:tail, :] = csb_ref[:, cns[gi]]
            bufc[gi, 0:tail, :] = csc_ref[:, cns[gi]]

    if single_chunk:
        init_history()
        hs_ref = h0_ref
    else:
        hs_ref = h_ref

        @pl.when(pl.program_id(2) == 0)
        def _():
            h_ref[...] = h0_ref[...]
            init_history()

    def conv(raw, buf, cw_ref, cb_ref, cols):
        buf[tail:tail + lc, :] = raw
        acc = cb_ref[:, cols] + cw_ref[SSM_CONV - 1:SSM_CONV, cols] * raw
        for j in range(SSM_CONV - 1):
            off = tail - (SSM_CONV - 1) + j
            acc = acc + cw_ref[j:j + 1, cols] * buf[off:off + lc, :]
        if not single_chunk:
            buf[0:tail, :] = raw[lc - tail:lc, :]
        return _silu(acc)

    convs = []
    for gi in range(gs):
        convs.append((conv(x_ref[:, cxs[gi]], bufx.at[gi], cwx_ref, cbx_ref, cxs[gi]),
                      conv(b_ref[:, cns[gi]], bufb.at[gi], cwb_ref, cbb_ref, cns[gi]),
                      conv(c_ref[:, cns[gi]], bufc.at[gi], cwc_ref, cbc_ref, cns[gi])))

    outs = _ssd_groups(
        convs,
        [dict(z=z_ref[:, cxs[gi]], dtc=dtc_ref[:, gi * LANE:gi * LANE + SSM_HG],
              dtr=dtr_ref[gi * SSM_HG:(gi + 1) * SSM_HG, :], pc=pc_ref[gi], pr=pr_ref[gi],
              dsk=dsk_ref[:, cxs[gi]], ng=ng_ref[:, cxs[gi]], h=hs_ref[gi]) for gi in range(gs)],
        lc=lc, n_valid=n_valid)

    for gi, (y, h_new) in enumerate(outs):
        y_ref[:, cxs[gi]] = y.astype(y_ref.dtype)
        h_ref[gi] = h_new


def _ssd_groups(convs, ins, *, lc, n_valid):
    hg = SSM_HG
    gs = range(len(ins))
    row = lax.broadcasted_iota(jnp.int32, (lc, lc), 0)
    col = lax.broadcasted_iota(jnp.int32, (lc, lc), 1)
    causal = col <= row
    tril = causal.astype(F32)
    triu = (row <= col).astype(F32)
    expand = (lax.broadcasted_iota(jnp.int32, (hg, SSM_GW), 1) // SSM_HEADDIM
              == lax.broadcasted_iota(jnp.int32, (hg, SSM_GW), 0)).astype(F32)
    lane_head = lax.broadcasted_iota(jnp.int32, (lc, LANE), 1) // SSM_HEADDIM

    def hi(a, b):
        return jnp.dot(a, b, precision=HIGHEST, preferred_element_type=F32)

    dt_c, dt_r = [], []
    for g in gs:
        c = _softplus(ins[g]["dtc"] + ins[g]["pc"][0:1, 0:hg])
        r = _softplus(ins[g]["dtr"] + ins[g]["pr"][:, 0:1])
        if n_valid < lc:
            c = jnp.where(lax.broadcasted_iota(jnp.int32, (lc, hg), 0) < n_valid, c, 0.0)
            r = jnp.where(lax.broadcasted_iota(jnp.int32, (hg, lc), 1) < n_valid, r, 0.0)
        dt_c.append(c)
        dt_r.append(r)
    cum_c = [hi(tril, dt_c[g] * (-jnp.exp(ins[g]["pc"][1:2, 0:hg]))) for g in gs]
    cum_r = [hi(dt_r[g] * (-jnp.exp(ins[g]["pr"][:, 1:2])), triu) for g in gs]
    cb16 = [convs[g][2].astype(BF16) for g in gs]
    bb16 = [convs[g][1].astype(BF16) for g in gs]
    cbm = [_dot_nt(cb16[g], bb16[g]) for g in gs]
    yh = [_dot_nt(cb16[g], ins[g]["h"].astype(BF16)) for g in gs]
    e_in = [hi(jnp.exp(cum_c[g]), expand) for g in gs]
    e_st = [hi(jnp.exp(cum_c[g][lc - 1:lc, :] - cum_c[g]) * dt_c[g], expand) for g in gs]
    upd = [_dot_tn((convs[g][0] * e_st[g]).astype(BF16), bb16[g]) for g in gs]

    pieces = [[] for _ in gs]
    for pair in range(hg // 2):
        parts = [[] for _ in gs]
        for sub in range(2):
            j = 2 * pair + sub
            for g in gs:
                seg = jnp.where(causal, cum_c[g][:, j:j + 1] - cum_r[g][j:j + 1, :], -jnp.inf)
                w = cbm[g] * jnp.exp(seg) * dt_r[g][j:j + 1, :]
                xm = jnp.where(lane_head == sub, convs[g][0][:, pair * LANE:(pair + 1) * LANE], 0.0)
                parts[g].append(_dot(w.astype(BF16), xm.astype(BF16)))
        for g in gs:
            pieces[g].append(parts[g][0] + parts[g][1])

    outs = []
    for g in gs:
        xa, h_old = convs[g][0], ins[g]["h"]
        y = yh[g] * e_in[g] + jnp.concatenate(pieces[g], axis=1)
        dec_r = jnp.exp(cum_r[g][:, lc - 1:lc])
        h_new = jnp.concatenate(
            [dec_r[j:j + 1, :] * h_old[j * SSM_HEADDIM:(j + 1) * SSM_HEADDIM, :]
             + upd[g][j * SSM_HEADDIM:(j + 1) * SSM_HEADDIM, :] for j in range(hg)], axis=0)
        y = (y + ins[g]["dsk"] * xa) * _silu(ins[g]["z"])
        outs.append((_rms(y, ins[g]["ng"]), h_new))
    return outs


def ssd(zp, dt_rows, conv_w, conv_b, conv_state8, dt_bias, a_log, d_skip, norm_g, h0, *, bsz, lc, nchunk,
        n_valid, gs):
    g_, hg = SSM_GROUPS // gs, SSM_HG * gs
    gw, ns, lw = SSM_GW * gs, SSM_STATE * gs, LANE * gs

    def rows(off, width):
        assert off % width == 0
        return lambda b, g, c: (b * nchunk + c, off // width + g)

    ng_all, nh1 = SSM_GROUPS, SSM_HG
    pcol = jnp.zeros((ng_all, 8, LANE), F32)
    pcol = pcol.at[:, 0, :nh1].set(dt_bias.reshape(ng_all, nh1)).at[:, 1, :nh1].set(a_log.reshape(ng_all, nh1))
    prow = jnp.zeros((ng_all, nh1, LANE), F32)
    prow = prow.at[:, :, 0].set(dt_bias.reshape(ng_all, nh1)).at[:, :, 1].set(a_log.reshape(ng_all, nh1))
    dsk = jnp.repeat(d_skip, SSM_HEADDIM).reshape(1, SSM_D_INNER)
    cb2 = conv_b.reshape(1, -1)
    bo, co = SSM_D_INNER // ns, SSM_D_INNER // ns + g_
    assert SSM_D_INNER % ns == 0

    in_specs = [
        pl.BlockSpec((lc, gw), rows(0, gw)),
        pl.BlockSpec((lc, gw), rows(Z_X, gw)),
        pl.BlockSpec((lc, ns), rows(Z_B, ns)),
        pl.BlockSpec((lc, ns), rows(Z_C, ns)),
        pl.BlockSpec((lc, lw), rows(Z_DT, lw)),
        pl.BlockSpec((None, hg, lc), lambda b, g, c: (b * nchunk + c, g, 0)),
        pl.BlockSpec((SSM_CONV, gw), lambda b, g, c: (0, g)),
        pl.BlockSpec((SSM_CONV, ns), lambda b, g, c: (0, bo + g)),
        pl.BlockSpec((SSM_CONV, ns), lambda b, g, c: (0, co + g)),
        pl.BlockSpec((1, gw), lambda b, g, c: (0, g)),
        pl.BlockSpec((1, ns), lambda b, g, c: (0, bo + g)),
        pl.BlockSpec((1, ns), lambda b, g, c: (0, co + g)),
        pl.BlockSpec((None, 8, gw), lambda b, g, c: (b, 0, g)),
        pl.BlockSpec((None, 8, ns), lambda b, g, c: (b, 0, bo + g)),
        pl.BlockSpec((None, 8, ns), lambda b, g, c: (b, 0, co + g)),
        pl.BlockSpec((gs, 8, LANE), lambda b, g, c: (g, 0, 0)),
        pl.BlockSpec((gs, nh1, LANE), lambda b, g, c: (g, 0, 0)),
        pl.BlockSpec((1, gw), lambda b, g, c: (0, g)),
        pl.BlockSpec((1, gw), lambda b, g, c: (0, g)),
        pl.BlockSpec((None, gs, SSM_GW, SSM_STATE), lambda b, g, c: (b, g, 0, 0)),
    ]
    return pl.pallas_call(
        functools.partial(_ssd_body, lc=lc, n_valid=n_valid, gs=gs, single_chunk=nchunk == 1),
        grid=(bsz, g_, nchunk),
        in_specs=in_specs,
        out_specs=[
            pl.BlockSpec((lc, gw), lambda b, g, c: (b * nchunk + c, g)),
            pl.BlockSpec((None, gs, SSM_GW, SSM_STATE), lambda b, g, c: (b, g, 0, 0)),
        ],
        out_shape=[
            jax.ShapeDtypeStruct((bsz * nchunk * lc, SSM_D_INNER), BF16),
            jax.ShapeDtypeStruct((bsz, SSM_GROUPS, SSM_GW, SSM_STATE), F32),
        ],
        scratch_shapes=[
            pltpu.VMEM((gs, lc + SAMPLE_PAD, SSM_GW), F32),
            pltpu.VMEM((gs, lc + SAMPLE_PAD, SSM_STATE), F32),
            pltpu.VMEM((gs, lc + SAMPLE_PAD, SSM_STATE), F32),
        ],
        compiler_params=_cparams(("parallel", "parallel", "arbitrary")),
        name="ssd",
    )(zp, zp, zp, zp, zp, dt_rows, conv_w, conv_w, conv_w, cb2, cb2, cb2, conv_state8, conv_state8, conv_state8,
      pcol, prow, dsk, norm_g.reshape(1, -1), h0)


def _dt_rows(zp, lc):
    nrows = zp.shape[0]
    dt = zp[:, Z_DT:].reshape(nrows, SSM_GROUPS, LANE)[:, :, :SSM_HG].reshape(nrows // lc, lc, SSM_HEADS)
    return dt.transpose(0, 2, 1)


def _conv_state8(state):
    return jnp.pad(state, ((0, 0), (SAMPLE_PAD - (SSM_CONV - 1), 0), (0, 0)))


def _trunk(x_prompt, x_sample, cache_mla, state_mlstm_C, state_mlstm_n, state_mlstm_m, state_ssm, state_conv,
           page_table, norm_g, ffn_w_gate, ffn_w_up, ffn_w_down, w_in_even, b_ig, b_fg, mlstm_norm_g, mla_q_norm_g,
           mla_kv_norm_g, w_uq, w_ukv, w_out_even, w_in_ssm, conv_w, conv_b, dt_bias, A_log, D_skip, ssm_norm_g,
           w_out_ssm, final_norm_g, *, tm, tf, tn, tmp, lc_ml, nb_ml, lc_ssd, gs_p, gs_s, tq, hp, gp, nb_pg):
    bp, sp, d = x_prompt.shape
    bs, ts, _ = x_sample.shape
    tp = bp * sp
    tsp = bs * SAMPLE_PAD
    xs = [x_prompt.reshape(tp, d), x_sample.reshape(bs * ts, d)]
    tms = [min(tm, tp), min(tm, bs * ts)]
    tmps = [min(tmp, tp), min(tmp, tsp)]
    assert tmps[0] == tq
    tables = [_rope_tables(jnp.tile(jnp.arange(sp), bp)),
              _rope_tables(jnp.tile(PAST_LEN + jnp.arange(SAMPLE_PAD), bs))]

    def pad_seq(a):
        return jnp.pad(a.reshape(bs, ts, a.shape[1]), ((0, 0), (0, SAMPLE_PAD - ts), (0, 0))).reshape(tsp, a.shape[1])

    def unpad_seq(a):
        return a.reshape(bs, SAMPLE_PAD, a.shape[1])[:, :ts].reshape(bs * ts, a.shape[1])

    def ff(xs, g, layer, which, final_g=None):
        return [ffn(x, g, ffn_w_gate, ffn_w_up, ffn_w_down, layer, which, tm=t, tf=tf, final_g=final_g)
                for x, t in zip(xs, tms)]

    xs = ff(xs, norm_g[0, 0], 0, 0)
    w_in = _prep_w_in_even(w_in_even)
    p_p, p_s = [norm_matmul(x, norm_g[0, 1], w_in, tm=t, tn=tn) for x, t in zip(xs, tms)]
    p_s = pad_seq(p_s)

    zeros = functools.partial(jnp.zeros, dtype=F32)
    hm_p, c_p, n_p, m_p = mlstm(
        p_p, _gates_rows(p_p, lc_ml), b_ig, b_fg, mlstm_norm_g,
        zeros((bp, ML_HEADS, ML_DV, ML_DQK)), zeros((bp, ML_HEADS, ML_DQK)), zeros((bp, 2 * ML_HEADS, LANE)),
        bsz=bp, lc=lc_ml, nchunk=sp // lc_ml, n_valid=lc_ml, nb=1)
    hm_s, c_s, n_s, m_s = mlstm(
        p_s, _gates_rows(p_s, SAMPLE_PAD), b_ig, b_fg, mlstm_norm_g,
        state_mlstm_C, state_mlstm_n, _m_pack(state_mlstm_m),
        bsz=bs, lc=SAMPLE_PAD, nchunk=1, n_valid=ts, nb=nb_ml)

    w_ukv16 = w_ukv.astype(BF16)
    w_q = _prep_w_uq(w_uq)
    (qt_p,) = q_prep(p_p, mla_q_norm_g, w_q, *tables[0], tm=tmps[0], transposed=True)
    qn_s, qr_s = q_prep(p_s, mla_q_norm_g, w_q, *tables[1], tm=tmps[1])
    rows_p, k_p, vt_p = kv_prep(p_p, mla_kv_norm_g, *tables[0], tm=tmps[0], w_ukv=w_ukv16)
    (rows_s,) = kv_prep(p_s, mla_kv_norm_g, *tables[1], tm=tmps[1])
    ha_p = mla_prompt_attention(qt_p, k_p, vt_p, bsz=bp, seq=sp, tq=tq, hp=hp)
    cache_t = jnp.swapaxes(cache_mla, 1, 2)
    ha_s = mla_paged_attention(qn_s, qr_s, rows_s, w_ukv16, cache_t, page_table, n_valid=ts, gp=gp, nb=nb_pg)

    w_oe = w_out_even.astype(BF16)
    xs = [matmul_residual([hm, ha], w_oe, x, tm=t, tn=tn)
          for hm, ha, x, t in zip((hm_p, unpad_seq(hm_s)), (ha_p, unpad_seq(ha_s)), xs, tms)]
    xs = ff(xs, norm_g[0, 2], 0, 1)

    xs = ff(xs, norm_g[1, 0], 1, 0)
    w_is = _prep_w_in_ssm(w_in_ssm)
    z_p, z_s4 = [norm_matmul(x, norm_g[1, 1], w_is, tm=t, tn=tn) for x, t in zip(xs, tms)]
    z_s = pad_seq(z_s4)
    y_p, ssm_p = ssd(
        z_p, _dt_rows(z_p, lc_ssd), conv_w, conv_b, zeros((bp, SAMPLE_PAD, SSM_CONV_DIM)), dt_bias, A_log,
        D_skip, ssm_norm_g, zeros((bp, SSM_GROUPS, SSM_GW, SSM_STATE)),
        bsz=bp, lc=lc_ssd, nchunk=sp // lc_ssd, n_valid=lc_ssd, gs=gs_p)
    y_s, ssm_s = ssd(
        z_s, _dt_rows(z_s, SAMPLE_PAD), conv_w, conv_b, _conv_state8(state_conv), dt_bias, A_log,
        D_skip, ssm_norm_g, state_ssm.reshape(bs, SSM_GROUPS, SSM_GW, SSM_STATE),
        bsz=bs, lc=SAMPLE_PAD, nchunk=1, n_valid=ts, gs=gs_s)
    w_os = w_out_ssm.astype(BF16)
    xs = [matmul_residual([y], w_os, x, tm=t, tn=tn) for y, x, t in zip((y_p, unpad_seq(y_s)), xs, tms)]
    out_p, out_s = ff(xs, norm_g[1, 2], 1, 1, final_g=final_norm_g)

    keep = SSM_CONV - 1
    assert sp >= keep and ts >= keep
    conv_p = jnp.stack([lax.slice(z_p, (b * sp + sp - keep, Z_X), (b * sp + sp, Z_DT)) for b in range(bp)])
    conv_s = z_s4[:, Z_X:Z_DT].reshape(bs, ts, SSM_CONV_DIM)[:, ts - keep:ts]
    hshape = (SSM_HEADS, SSM_HEADDIM, SSM_STATE)
    return (out_p.reshape(bp, sp, d), out_s.reshape(bs, ts, d), rows_p.reshape(bp, sp, ROW_W),
            rows_s.reshape(bs, SAMPLE_PAD, ROW_W)[:, :ts],
            c_p, c_s, n_p, n_s, m_p[:, :ML_HEADS, 0], m_s[:, :ML_HEADS, 0],
            ssm_p.reshape((bp,) + hshape), ssm_s.reshape((bs,) + hshape), conv_p, conv_s)


def kernel(x_prompt, x_sample, cache_mla, state_mlstm_C, state_mlstm_n, state_mlstm_m, state_ssm, state_conv, page_table, norm_g, ffn_w_gate, ffn_w_up, ffn_w_down, w_in_even, b_ig, b_fg, mlstm_norm_g, mla_q_norm_g, mla_kv_norm_g, w_uq, w_ukv, w_out_even, w_in_ssm, conv_w, conv_b, dt_bias, A_log, D_skip, ssm_norm_g, w_out_ssm, final_norm_g):
    return _trunk(x_prompt, x_sample, cache_mla, state_mlstm_C, state_mlstm_n, state_mlstm_m, state_ssm, state_conv,
                  page_table, norm_g, ffn_w_gate, ffn_w_up, ffn_w_down, w_in_even, b_ig, b_fg, mlstm_norm_g,
                  mla_q_norm_g, mla_kv_norm_g, w_uq, w_ukv, w_out_even, w_in_ssm, conv_w, conv_b, dt_bias, A_log,
                  D_skip, ssm_norm_g, w_out_ssm, final_norm_g,
                  tm=1024, tf=256, tn=512, tmp=512, lc_ml=256, nb_ml=4, lc_ssd=128, gs_p=4, gs_s=8,
                  tq=512, hp=2, gp=32, nb_pg=2)
```

```python
import functools
import math

import jax
import jax.numpy as jnp
from jax import lax
from jax.experimental import pallas as pl
from jax.experimental.pallas import tpu as pltpu

F32 = jnp.float32
BF16 = jnp.bfloat16

D_MODEL = 2048
D_FF = 5632
EPS = 1e-6
PAST_LEN = 8192
PAGE_SIZE = 128

ML_HEADS = 4
ML_DQK = 128
ML_DV = 256

MLA_HEADS = 8
MLA_NOPE = 128
MLA_ROPE = 64
MLA_V = 128
MLA_Q_RANK = 512
MLA_KV_RANK = 512
MLA_SCALE = (MLA_NOPE + MLA_ROPE) ** -0.5
ROPE_BASE = 10000.0
ROW_W = MLA_KV_RANK + MLA_ROPE

SSM_D_INNER = 4096
SSM_HEADDIM = 64
SSM_HEADS = 64
SSM_GROUPS = 8
SSM_HG = SSM_HEADS // SSM_GROUPS
SSM_GW = SSM_D_INNER // SSM_GROUPS
SSM_STATE = 128
SSM_CONV = 4
SSM_CONV_DIM = SSM_D_INNER + 2 * SSM_GROUPS * SSM_STATE

SAMPLE_PAD = 8
LANE = 128
VMEM_LIMIT = 56 * 1024 * 1024

P_Q, P_K, P_V, P_OG, P_CQ, P_CKV = 0, 512, 1024, 2048, 3072, 3584
P_KR, P_KRSW, P_GATE, P_WIDTH = 4096, 4224, 4352, 4608
Z_X, Z_B, Z_C, Z_DT, Z_WIDTH = 4096, 8192, 9216, 10240, 10368

NEG_BIG = -1e30


def _cparams(sem):
    return pltpu.CompilerParams(dimension_semantics=sem, vmem_limit_bytes=VMEM_LIMIT)


def _rms(x, g):
    return x * lax.rsqrt(jnp.mean(x * x, axis=-1, keepdims=True) + EPS) * g


def _sigmoid(x):
    return 1.0 / (1.0 + jnp.exp(-x))


def _silu(x):
    return x * _sigmoid(x)


def _log_sigmoid(x):
    return jnp.minimum(x, 0.0) - jnp.log1p(jnp.exp(-jnp.abs(x)))


def _softplus(x):
    return jnp.maximum(x, 0.0) + jnp.log1p(jnp.exp(-jnp.abs(x)))


def _dot(a, b):
    return jnp.dot(a, b, preferred_element_type=F32)


def _dot_nt(a, b):
    return lax.dot_general(a, b, (((1,), (1,)), ((), ())), preferred_element_type=F32)


def _dot_tn(a, b):
    return lax.dot_general(a, b, (((0,), (0,)), ((), ())), preferred_element_type=F32)


def _split3(x):
    x1 = x.astype(BF16)
    r1 = x - x1.astype(F32)
    x2 = r1.astype(BF16)
    x3 = (r1 - x2.astype(F32)).astype(BF16)
    return x1, x2, x3


def _dot_sel_lhs(sel, x):
    s = sel.astype(BF16)
    x1, x2, x3 = _split3(x)
    return (_dot(s, x3) + _dot(s, x2)) + _dot(s, x1)


def _dot_sel_rhs(x, sel):
    s = sel.astype(BF16)
    x1, x2, x3 = _split3(x)
    return (_dot(x3, s) + _dot(x2, s)) + _dot(x1, s)


def _ffn_body(x_ref, g_ref, wg_ref, wu_ref, wd_ref, *rest, final_norm):
    if final_norm:
        gf_ref, o_ref, xn_ref = rest
    else:
        o_ref, xn_ref = rest
    j = pl.program_id(1)

    @pl.when(j == 0)
    def _():
        x = x_ref[...]
        xn_ref[...] = _rms(x, g_ref[...]).astype(BF16)
        o_ref[...] = x

    xn = xn_ref[...]
    hg = _dot(xn, wg_ref[...].astype(BF16))
    hu = _dot(xn, wu_ref[...].astype(BF16))
    h = _silu(hg) * (0.5 * hu)
    o_ref[...] += _dot(h.astype(BF16), wd_ref[...].astype(BF16))

    if final_norm:
        @pl.when(j == pl.num_programs(1) - 1)
        def _():
            o_ref[...] = _rms(o_ref[...], gf_ref[...])


def ffn(x, g, w_gate, w_up, w_down, layer, which, *, tm, tf, final_g=None):
    t, d = x.shape
    dff = w_gate.shape[-1]
    final_norm = final_g is not None
    in_specs = [
        pl.BlockSpec((tm, d), lambda i, j: (i, 0), pipeline_mode=pl.Buffered(1)),
        pl.BlockSpec((1, d), lambda i, j: (0, 0)),
        pl.BlockSpec((None, None, d, tf), lambda i, j: (layer, which, 0, j)),
        pl.BlockSpec((None, None, d, tf), lambda i, j: (layer, which, 0, j)),
        pl.BlockSpec((None, None, tf, d), lambda i, j: (layer, which, j, 0)),
    ]
    args = [x, g.reshape(1, d), w_gate, w_up, w_down]
    if final_norm:
        in_specs.append(pl.BlockSpec((1, d), lambda i, j: (0, 0)))
        args.append(final_g.reshape(1, d))
    return pl.pallas_call(
        functools.partial(_ffn_body, final_norm=final_norm),
        grid=(t // tm, dff // tf),
        in_specs=in_specs,
        out_specs=pl.BlockSpec((tm, d), lambda i, j: (i, 0)),
        out_shape=jax.ShapeDtypeStruct((t, d), F32),
        scratch_shapes=[pltpu.VMEM((tm, d), BF16)],
        compiler_params=_cparams(("parallel", "arbitrary")),
        name="ffn",
    )(*args)


def _norm_mm_body(x_ref, g_ref, w_ref, o_ref, xn_ref):
    @pl.when(pl.program_id(1) == 0)
    def _():
        xn_ref[...] = _rms(x_ref[...], g_ref[...]).astype(BF16)

    o_ref[...] = _dot(xn_ref[...], w_ref[...].astype(BF16))


def norm_matmul(x, g, w, *, tm, tn):
    t, d = x.shape
    n = w.shape[1]
    return pl.pallas_call(
        _norm_mm_body,
        grid=(t // tm, n // tn),
        in_specs=[
            pl.BlockSpec((tm, d), lambda i, j: (i, 0)),
            pl.BlockSpec((1, d), lambda i, j: (0, 0)),
            pl.BlockSpec((d, tn), lambda i, j: (0, j)),
        ],
        out_specs=pl.BlockSpec((tm, tn), lambda i, j: (i, j)),
        out_shape=jax.ShapeDtypeStruct((t, n), F32),
        scratch_shapes=[pltpu.VMEM((tm, d), BF16)],
        compiler_params=_cparams(("parallel", "arbitrary")),
        name="norm_matmul",
    )(x, g.reshape(1, d), w)


def _mm_res_body(*refs, n_a):
    a_refs, w_refs = refs[:n_a], refs[n_a:2 * n_a]
    res_ref, o_ref = refs[2 * n_a], refs[2 * n_a + 1]
    acc = res_ref[...]
    for a_ref, w_ref in zip(a_refs, w_refs):
        acc = acc + _dot(a_ref[...].astype(BF16), w_ref[...].astype(BF16))
    o_ref[...] = acc


def matmul_residual(a_list, w, res, *, tm, tn):
    n_a = len(a_list)
    t, ka = a_list[0].shape
    n = w.shape[1]
    in_specs = [pl.BlockSpec((tm, ka), lambda i, j: (i, 0)) for _ in a_list]
    in_specs += [pl.BlockSpec((ka, tn), lambda i, j, s=s: (s, j)) for s in range(n_a)]
    in_specs.append(pl.BlockSpec((tm, tn), lambda i, j: (i, j)))
    return pl.pallas_call(
        functools.partial(_mm_res_body, n_a=n_a),
        grid=(t // tm, n // tn),
        in_specs=in_specs,
        out_specs=pl.BlockSpec((tm, tn), lambda i, j: (i, j)),
        out_shape=jax.ShapeDtypeStruct((t, n), F32),
        compiler_params=_cparams(("parallel", "parallel")),
        name="matmul_residual",
    )(*a_list, *([w] * n_a), res)


def _swap_halves(w):
    half = w.shape[-1] // 2
    return jnp.concatenate([w[..., half:], w[..., :half]], axis=-1)


def _prep_w_in_even(w):
    d = w.shape[0]
    sizes = (512, 512, 1024, 4, 4, 1024, MLA_Q_RANK, MLA_KV_RANK, MLA_ROPE)
    idx = [0]
    for s in sizes:
        idx.append(idx[-1] + s)
    q, k, v, ig, fg, og, cq, ckv, kr = [w[:, idx[i]:idx[i + 1]] for i in range(len(sizes))]
    z64 = jnp.zeros((d, LANE - MLA_ROPE), w.dtype)
    gates = jnp.concatenate([ig, fg, jnp.zeros((d, LANE - 2 * ML_HEADS), w.dtype)], axis=1)
    pad = jnp.zeros((d, P_WIDTH - P_GATE - LANE), w.dtype)
    return jnp.concatenate([q, k, v, og, cq, ckv, kr, z64, _swap_halves(kr), z64, gates, pad], axis=1).astype(BF16)


def _prep_w_uq(w):
    r = w.shape[0]
    w3 = w.reshape(r, MLA_HEADS, MLA_NOPE + MLA_ROPE)
    nope = w3[..., :MLA_NOPE].reshape(r, MLA_HEADS * MLA_NOPE)
    rope = w3[..., MLA_NOPE:]
    zpad = jnp.zeros((r, MLA_HEADS, LANE - MLA_ROPE), w.dtype)
    ra = jnp.concatenate([rope, zpad], axis=-1).reshape(r, MLA_HEADS * LANE)
    rb = jnp.concatenate([_swap_halves(rope), zpad], axis=-1).reshape(r, MLA_HEADS * LANE)
    return jnp.concatenate([nope, ra, rb], axis=1).astype(BF16)


def _prep_w_in_ssm(w):
    return jnp.pad(w, ((0, 0), (0, Z_WIDTH - w.shape[1]))).astype(BF16)


def _rope_tables(pos):
    half = MLA_ROPE // 2
    inv = ROPE_BASE ** (-jnp.arange(half, dtype=F32) / half)
    ang = pos.astype(F32)[:, None] * inv[None, :]
    c, s = jnp.cos(ang), jnp.sin(ang)
    z = jnp.zeros((pos.shape[0], LANE - MLA_ROPE), F32)
    return jnp.concatenate([c, c, z], axis=1), jnp.concatenate([-s, s, z], axis=1)


def _q_prep_body(cq_ref, g_ref, w_ref, cos_ref, sin_ref, *outs, transposed):
    cn = _rms(cq_ref[...], g_ref[...]).astype(BF16)
    a = _dot(cn, w_ref[...])
    nw = MLA_HEADS * MLA_NOPE
    cos, sin = cos_ref[...], sin_ref[...]
    for h in range(MLA_HEADS):
        qn = a[:, h * LANE:(h + 1) * LANE] * MLA_SCALE
        ra = a[:, nw + h * LANE: nw + (h + 1) * LANE]
        rb = a[:, 2 * nw + h * LANE: 2 * nw + (h + 1) * LANE]
        qr = (ra * cos + rb * sin) * MLA_SCALE
        if transposed:
            (qt_ref,) = outs
            qt_ref[h, 0:MLA_NOPE, :] = qn.T.astype(BF16)
            qt_ref[h, MLA_NOPE:MLA_NOPE + LANE, :] = qr.T.astype(BF16)
        else:
            qn_ref, qr_ref = outs
            qn_ref[:, h * LANE:(h + 1) * LANE] = qn.astype(BF16)
            qr_ref[:, h * LANE:(h + 1) * LANE] = qr.astype(BF16)


def q_prep(p, g, wq, cos, sin, *, tm, transposed=False):
    t = p.shape[0]
    nw = MLA_HEADS * LANE
    if transposed:
        qw = MLA_NOPE + LANE
        out_specs = [pl.BlockSpec((MLA_HEADS, None, qw, tm), lambda i: (0, i, 0, 0))]
        out_shape = [jax.ShapeDtypeStruct((MLA_HEADS, t // tm, qw, tm), BF16)]
    else:
        out_specs = [pl.BlockSpec((tm, nw), lambda i: (i, 0)), pl.BlockSpec((tm, nw), lambda i: (i, 0))]
        out_shape = [jax.ShapeDtypeStruct((t, nw), BF16), jax.ShapeDtypeStruct((t, nw), BF16)]
    return pl.pallas_call(
        functools.partial(_q_prep_body, transposed=transposed),
        grid=(t // tm,),
        in_specs=[
            pl.BlockSpec((tm, MLA_Q_RANK), lambda i: (i, P_CQ // MLA_Q_RANK)),
            pl.BlockSpec((1, MLA_Q_RANK), lambda i: (0, 0)),
            pl.BlockSpec(wq.shape, lambda i: (0, 0)),
            pl.BlockSpec((tm, LANE), lambda i: (i, 0)),
            pl.BlockSpec((tm, LANE), lambda i: (i, 0)),
        ],
        out_specs=out_specs,
        out_shape=out_shape,
        compiler_params=_cparams(("parallel",)),
        name="q_prep",
    )(p, g.reshape(1, -1), wq, cos, sin)


def _kv_prep_body(ckv_ref, kr_ref, krsw_ref, g_ref, cos_ref, sin_ref, *rest, expand):
    cn = _rms(ckv_ref[...], g_ref[...])
    kr = kr_ref[...] * cos_ref[...] + krsw_ref[...] * sin_ref[...]
    if expand:
        w_ref, rows_ref, k_ref, vt_ref = rest
        kvw = MLA_NOPE + MLA_V
        kv = _dot(cn.astype(BF16), w_ref[...])
        krb = kr.astype(BF16)
        for h in range(MLA_HEADS):
            k_ref[:, h * kvw:h * kvw + MLA_NOPE] = kv[:, h * kvw:h * kvw + MLA_NOPE].astype(BF16)
            k_ref[:, h * kvw + MLA_NOPE:(h + 1) * kvw] = krb
            vt_ref[h] = kv[:, h * kvw + MLA_NOPE:(h + 1) * kvw].T.astype(BF16)
    else:
        (rows_ref,) = rest
    rows_ref[:, :MLA_KV_RANK] = cn
    rows_ref[:, MLA_KV_RANK:] = kr[:, :MLA_ROPE]


def kv_prep(p, g, cos, sin, *, tm, w_ukv=None):
    t = p.shape[0]
    expand = w_ukv is not None
    in_specs = [
        pl.BlockSpec((tm, MLA_KV_RANK), lambda i: (i, P_CKV // MLA_KV_RANK)),
        pl.BlockSpec((tm, LANE), lambda i: (i, P_KR // LANE)),
        pl.BlockSpec((tm, LANE), lambda i: (i, P_KRSW // LANE)),
        pl.BlockSpec((1, MLA_KV_RANK), lambda i: (0, 0)),
        pl.BlockSpec((tm, LANE), lambda i: (i, 0)),
        pl.BlockSpec((tm, LANE), lambda i: (i, 0)),
    ]
    args = [p, p, p, g.reshape(1, -1), cos, sin]
    out_specs = [pl.BlockSpec((tm, ROW_W), lambda i: (i, 0))]
    out_shape = [jax.ShapeDtypeStruct((t, ROW_W), F32)]
    if expand:
        nkv = w_ukv.shape[1]
        in_specs.append(pl.BlockSpec(w_ukv.shape, lambda i: (0, 0)))
        args.append(w_ukv)
        out_specs += [pl.BlockSpec((tm, nkv), lambda i: (i, 0)),
                      pl.BlockSpec((MLA_HEADS, None, MLA_V, tm), lambda i: (0, i, 0, 0))]
        out_shape += [jax.ShapeDtypeStruct((t, nkv), BF16),
                      jax.ShapeDtypeStruct((MLA_HEADS, t // tm, MLA_V, tm), BF16)]
    return pl.pallas_call(
        functools.partial(_kv_prep_body, expand=expand),
        grid=(t // tm,),
        in_specs=in_specs,
        out_specs=out_specs,
        out_shape=out_shape,
        compiler_params=_cparams(("parallel",)),
        name="kv_prep",
    )(*args)


def _attn_body(qt_ref, k_ref, vt_ref, o_ref, m_ref, l_ref, acc_ref, *, tq, hp):
    qi = pl.program_id(2)
    m_ref[...] = jnp.full(m_ref.shape, -jnp.inf, F32)
    l_ref[...] = jnp.zeros(l_ref.shape, F32)
    acc_ref[...] = jnp.zeros(acc_ref.shape, F32)
    kw = MLA_NOPE + LANE
    hs = range(hp)

    def block(ki, on_diagonal):
        rows = pl.ds(pl.multiple_of(ki * tq, tq), tq)
        s = [_dot(k_ref[rows, j * kw:(j + 1) * kw], qt_ref[j]) for j in hs]
        if on_diagonal:
            visible = (lax.broadcasted_iota(jnp.int32, (tq, tq), 0) <= lax.broadcasted_iota(jnp.int32, (tq, tq), 1))
            s = [jnp.where(visible, sj, -jnp.inf) for sj in s]
        m_old = [m_ref[j] for j in hs]
        m_new = [jnp.maximum(m_old[j], jnp.max(s[j], axis=0, keepdims=True)) for j in hs]
        p = [jnp.exp(s[j] - m_new[j]) for j in hs]
        pv = [_dot(vt_ref[j, ki], p[j].astype(BF16)) for j in hs]
        corr = [jnp.exp(m_old[j] - m_new[j]) for j in hs]
        l_new = [l_ref[j] * corr[j] + jnp.sum(p[j], axis=0, keepdims=True) for j in hs]
        acc_new = [acc_ref[j] * corr[j] + pv[j] for j in hs]
        for j in hs:
            m_ref[j] = m_new[j]
            l_ref[j] = l_new[j]
            acc_ref[j] = acc_new[j]

    def below_diagonal(ki, carry):
        block(ki, False)
        return carry

    lax.fori_loop(0, qi, below_diagonal, 0)
    block(qi, True)
    for j in hs:
        o_ref[:, j * LANE:(j + 1) * LANE] = (acc_ref[j] / l_ref[j]).T.astype(o_ref.dtype)


def mla_prompt_attention(qt, k, vt, *, bsz, seq, tq, hp):
    nq = seq // tq
    kw = MLA_NOPE + LANE
    return pl.pallas_call(
        functools.partial(_attn_body, tq=tq, hp=hp),
        grid=(bsz, MLA_HEADS // hp, nq),
        in_specs=[
            pl.BlockSpec((hp, None, kw, tq), lambda b, h, qi: (h, b * nq + qi, 0, 0)),
            pl.BlockSpec((seq, hp * kw), lambda b, h, qi: (b, h)),
            pl.BlockSpec((hp, nq, MLA_V, tq), lambda b, h, qi: (h, b, 0, 0)),
        ],
        out_specs=pl.BlockSpec((tq, hp * LANE), lambda b, h, qi: (b * nq + qi, h)),
        out_shape=jax.ShapeDtypeStruct((bsz * seq, MLA_HEADS * MLA_V), BF16),
        scratch_shapes=[pltpu.VMEM((hp, 1, tq), F32), pltpu.VMEM((hp, 1, tq), F32),
                        pltpu.VMEM((hp, MLA_V, tq), F32)],
        compiler_params=_cparams(("parallel", "parallel", "arbitrary")),
        name="mla_prompt_attention",
    )(qt, k, vt)


def _paged_body(pt_ref, qn_ref, qr_ref, rows_ref, w_ref, *rest, gp, nb, n_valid):
    page_refs = rest[:nb * gp]
    o_ref, ql_ref, qrr_ref, m_ref, l_ref, acc_ref = rest[nb * gp:]
    del pt_ref
    g = pl.program_id(1)
    rows_q = MLA_HEADS * SAMPLE_PAD
    kvw = MLA_NOPE + MLA_V

    @pl.when(g == 0)
    def _():
        for sb in range(nb):
            tok = slice(sb * SAMPLE_PAD, (sb + 1) * SAMPLE_PAD)
            for h in range(MLA_HEADS):
                w_uk = w_ref[:, h * kvw: h * kvw + MLA_NOPE]
                ql = _dot_nt(qn_ref[tok, h * LANE:(h + 1) * LANE], w_uk)
                ql_ref[sb, h * SAMPLE_PAD:(h + 1) * SAMPLE_PAD, :] = ql.astype(BF16)
                qrr_ref[sb, h * SAMPLE_PAD:(h + 1) * SAMPLE_PAD, :] = qr_ref[tok, h * LANE:(h + 1) * LANE]
        m_ref[...] = jnp.full(m_ref.shape, -jnp.inf, F32)
        l_ref[...] = jnp.zeros(l_ref.shape, F32)
        acc_ref[...] = jnp.zeros(acc_ref.shape, F32)

    def update(scores, pvs):
        sbs = range(nb)
        m_old = [m_ref[sb] for sb in sbs]
        m_new = [jnp.maximum(m_old[sb], jnp.max(scores[sb], axis=-1, keepdims=True)) for sb in sbs]
        p = [jnp.exp(scores[sb] - m_new[sb]) for sb in sbs]
        pv = [pvs[sb](p[sb].astype(BF16)) for sb in sbs]
        corr = [jnp.exp(m_old[sb] - m_new[sb]) for sb in sbs]
        return [(m_new[sb], l_ref[sb] * corr[sb] + jnp.sum(p[sb], axis=-1, keepdims=True),
                 acc_ref[sb] * corr[sb] + pv[sb]) for sb in sbs]

    def commit(new):
        for sb, (m_new, l_new, acc_new) in enumerate(new):
            m_ref[sb] = m_new
            l_ref[sb] = l_new
            acc_ref[sb] = acc_new

    lat_t = [jnp.concatenate([pr[:MLA_KV_RANK, :].astype(BF16) for pr in page_refs[sb * gp:(sb + 1) * gp]], axis=1)
             for sb in range(nb)]
    rk_t = [jnp.concatenate([pr[MLA_KV_RANK:, :].astype(BF16) for pr in page_refs[sb * gp:(sb + 1) * gp]], axis=1)
            for sb in range(nb)]
    scores = [_dot(ql_ref[sb], lat_t[sb]) + _dot(qrr_ref[sb][:, :MLA_ROPE], rk_t[sb]) for sb in range(nb)]
    commit(update(scores, [lambda p, lt=lt: _dot_nt(p, lt) for lt in lat_t]))

    @pl.when(g == pl.num_programs(1) - 1)
    def _():
        tq = lax.broadcasted_iota(jnp.int32, (rows_q, SAMPLE_PAD), 0) % SAMPLE_PAD
        tk = lax.broadcasted_iota(jnp.int32, (rows_q, SAMPLE_PAD), 1)
        visible = (tk <= tq) & (tk < n_valid)
        scores, pvs = [], []
        for sb in range(nb):
            tok = slice(sb * SAMPLE_PAD, (sb + 1) * SAMPLE_PAD)
            lat = rows_ref[tok, :MLA_KV_RANK].astype(BF16)
            rk = rows_ref[tok, MLA_KV_RANK:].astype(BF16)
            s = _dot_nt(ql_ref[sb], lat) + _dot_nt(qrr_ref[sb][:, :MLA_ROPE], rk)
            scores.append(jnp.where(visible, s, -jnp.inf))
            pvs.append(lambda p, lat=lat: _dot(p, lat))
        for sb, (_, l_new, acc_new) in enumerate(update(scores, pvs)):
            tok = slice(sb * SAMPLE_PAD, (sb + 1) * SAMPLE_PAD)
            out_lat = (acc_new / l_new).astype(BF16)
            for h in range(MLA_HEADS):
                w_uv = w_ref[:, h * kvw + MLA_NOPE:(h + 1) * kvw]
                o = _dot(out_lat[h * SAMPLE_PAD:(h + 1) * SAMPLE_PAD, :], w_uv)
                o_ref[tok, h * LANE:(h + 1) * LANE] = o.astype(o_ref.dtype)


def mla_paged_attention(qn, qr, rows, w_ukv, cache_t, page_table, *, n_valid, gp, nb):
    bsz, n_pages = page_table.shape
    nw = MLA_HEADS * LANE
    npg = n_pages // gp
    rq = MLA_HEADS * SAMPLE_PAD

    def page_spec(sb, i):
        return pl.BlockSpec((None, ROW_W, PAGE_SIZE),
                            lambda b, g, pt: (pt[(b * nb + sb) * n_pages + g * gp + i], 0, 0))

    grid_spec = pltpu.PrefetchScalarGridSpec(
        num_scalar_prefetch=1,
        grid=(bsz // nb, npg),
        in_specs=[
            pl.BlockSpec((nb * SAMPLE_PAD, nw), lambda b, g, pt: (b, 0)),
            pl.BlockSpec((nb * SAMPLE_PAD, nw), lambda b, g, pt: (b, 0)),
            pl.BlockSpec((nb * SAMPLE_PAD, ROW_W), lambda b, g, pt: (b, 0)),
            pl.BlockSpec(w_ukv.shape, lambda b, g, pt: (0, 0)),
        ] + [page_spec(sb, i) for sb in range(nb) for i in range(gp)],
        out_specs=pl.BlockSpec((nb * SAMPLE_PAD, nw), lambda b, g, pt: (b, 0)),
        scratch_shapes=[
            pltpu.VMEM((nb, rq, MLA_KV_RANK), BF16),
            pltpu.VMEM((nb, rq, LANE), BF16),
            pltpu.VMEM((nb, rq, 1), F32),
            pltpu.VMEM((nb, rq, 1), F32),
            pltpu.VMEM((nb, rq, MLA_KV_RANK), F32),
        ],
    )
    return pl.pallas_call(
        functools.partial(_paged_body, gp=gp, nb=nb, n_valid=n_valid),
        grid_spec=grid_spec,
        out_shape=jax.ShapeDtypeStruct((bsz * SAMPLE_PAD, nw), BF16),
        compiler_params=_cparams(("parallel", "arbitrary")),
        name="mla_paged_attention",
    )(page_table.reshape(-1), qn, qr, rows, w_ukv, *([cache_t] * (nb * gp)))


def _mlstm_body(q_ref, k_ref, v_ref, og_ref, gc_ref, gr_ref, bc_ref, br_ref, ng_ref, c0_ref, n0_ref, m0_ref,
                h_ref, c_ref, n_ref, m_ref, *, lc, n_valid, nb, single_chunk):
    if single_chunk:
        cs_ref, ns_ref, ms_ref = c0_ref, n0_ref, m0_ref
        m_ref[...] = m0_ref[...]
    else:
        cs_ref, ns_ref, ms_ref = c_ref, n_ref, m_ref

        @pl.when(pl.program_id(1) == 0)
        def _():
            c_ref[...] = c0_ref[...]
            n_ref[...] = n0_ref[...]
            m_ref[...] = m0_ref[...]

    nh = ML_HEADS
    row = lax.broadcasted_iota(jnp.int32, (lc, lc), 0)
    col = lax.broadcasted_iota(jnp.int32, (lc, lc), 1)
    causal = col <= row
    tril = causal.astype(F32)
    triu = (row <= col).astype(F32)
    seqs = []
    for sb in range(nb):
        tok = slice(sb * lc, (sb + 1) * lc)
        gc = gc_ref[tok, :] + bc_ref[...]
        gr = gr_ref[sb] + br_ref[:, 0:1]
        li_c, lf_c = gc[:, 0:nh], _log_sigmoid(gc[:, nh:2 * nh])
        li_r, lf_r = gr[0:nh, :], _log_sigmoid(gr[nh:2 * nh, :])
        if n_valid < lc:
            vc = lax.broadcasted_iota(jnp.int32, (lc, nh), 0) < n_valid
            vr = lax.broadcasted_iota(jnp.int32, (nh, lc), 1) < n_valid
            li_c, lf_c = jnp.where(vc, li_c, NEG_BIG), jnp.where(vc, lf_c, 0.0)
            li_r, lf_r = jnp.where(vr, li_r, NEG_BIG), jnp.where(vr, lf_r, 0.0)
        seqs.append((li_c, li_r, lf_c, lf_r))
    b_cs = [_dot_sel_lhs(tril, s[2]) for s in seqs]
    b_rs = [_dot_sel_rhs(s[3], triu) for s in seqs]

    items = [(sb, h) for sb in range(nb) for h in range(nh)]
    st = []
    for sb, h in items:
        tok = slice(sb * lc, (sb + 1) * lc)
        qh = q_ref[tok, h * ML_DQK:(h + 1) * ML_DQK] * (ML_DQK ** -0.5)
        kh = k_ref[tok, h * ML_DQK:(h + 1) * ML_DQK]
        st.append(dict(
            bc=b_cs[sb][:, h:h + 1], br=b_rs[sb][h:h + 1, :],
            ic=seqs[sb][0][:, h:h + 1], ir=seqs[sb][1][h:h + 1, :],
            m_prev=ms_ref[sb, h:h + 1, 0:1], qh=qh, qb=qh.astype(BF16), kh=kh, kb=kh.astype(BF16),
            vh=v_ref[tok, h * ML_DV:(h + 1) * ML_DV], og=og_ref[tok, h * ML_DV:(h + 1) * ML_DV],
            c=cs_ref[sb, h], n=ns_ref[sb, h:h + 1, :], ng=ng_ref[:, h * ML_DV:(h + 1) * ML_DV]))
    for d in st:
        d["qk"] = _dot_nt(d["qb"], d["kb"])
        d["qc"] = _dot_nt(d["qb"], d["c"].astype(BF16))
    for d in st:
        dmat = jnp.where(causal, d["bc"] - d["br"] + d["ir"], -jnp.inf)
        inter = d["bc"] + d["m_prev"]
        d["m_t"] = jnp.maximum(inter, jnp.max(dmat, axis=-1, keepdims=True))
        d["w_inter"] = jnp.exp(inter - d["m_t"])
        d["s"] = d["qk"] * jnp.exp(dmat - d["m_t"])
        b_last = d["bc"][lc - 1:lc, :]
        g_c = b_last - d["bc"] + d["ic"]
        g_r = b_last - d["br"] + d["ir"]
        d["m_new"] = jnp.maximum(b_last + d["m_prev"], jnp.max(g_r, axis=-1, keepdims=True))
        d["decay"] = jnp.exp(b_last + d["m_prev"] - d["m_new"])
        d["wg"] = jnp.exp(g_c - d["m_new"])
    for d in st:
        d["sv"] = _dot(d["s"].astype(BF16), d["vh"].astype(BF16))
        d["vk"] = _dot_tn((d["vh"] * d["wg"]).astype(BF16), d["kb"])
    for d in st:
        num = d["sv"] + d["w_inter"] * d["qc"]
        den = (jnp.sum(d["s"], axis=-1, keepdims=True)
               + d["w_inter"] * jnp.sum(d["qh"] * d["n"], axis=-1, keepdims=True))
        hout = num / jnp.maximum(jnp.abs(den), jnp.exp(-d["m_t"]))
        d["h_new"] = (_sigmoid(d["og"]) * _rms(hout, d["ng"])).astype(h_ref.dtype)
        d["c_new"] = d["decay"] * d["c"] + d["vk"]
        d["n_new"] = d["decay"] * d["n"] + jnp.sum(d["kh"] * d["wg"], axis=0, keepdims=True)

    for (sb, h), d in zip(items, st):
        h_ref[sb * lc:(sb + 1) * lc, h * ML_DV:(h + 1) * ML_DV] = d["h_new"]
        c_ref[sb, h] = d["c_new"]
        n_ref[sb, h:h + 1, :] = d["n_new"]
        m_ref[sb, h:h + 1, :] = jnp.broadcast_to(d["m_new"], (1, LANE))


def mlstm(p, gates_rows, b_ig, b_fg, norm_g, c0, n0, m0, *, bsz, lc, nchunk, n_valid, nb):
    assert nb == 1 or nchunk == 1
    nh = ML_HEADS
    bias_c = jnp.concatenate([b_ig, b_fg, jnp.zeros((LANE - 2 * nh,), F32)]).reshape(1, LANE)
    bias_r = jnp.broadcast_to(jnp.concatenate([b_ig, b_fg]).reshape(2 * nh, 1), (2 * nh, LANE))
    tl = nb * lc

    def rows(cb):
        return lambda b, c: (b * nchunk + c, cb)

    const2 = lambda b, c: (0, 0)
    return pl.pallas_call(
        functools.partial(_mlstm_body, lc=lc, n_valid=n_valid, nb=nb, single_chunk=nchunk == 1),
        grid=(bsz // nb, nchunk),
        in_specs=[
            pl.BlockSpec((tl, nh * ML_DQK), rows(P_Q // (nh * ML_DQK))),
            pl.BlockSpec((tl, nh * ML_DQK), rows(P_K // (nh * ML_DQK))),
            pl.BlockSpec((tl, nh * ML_DV), rows(P_V // (nh * ML_DV))),
            pl.BlockSpec((tl, nh * ML_DV), rows(P_OG // (nh * ML_DV))),
            pl.BlockSpec((tl, LANE), rows(P_GATE // LANE)),
            pl.BlockSpec((nb, 2 * nh, lc), lambda b, c: (b * nchunk + c, 0, 0)),
            pl.BlockSpec((1, LANE), const2),
            pl.BlockSpec((2 * nh, LANE), const2),
            pl.BlockSpec((1, nh * ML_DV), const2),
            pl.BlockSpec((nb, nh, ML_DV, ML_DQK), lambda b, c: (b, 0, 0, 0)),
            pl.BlockSpec((nb, nh, ML_DQK), lambda b, c: (b, 0, 0)),
            pl.BlockSpec((nb, 2 * nh, LANE), lambda b, c: (b, 0, 0)),
        ],
        out_specs=[
            pl.BlockSpec((tl, nh * ML_DV), lambda b, c: (b * nchunk + c, 0)),
            pl.BlockSpec((nb, nh, ML_DV, ML_DQK), lambda b, c: (b, 0, 0, 0)),
            pl.BlockSpec((nb, nh, ML_DQK), lambda b, c: (b, 0, 0)),
            pl.BlockSpec((nb, 2 * nh, LANE), lambda b, c: (b, 0, 0)),
        ],
        out_shape=[
            jax.ShapeDtypeStruct((bsz * nchunk * lc, nh * ML_DV), BF16),
            jax.ShapeDtypeStruct((bsz, nh, ML_DV, ML_DQK), F32),
            jax.ShapeDtypeStruct((bsz, nh, ML_DQK), F32),
            jax.ShapeDtypeStruct((bsz, 2 * nh, LANE), F32),
        ],
        compiler_params=_cparams(("parallel", "arbitrary")),
        name="mlstm",
    )(p, p, p, p, p, gates_rows, bias_c, bias_r, norm_g.reshape(1, -1), c0, n0, m0)


def _gates_rows(p, lc):
    nrows = p.shape[0]
    g = p[:, P_GATE:P_GATE + 2 * ML_HEADS]
    return g.reshape(nrows // lc, lc, 2 * ML_HEADS).transpose(0, 2, 1)


def _m_pack(m):
    b = m.shape[0]
    mp = jnp.pad(m.astype(F32), ((0, 0), (0, 2 * ML_HEADS - m.shape[1])))
    return jnp.broadcast_to(mp[:, :, None], (b, 2 * ML_HEADS, LANE))


def _ssd_body(z_ref, x_ref, b_ref, c_ref, dtc_ref, dtr_ref, cwx_ref, cwb_ref, cwc_ref, cbx_ref, cbb_ref, cbc_ref,
              csx_ref, csb_ref, csc_ref, pc_ref, pr_ref, dsk_ref, ng_ref, h0_ref,
              y_ref, h_ref, bufx, bufb, bufc, *, lc, n_valid, gs, single_chunk):
    tail = SAMPLE_PAD
    cxs = [slice(gi * SSM_GW, (gi + 1) * SSM_GW) for gi in range(gs)]
    cns = [slice(gi * SSM_STATE, (gi + 1) * SSM_STATE) for gi in range(gs)]

    def init_history():
        for gi in range(gs):
            bufx[gi, 0:tail, :] = csx_ref[:, cxs[gi]]
            bufb[gi, 0:tail, :] = csb_ref[:, cns[gi]]
            bufc[gi, 0:tail, :] = csc_ref[:, cns[gi]]

    if single_chunk:
        init_history()
        hs_ref = h0_ref
    else:
        hs_ref = h_ref

        @pl.when(pl.program_id(2) == 0)
        def _():
            h_ref[...] = h0_ref[...]
            init_history()

    def conv(raw, buf, cw_ref, cb_ref, cols):
        buf[tail:tail + lc, :] = raw
        acc = cb_ref[:, cols] + cw_ref[SSM_CONV - 1:SSM_CONV, cols] * raw
        for j in range(SSM_CONV - 1):
            off = tail - (SSM_CONV - 1) + j
            acc = acc + cw_ref[j:j + 1, cols] * buf[off:off + lc, :]
        if not single_chunk:
            buf[0:tail, :] = raw[lc - tail:lc, :]
        return _silu(acc)

    convs = []
    for gi in range(gs):
        convs.append((conv(x_ref[:, cxs[gi]], bufx.at[gi], cwx_ref, cbx_ref, cxs[gi]),
                      conv(b_ref[:, cns[gi]], bufb.at[gi], cwb_ref, cbb_ref, cns[gi]),
                      conv(c_ref[:, cns[gi]], bufc.at[gi], cwc_ref, cbc_ref, cns[gi])))

    outs = _ssd_groups(
        convs,
        [dict(z=z_ref[:, cxs[gi]], dtc=dtc_ref[:, gi * SSM_HG:(gi + 1) * SSM_HG],
              dtr=dtr_ref[gi * SSM_HG:(gi + 1) * SSM_HG, :], pc=pc_ref[gi], pr=pr_ref[gi],
              dsk=dsk_ref[:, cxs[gi]], ng=ng_ref[:, cxs[gi]], h=hs_ref[gi]) for gi in range(gs)],
        lc=lc, n_valid=n_valid)

    for gi, (y, h_new) in enumerate(outs):
        y_ref[:, cxs[gi]] = y.astype(y_ref.dtype)
        h_ref[gi] = h_new


def _ssd_groups(convs, ins, *, lc, n_valid):
    hg = SSM_HG
    gs = range(len(ins))
    row = lax.broadcasted_iota(jnp.int32, (lc, lc), 0)
    col = lax.broadcasted_iota(jnp.int32, (lc, lc), 1)
    causal = col <= row
    tril = causal.astype(F32)
    triu = (row <= col).astype(F32)
    expand = (lax.broadcasted_iota(jnp.int32, (hg, SSM_GW), 1) // SSM_HEADDIM
              == lax.broadcasted_iota(jnp.int32, (hg, SSM_GW), 0)).astype(F32)
    lane_head = lax.broadcasted_iota(jnp.int32, (lc, LANE), 1) // SSM_HEADDIM

    dt_c, dt_r = [], []
    for g in gs:
        c = _softplus(ins[g]["dtc"] + ins[g]["pc"][0:1, 0:hg])
        r = _softplus(ins[g]["dtr"] + ins[g]["pr"][:, 0:1])
        if n_valid < lc:
            c = jnp.where(lax.broadcasted_iota(jnp.int32, (lc, hg), 0) < n_valid, c, 0.0)
            r = jnp.where(lax.broadcasted_iota(jnp.int32, (hg, lc), 1) < n_valid, r, 0.0)
        dt_c.append(c)
        dt_r.append(r)
    cum_c = [_dot_sel_lhs(tril, dt_c[g] * (-jnp.exp(ins[g]["pc"][1:2, 0:hg]))) for g in gs]
    cum_r = [_dot_sel_rhs(dt_r[g] * (-jnp.exp(ins[g]["pr"][:, 1:2])), triu) for g in gs]
    cb16 = [convs[g][2].astype(BF16) for g in gs]
    bb16 = [convs[g][1].astype(BF16) for g in gs]
    cbm = [_dot_nt(cb16[g], bb16[g]) for g in gs]
    yh = [_dot_nt(cb16[g], ins[g]["h"].astype(BF16)) for g in gs]
    e_in = [_dot_sel_rhs(jnp.exp(cum_c[g]), expand) for g in gs]
    e_st = [_dot_sel_rhs(jnp.exp(cum_c[g][lc - 1:lc, :] - cum_c[g]) * dt_c[g], expand) for g in gs]
    upd = [_dot_tn((convs[g][0] * e_st[g]).astype(BF16), bb16[g]) for g in gs]

    pieces = [[] for _ in gs]
    for pair in range(hg // 2):
        parts = [[] for _ in gs]
        for sub in range(2):
            j = 2 * pair + sub
            for g in gs:
                seg = jnp.where(causal, cum_c[g][:, j:j + 1] - cum_r[g][j:j + 1, :], -jnp.inf)
                w = cbm[g] * jnp.exp(seg) * dt_r[g][j:j + 1, :]
                xm = jnp.where(lane_head == sub, convs[g][0][:, pair * LANE:(pair + 1) * LANE], 0.0)
                parts[g].append(_dot(w.astype(BF16), xm.astype(BF16)))
        for g in gs:
            pieces[g].append(parts[g][0] + parts[g][1])

    outs = []
    for g in gs:
        xa, h_old = convs[g][0], ins[g]["h"]
        y = yh[g] * e_in[g] + jnp.concatenate(pieces[g], axis=1)
        dec_r = jnp.exp(cum_r[g][:, lc - 1:lc])
        h_new = jnp.concatenate(
            [dec_r[j:j + 1, :] * h_old[j * SSM_HEADDIM:(j + 1) * SSM_HEADDIM, :]
             + upd[g][j * SSM_HEADDIM:(j + 1) * SSM_HEADDIM, :] for j in range(hg)], axis=0)
        y = (y + ins[g]["dsk"] * xa) * _silu(ins[g]["z"])
        outs.append((_rms(y, ins[g]["ng"]), h_new))
    return outs


def ssd(zp, dt_rows, conv_w, conv_b, conv_state8, dt_bias, a_log, d_skip, norm_g, h0, *, bsz, lc, nchunk,
        n_valid, gs):
    assert gs == SSM_GROUPS
    g_, hg = SSM_GROUPS // gs, SSM_HG * gs
    gw, ns = SSM_GW * gs, SSM_STATE * gs

    def rows(off, width):
        assert off % width == 0
        return lambda b, g, c: (b * nchunk + c, off // width + g)

    ng_all, nh1 = SSM_GROUPS, SSM_HG
    pcol = jnp.zeros((ng_all, 8, LANE), F32)
    pcol = pcol.at[:, 0, :nh1].set(dt_bias.reshape(ng_all, nh1)).at[:, 1, :nh1].set(a_log.reshape(ng_all, nh1))
    prow = jnp.zeros((ng_all, nh1, LANE), F32)
    prow = prow.at[:, :, 0].set(dt_bias.reshape(ng_all, nh1)).at[:, :, 1].set(a_log.reshape(ng_all, nh1))
    dsk = jnp.repeat(d_skip, SSM_HEADDIM).reshape(1, SSM_D_INNER)
    cb2 = conv_b.reshape(1, -1)
    bo, co = SSM_D_INNER // ns, SSM_D_INNER // ns + g_
    assert SSM_D_INNER % ns == 0

    in_specs = [
        pl.BlockSpec((lc, gw), rows(0, gw)),
        pl.BlockSpec((lc, gw), rows(Z_X, gw)),
        pl.BlockSpec((lc, ns), rows(Z_B, ns)),
        pl.BlockSpec((lc, ns), rows(Z_C, ns)),
        pl.BlockSpec((lc, LANE), rows(Z_DT, LANE)),
        pl.BlockSpec((None, hg, lc), lambda b, g, c: (b * nchunk + c, g, 0)),
        pl.BlockSpec((SSM_CONV, gw), lambda b, g, c: (0, g)),
        pl.BlockSpec((SSM_CONV, ns), lambda b, g, c: (0, bo + g)),
        pl.BlockSpec((SSM_CONV, ns), lambda b, g, c: (0, co + g)),
        pl.BlockSpec((1, gw), lambda b, g, c: (0, g)),
        pl.BlockSpec((1, ns), lambda b, g, c: (0, bo + g)),
        pl.BlockSpec((1, ns), lambda b, g, c: (0, co + g)),
        pl.BlockSpec((None, 8, gw), lambda b, g, c: (b, 0, g)),
        pl.BlockSpec((None, 8, ns), lambda b, g, c: (b, 0, bo + g)),
        pl.BlockSpec((None, 8, ns), lambda b, g, c: (b, 0, co + g)),
        pl.BlockSpec((gs, 8, LANE), lambda b, g, c: (g, 0, 0)),
        pl.BlockSpec((gs, nh1, LANE), lambda b, g, c: (g, 0, 0)),
        pl.BlockSpec((1, gw), lambda b, g, c: (0, g)),
        pl.BlockSpec((1, gw), lambda b, g, c: (0, g)),
        pl.BlockSpec((None, gs, SSM_GW, SSM_STATE), lambda b, g, c: (b, g, 0, 0)),
    ]
    return pl.pallas_call(
        functools.partial(_ssd_body, lc=lc, n_valid=n_valid, gs=gs, single_chunk=nchunk == 1),
        grid=(bsz, g_, nchunk),
        in_specs=in_specs,
        out_specs=[
            pl.BlockSpec((lc, gw), lambda b, g, c: (b * nchunk + c, g)),
            pl.BlockSpec((None, gs, SSM_GW, SSM_STATE), lambda b, g, c: (b, g, 0, 0)),
        ],
        out_shape=[
            jax.ShapeDtypeStruct((bsz * nchunk * lc, SSM_D_INNER), BF16),
            jax.ShapeDtypeStruct((bsz, SSM_GROUPS, SSM_GW, SSM_STATE), F32),
        ],
        scratch_shapes=[
            pltpu.VMEM((gs, lc + SAMPLE_PAD, SSM_GW), F32),
            pltpu.VMEM((gs, lc + SAMPLE_PAD, SSM_STATE), F32),
            pltpu.VMEM((gs, lc + SAMPLE_PAD, SSM_STATE), F32),
        ],
        compiler_params=_cparams(("parallel", "parallel", "arbitrary")),
        name="ssd",
    )(zp, zp, zp, zp, zp, dt_rows, conv_w, conv_w, conv_w, cb2, cb2, cb2, conv_state8, conv_state8, conv_state8,
      pcol, prow, dsk, norm_g.reshape(1, -1), h0)


def _dt_rows(zp, lc):
    nrows = zp.shape[0]
    return zp[:, Z_DT:Z_DT + SSM_HEADS].reshape(nrows // lc, lc, SSM_HEADS).transpose(0, 2, 1)


def _conv_state8(state):
    return jnp.pad(state, ((0, 0), (SAMPLE_PAD - (SSM_CONV - 1), 0), (0, 0)))


def _trunk(x_prompt, x_sample, cache_mla, state_mlstm_C, state_mlstm_n, state_mlstm_m, state_ssm, state_conv,
           page_table, norm_g, ffn_w_gate, ffn_w_up, ffn_w_down, w_in_even, b_ig, b_fg, mlstm_norm_g, mla_q_norm_g,
           mla_kv_norm_g, w_uq, w_ukv, w_out_even, w_in_ssm, conv_w, conv_b, dt_bias, A_log, D_skip, ssm_norm_g,
           w_out_ssm, final_norm_g, *, tm, tf, tn, tn_ssm, tmp, lc_ml, nb_ml, lc_ssd, gs_p, gs_s, tq, hp, gp, nb_pg):
    bp, sp, d = x_prompt.shape
    bs, ts, _ = x_sample.shape
    tp = bp * sp
    tsp = bs * SAMPLE_PAD
    xs = [x_prompt.reshape(tp, d), x_sample.reshape(bs * ts, d)]
    tms = [min(tm, tp), min(tm, bs * ts)]
    tmps = [min(tmp, tp), min(tmp, tsp)]
    assert tmps[0] == tq
    tables = [_rope_tables(jnp.tile(jnp.arange(sp), bp)),
              _rope_tables(jnp.tile(PAST_LEN + jnp.arange(SAMPLE_PAD), bs))]

    def pad_seq(a):
        return jnp.pad(a.reshape(bs, ts, a.shape[1]), ((0, 0), (0, SAMPLE_PAD - ts), (0, 0))).reshape(tsp, a.shape[1])

    def unpad_seq(a):
        return a.reshape(bs, SAMPLE_PAD, a.shape[1])[:, :ts].reshape(bs * ts, a.shape[1])

    def ff(xs, g, layer, which, final_g=None):
        return [ffn(x, g, ffn_w_gate, ffn_w_up, ffn_w_down, layer, which, tm=t, tf=tf, final_g=final_g)
                for x, t in zip(xs, tms)]

    xs = ff(xs, norm_g[0, 0], 0, 0)
    w_in = _prep_w_in_even(w_in_even)
    p_p, p_s = [norm_matmul(x, norm_g[0, 1], w_in, tm=t, tn=tn) for x, t in zip(xs, tms)]
    p_s = pad_seq(p_s)

    zeros = functools.partial(jnp.zeros, dtype=F32)
    hm_p, c_p, n_p, m_p = mlstm(
        p_p, _gates_rows(p_p, lc_ml), b_ig, b_fg, mlstm_norm_g,
        zeros((bp, ML_HEADS, ML_DV, ML_DQK)), zeros((bp, ML_HEADS, ML_DQK)), zeros((bp, 2 * ML_HEADS, LANE)),
        bsz=bp, lc=lc_ml, nchunk=sp // lc_ml, n_valid=lc_ml, nb=1)
    hm_s, c_s, n_s, m_s = mlstm(
        p_s, _gates_rows(p_s, SAMPLE_PAD), b_ig, b_fg, mlstm_norm_g,
        state_mlstm_C, state_mlstm_n, _m_pack(state_mlstm_m),
        bsz=bs, lc=SAMPLE_PAD, nchunk=1, n_valid=ts, nb=nb_ml)

    w_ukv16 = w_ukv.astype(BF16)
    w_q = _prep_w_uq(w_uq)
    (qt_p,) = q_prep(p_p, mla_q_norm_g, w_q, *tables[0], tm=tmps[0], transposed=True)
    qn_s, qr_s = q_prep(p_s, mla_q_norm_g, w_q, *tables[1], tm=tmps[1])
    rows_p, k_p, vt_p = kv_prep(p_p, mla_kv_norm_g, *tables[0], tm=tmps[0], w_ukv=w_ukv16)
    (rows_s,) = kv_prep(p_s, mla_kv_norm_g, *tables[1], tm=tmps[1])
    ha_p = mla_prompt_attention(qt_p, k_p, vt_p, bsz=bp, seq=sp, tq=tq, hp=hp)
    cache_t = jnp.swapaxes(cache_mla, 1, 2)
    ha_s = mla_paged_attention(qn_s, qr_s, rows_s, w_ukv16, cache_t, page_table, n_valid=ts, gp=gp, nb=nb_pg)

    w_oe = w_out_even.astype(BF16)
    xs = [matmul_residual([hm, ha], w_oe, x, tm=t, tn=tn)
          for hm, ha, x, t in zip((hm_p, unpad_seq(hm_s)), (ha_p, unpad_seq(ha_s)), xs, tms)]
    xs = ff(xs, norm_g[0, 2], 0, 1)

    xs = ff(xs, norm_g[1, 0], 1, 0)
    w_is = _prep_w_in_ssm(w_in_ssm)
    z_p, z_s4 = [norm_matmul(x, norm_g[1, 1], w_is, tm=t, tn=tn_ssm) for x, t in zip(xs, tms)]
    z_s = pad_seq(z_s4)
    y_p, ssm_p = ssd(
        z_p, _dt_rows(z_p, lc_ssd), conv_w, conv_b, zeros((bp, SAMPLE_PAD, SSM_CONV_DIM)), dt_bias, A_log,
        D_skip, ssm_norm_g, zeros((bp, SSM_GROUPS, SSM_GW, SSM_STATE)),
        bsz=bp, lc=lc_ssd, nchunk=sp // lc_ssd, n_valid=lc_ssd, gs=gs_p)
    y_s, ssm_s = ssd(
        z_s, _dt_rows(z_s, SAMPLE_PAD), conv_w, conv_b, _conv_state8(state_conv), dt_bias, A_log,
        D_skip, ssm_norm_g, state_ssm.reshape(bs, SSM_GROUPS, SSM_GW, SSM_STATE),
        bsz=bs, lc=SAMPLE_PAD, nchunk=1, n_valid=ts, gs=gs_s)
    w_os = w_out_ssm.astype(BF16)
    xs = [matmul_residual([y], w_os, x, tm=t, tn=tn) for y, x, t in zip((y_p, unpad_seq(y_s)), xs, tms)]
    out_p, out_s = ff(xs, norm_g[1, 2], 1, 1, final_g=final_norm_g)

    keep = SSM_CONV - 1
    assert sp >= keep and ts >= keep
    conv_p = jnp.stack([lax.slice(z_p, (b * sp + sp - keep, Z_X), (b * sp + sp, Z_DT)) for b in range(bp)])
    conv_s = z_s4[:, Z_X:Z_DT].reshape(bs, ts, SSM_CONV_DIM)[:, ts - keep:ts]
    hshape = (SSM_HEADS, SSM_HEADDIM, SSM_STATE)
    return (out_p.reshape(bp, sp, d), out_s.reshape(bs, ts, d), rows_p.reshape(bp, sp, ROW_W),
            rows_s.reshape(bs, SAMPLE_PAD, ROW_W)[:, :ts],
            c_p, c_s, n_p, n_s, m_p[:, :ML_HEADS, 0], m_s[:, :ML_HEADS, 0],
            ssm_p.reshape((bp,) + hshape), ssm_s.reshape((bs,) + hshape), conv_p, conv_s)


def kernel(x_prompt, x_sample, cache_mla, state_mlstm_C, state_mlstm_n, state_mlstm_m, state_ssm, state_conv, page_table, norm_g, ffn_w_gate, ffn_w_up, ffn_w_down, w_in_even, b_ig, b_fg, mlstm_norm_g, mla_q_norm_g, mla_kv_norm_g, w_uq, w_ukv, w_out_even, w_in_ssm, conv_w, conv_b, dt_bias, A_log, D_skip, ssm_norm_g, w_out_ssm, final_norm_g):
    return _trunk(x_prompt, x_sample, cache_mla, state_mlstm_C, state_mlstm_n, state_mlstm_m, state_ssm, state_conv,
                  page_table, norm_g, ffn_w_gate, ffn_w_up, ffn_w_down, w_in_even, b_ig, b_fg, mlstm_norm_g,
                  mla_q_norm_g, mla_kv_norm_g, w_uq, w_ukv, w_out_even, w_in_ssm, conv_w, conv_b, dt_bias, A_log,
                  D_skip, ssm_norm_g, w_out_ssm, final_norm_g,
                  tm=1024, tf=256, tn=512, tn_ssm=1152, tmp=512, lc_ml=256, nb_ml=4, lc_ssd=128, gs_p=8, gs_s=8,
                  tq=512, hp=2, gp=32, nb_pg=2)
```

```python
import functools
import math

import jax
import jax.numpy as jnp
from jax import lax
from jax.experimental import pallas as pl
from jax.experimental.pallas import tpu as pltpu

F32 = jnp.float32
BF16 = jnp.bfloat16

D_MODEL = 2048
D_FF = 5632
EPS = 1e-6
PAST_LEN = 8192
PAGE_SIZE = 128

ML_HEADS = 4
ML_DQK = 128
ML_DV = 256

MLA_HEADS = 8
MLA_NOPE = 128
MLA_ROPE = 64
MLA_V = 128
MLA_Q_RANK = 512
MLA_KV_RANK = 512
MLA_SCALE = (MLA_NOPE + MLA_ROPE) ** -0.5
ROPE_BASE = 10000.0
ROW_W = MLA_KV_RANK + MLA_ROPE

SSM_D_INNER = 4096
SSM_HEADDIM = 64
SSM_HEADS = 64
SSM_GROUPS = 8
SSM_HG = SSM_HEADS // SSM_GROUPS
SSM_GW = SSM_D_INNER // SSM_GROUPS
SSM_STATE = 128
SSM_CONV = 4
SSM_CONV_DIM = SSM_D_INNER + 2 * SSM_GROUPS * SSM_STATE

SAMPLE_PAD = 8
LANE = 128
VMEM_LIMIT = 56 * 1024 * 1024

P_Q, P_K, P_V, P_OG, P_CQ, P_CKV = 0, 512, 1024, 2048, 3072, 3584
P_KR, P_KRSW, P_GATE, P_WIDTH = 4096, 4224, 4352, 4608
Z_X, Z_B, Z_C, Z_DT, Z_WIDTH = 4096, 8192, 9216, 10240, 10368

NEG_BIG = -1e30


def _cparams(sem):
    return pltpu.CompilerParams(dimension_semantics=sem, vmem_limit_bytes=VMEM_LIMIT)


def _rms(x, g):
    return x * lax.rsqrt(jnp.mean(x * x, axis=-1, keepdims=True) + EPS) * g


def _sigmoid(x):
    return 1.0 / (1.0 + jnp.exp(-x))


def _silu(x):
    return x * _sigmoid(x)


def _log_sigmoid(x):
    return jnp.minimum(x, 0.0) - jnp.log1p(jnp.exp(-jnp.abs(x)))


def _softplus(x):
    return jnp.maximum(x, 0.0) + jnp.log1p(jnp.exp(-jnp.abs(x)))


def _dot(a, b):
    return jnp.dot(a, b, preferred_element_type=F32)


def _dot_nt(a, b):
    return lax.dot_general(a, b, (((1,), (1,)), ((), ())), preferred_element_type=F32)


def _dot_tn(a, b):
    return lax.dot_general(a, b, (((0,), (0,)), ((), ())), preferred_element_type=F32)


def _split3(x):
    x1 = x.astype(BF16)
    r1 = x - x1.astype(F32)
    x2 = r1.astype(BF16)
    x3 = (r1 - x2.astype(F32)).astype(BF16)
    return x1, x2, x3


def _dot_sel_lhs(sel, x):
    s = sel.astype(BF16)
    x1, x2, x3 = _split3(x)
    return (_dot(s, x3) + _dot(s, x2)) + _dot(s, x1)


def _dot_sel_rhs(x, sel):
    s = sel.astype(BF16)
    x1, x2, x3 = _split3(x)
    return (_dot(x3, s) + _dot(x2, s)) + _dot(x1, s)


def _ffn_body(x_ref, g_ref, wg_ref, wu_ref, wd_ref, *rest, final_norm):
    if final_norm:
        gf_ref, o_ref, xn_ref = rest
    else:
        o_ref, xn_ref = rest
    j = pl.program_id(1)

    @pl.when(j == 0)
    def _():
        x = x_ref[...]
        xn_ref[...] = _rms(x, g_ref[...]).astype(BF16)
        o_ref[...] = x

    xn = xn_ref[...]
    hg = _dot(xn, wg_ref[...].astype(BF16))
    hu = _dot(xn, wu_ref[...].astype(BF16))
    h = _silu(hg) * (0.5 * hu)
    o_ref[...] += _dot(h.astype(BF16), wd_ref[...].astype(BF16))

    if final_norm:
        @pl.when(j == pl.num_programs(1) - 1)
        def _():
            o_ref[...] = _rms(o_ref[...], gf_ref[...])


def ffn(x, g, w_gate, w_up, w_down, layer, which, *, tm, tf, final_g=None):
    t, d = x.shape
    dff = w_gate.shape[-1]
    final_norm = final_g is not None
    in_specs = [
        pl.BlockSpec((tm, d), lambda i, j: (i, 0), pipeline_mode=pl.Buffered(1)),
        pl.BlockSpec((1, d), lambda i, j: (0, 0)),
        pl.BlockSpec((None, None, d, tf), lambda i, j: (layer, which, 0, j)),
        pl.BlockSpec((None, None, d, tf), lambda i, j: (layer, which, 0, j)),
        pl.BlockSpec((None, None, tf, d), lambda i, j: (layer, which, j, 0)),
    ]
    args = [x, g.reshape(1, d), w_gate, w_up, w_down]
    if final_norm:
        in_specs.append(pl.BlockSpec((1, d), lambda i, j: (0, 0)))
        args.append(final_g.reshape(1, d))
    return pl.pallas_call(
        functools.partial(_ffn_body, final_norm=final_norm),
        grid=(t // tm, dff // tf),
        in_specs=in_specs,
        out_specs=pl.BlockSpec((tm, d), lambda i, j: (i, 0)),
        out_shape=jax.ShapeDtypeStruct((t, d), F32),
        scratch_shapes=[pltpu.VMEM((tm, d), BF16)],
        compiler_params=_cparams(("parallel", "arbitrary")),
        name="ffn",
    )(*args)


def _norm_mm_body(x_ref, g_ref, w_ref, o_ref, xn_ref):
    @pl.when(pl.program_id(1) == 0)
    def _():
        xn_ref[...] = _rms(x_ref[...], g_ref[...]).astype(BF16)

    o_ref[...] = _dot(xn_ref[...], w_ref[...].astype(BF16))


def norm_matmul(x, g, w, *, tm, tn):
    t, d = x.shape
    n = w.shape[1]
    return pl.pallas_call(
        _norm_mm_body,
        grid=(t // tm, n // tn),
        in_specs=[
            pl.BlockSpec((tm, d), lambda i, j: (i, 0)),
            pl.BlockSpec((1, d), lambda i, j: (0, 0)),
            pl.BlockSpec((d, tn), lambda i, j: (0, j)),
        ],
        out_specs=pl.BlockSpec((tm, tn), lambda i, j: (i, j)),
        out_shape=jax.ShapeDtypeStruct((t, n), F32),
        scratch_shapes=[pltpu.VMEM((tm, d), BF16)],
        compiler_params=_cparams(("parallel", "arbitrary")),
        name="norm_matmul",
    )(x, g.reshape(1, d), w)


def _mm_res_body(*refs, n_a):
    a_refs, w_refs = refs[:n_a], refs[n_a:2 * n_a]
    res_ref, o_ref = refs[2 * n_a], refs[2 * n_a + 1]
    acc = res_ref[...]
    for a_ref, w_ref in zip(a_refs, w_refs):
        acc = acc + _dot(a_ref[...].astype(BF16), w_ref[...].astype(BF16))
    o_ref[...] = acc


def matmul_residual(a_list, w, res, *, tm, tn):
    n_a = len(a_list)
    t, ka = a_list[0].shape
    n = w.shape[1]
    in_specs = [pl.BlockSpec((tm, ka), lambda i, j: (i, 0)) for _ in a_list]
    in_specs += [pl.BlockSpec((ka, tn), lambda i, j, s=s: (s, j)) for s in range(n_a)]
    in_specs.append(pl.BlockSpec((tm, tn), lambda i, j: (i, j)))
    return pl.pallas_call(
        functools.partial(_mm_res_body, n_a=n_a),
        grid=(t // tm, n // tn),
        in_specs=in_specs,
        out_specs=pl.BlockSpec((tm, tn), lambda i, j: (i, j)),
        out_shape=jax.ShapeDtypeStruct((t, n), F32),
        compiler_params=_cparams(("parallel", "parallel")),
        name="matmul_residual",
    )(*a_list, *([w] * n_a), res)


def _swap_halves(w):
    half = w.shape[-1] // 2
    return jnp.concatenate([w[..., half:], w[..., :half]], axis=-1)


def _prep_w_in_even(w):
    d = w.shape[0]
    sizes = (512, 512, 1024, 4, 4, 1024, MLA_Q_RANK, MLA_KV_RANK, MLA_ROPE)
    idx = [0]
    for s in sizes:
        idx.append(idx[-1] + s)
    q, k, v, ig, fg, og, cq, ckv, kr = [w[:, idx[i]:idx[i + 1]] for i in range(len(sizes))]
    z64 = jnp.zeros((d, LANE - MLA_ROPE), w.dtype)
    gates = jnp.concatenate([ig, fg, jnp.zeros((d, LANE - 2 * ML_HEADS), w.dtype)], axis=1)
    pad = jnp.zeros((d, P_WIDTH - P_GATE - LANE), w.dtype)
    return jnp.concatenate([q, k, v, og, cq, ckv, kr, z64, _swap_halves(kr), z64, gates, pad], axis=1).astype(BF16)


def _prep_w_uq(w):
    r = w.shape[0]
    w3 = w.reshape(r, MLA_HEADS, MLA_NOPE + MLA_ROPE)
    nope = w3[..., :MLA_NOPE].reshape(r, MLA_HEADS * MLA_NOPE)
    rope = w3[..., MLA_NOPE:]
    zpad = jnp.zeros((r, MLA_HEADS, LANE - MLA_ROPE), w.dtype)
    ra = jnp.concatenate([rope, zpad], axis=-1).reshape(r, MLA_HEADS * LANE)
    rb = jnp.concatenate([_swap_halves(rope), zpad], axis=-1).reshape(r, MLA_HEADS * LANE)
    return jnp.concatenate([nope, ra, rb], axis=1).astype(BF16)


def _prep_w_in_ssm(w):
    return jnp.pad(w, ((0, 0), (0, Z_WIDTH - w.shape[1]))).astype(BF16)


def _rope_tables(pos):
    half = MLA_ROPE // 2
    inv = ROPE_BASE ** (-jnp.arange(half, dtype=F32) / half)
    ang = pos.astype(F32)[:, None] * inv[None, :]
    c, s = jnp.cos(ang), jnp.sin(ang)
    z = jnp.zeros((pos.shape[0], LANE - MLA_ROPE), F32)
    return jnp.concatenate([c, c, z], axis=1), jnp.concatenate([-s, s, z], axis=1)


def _q_prep_body(cq_ref, g_ref, w_ref, cos_ref, sin_ref, *outs, transposed):
    cn = _rms(cq_ref[...], g_ref[...]).astype(BF16)
    a = _dot(cn, w_ref[...])
    nw = MLA_HEADS * MLA_NOPE
    cos, sin = cos_ref[...], sin_ref[...]
    for h in range(MLA_HEADS):
        qn = a[:, h * LANE:(h + 1) * LANE] * MLA_SCALE
        ra = a[:, nw + h * LANE: nw + (h + 1) * LANE]
        rb = a[:, 2 * nw + h * LANE: 2 * nw + (h + 1) * LANE]
        qr = (ra * cos + rb * sin) * MLA_SCALE
        if transposed:
            (qt_ref,) = outs
            qt_ref[h, 0:MLA_NOPE, :] = qn.T.astype(BF16)
            qt_ref[h, MLA_NOPE:MLA_NOPE + LANE, :] = qr.T.astype(BF16)
        else:
            qn_ref, qr_ref = outs
            qn_ref[:, h * LANE:(h + 1) * LANE] = qn.astype(BF16)
            qr_ref[:, h * LANE:(h + 1) * LANE] = qr.astype(BF16)


def q_prep(p, g, wq, cos, sin, *, tm, transposed=False):
    t = p.shape[0]
    nw = MLA_HEADS * LANE
    if transposed:
        qw = MLA_NOPE + LANE
        out_specs = [pl.BlockSpec((MLA_HEADS, None, qw, tm), lambda i: (0, i, 0, 0))]
        out_shape = [jax.ShapeDtypeStruct((MLA_HEADS, t // tm, qw, tm), BF16)]
    else:
        out_specs = [pl.BlockSpec((tm, nw), lambda i: (i, 0)), pl.BlockSpec((tm, nw), lambda i: (i, 0))]
        out_shape = [jax.ShapeDtypeStruct((t, nw), BF16), jax.ShapeDtypeStruct((t, nw), BF16)]
    return pl.pallas_call(
        functools.partial(_q_prep_body, transposed=transposed),
        grid=(t // tm,),
        in_specs=[
            pl.BlockSpec((tm, MLA_Q_RANK), lambda i: (i, P_CQ // MLA_Q_RANK)),
            pl.BlockSpec((1, MLA_Q_RANK), lambda i: (0, 0)),
            pl.BlockSpec(wq.shape, lambda i: (0, 0)),
            pl.BlockSpec((tm, LANE), lambda i: (i, 0)),
            pl.BlockSpec((tm, LANE), lambda i: (i, 0)),
        ],
        out_specs=out_specs,
        out_shape=out_shape,
        compiler_params=_cparams(("parallel",)),
        name="q_prep",
    )(p, g.reshape(1, -1), wq, cos, sin)


def _kv_prep_body(ckv_ref, kr_ref, krsw_ref, g_ref, cos_ref, sin_ref, *rest, expand):
    cn = _rms(ckv_ref[...], g_ref[...])
    kr = kr_ref[...] * cos_ref[...] + krsw_ref[...] * sin_ref[...]
    if expand:
        w_ref, rows_ref, k_ref, vt_ref = rest
        kvw = MLA_NOPE + MLA_V
        kv = _dot(cn.astype(BF16), w_ref[...])
        krb = kr.astype(BF16)
        for h in range(MLA_HEADS):
            k_ref[:, h * kvw:h * kvw + MLA_NOPE] = kv[:, h * kvw:h * kvw + MLA_NOPE].astype(BF16)
            k_ref[:, h * kvw + MLA_NOPE:(h + 1) * kvw] = krb
            vt_ref[h] = kv[:, h * kvw + MLA_NOPE:(h + 1) * kvw].T.astype(BF16)
    else:
        (rows_ref,) = rest
    rows_ref[:, :MLA_KV_RANK] = cn
    rows_ref[:, MLA_KV_RANK:] = kr[:, :MLA_ROPE]


def kv_prep(p, g, cos, sin, *, tm, w_ukv=None):
    t = p.shape[0]
    expand = w_ukv is not None
    in_specs = [
        pl.BlockSpec((tm, MLA_KV_RANK), lambda i: (i, P_CKV // MLA_KV_RANK)),
        pl.BlockSpec((tm, LANE), lambda i: (i, P_KR // LANE)),
        pl.BlockSpec((tm, LANE), lambda i: (i, P_KRSW // LANE)),
        pl.BlockSpec((1, MLA_KV_RANK), lambda i: (0, 0)),
        pl.BlockSpec((tm, LANE), lambda i: (i, 0)),
        pl.BlockSpec((tm, LANE), lambda i: (i, 0)),
    ]
    args = [p, p, p, g.reshape(1, -1), cos, sin]
    out_specs = [pl.BlockSpec((tm, ROW_W), lambda i: (i, 0))]
    out_shape = [jax.ShapeDtypeStruct((t, ROW_W), F32)]
    if expand:
        nkv = w_ukv.shape[1]
        in_specs.append(pl.BlockSpec(w_ukv.shape, lambda i: (0, 0)))
        args.append(w_ukv)
        out_specs += [pl.BlockSpec((tm, nkv), lambda i: (i, 0)),
                      pl.BlockSpec((MLA_HEADS, None, MLA_V, tm), lambda i: (0, i, 0, 0))]
        out_shape += [jax.ShapeDtypeStruct((t, nkv), BF16),
                      jax.ShapeDtypeStruct((MLA_HEADS, t // tm, MLA_V, tm), BF16)]
    return pl.pallas_call(
        functools.partial(_kv_prep_body, expand=expand),
        grid=(t // tm,),
        in_specs=in_specs,
        out_specs=out_specs,
        out_shape=out_shape,
        compiler_params=_cparams(("parallel",)),
        name="kv_prep",
    )(*args)


def _attn_body(qt_ref, k_ref, vt_ref, o_ref, m_ref, l_ref, acc_ref, *, tq, hp):
    qi = pl.program_id(2)
    m_ref[...] = jnp.full(m_ref.shape, -jnp.inf, F32)
    l_ref[...] = jnp.zeros(l_ref.shape, F32)
    acc_ref[...] = jnp.zeros(acc_ref.shape, F32)
    kw = MLA_NOPE + LANE
    hs = range(hp)

    def block(ki, on_diagonal):
        rows = pl.ds(pl.multiple_of(ki * tq, tq), tq)
        s = [_dot(k_ref[rows, j * kw:(j + 1) * kw], qt_ref[j]) for j in hs]
        if on_diagonal:
            visible = (lax.broadcasted_iota(jnp.int32, (tq, tq), 0) <= lax.broadcasted_iota(jnp.int32, (tq, tq), 1))
            s = [jnp.where(visible, sj, -jnp.inf) for sj in s]
        m_old = [m_ref[j] for j in hs]
        m_new = [jnp.maximum(m_old[j], jnp.max(s[j], axis=0, keepdims=True)) for j in hs]
        p = [jnp.exp(s[j] - m_new[j]) for j in hs]
        pv = [_dot(vt_ref[j, ki], p[j].astype(BF16)) for j in hs]
        corr = [jnp.exp(m_old[j] - m_new[j]) for j in hs]
        l_new = [l_ref[j] * corr[j] + jnp.sum(p[j], axis=0, keepdims=True) for j in hs]
        acc_new = [acc_ref[j] * corr[j] + pv[j] for j in hs]
        for j in hs:
            m_ref[j] = m_new[j]
            l_ref[j] = l_new[j]
            acc_ref[j] = acc_new[j]

    def below_diagonal(ki, carry):
        block(ki, False)
        return carry

    lax.fori_loop(0, qi, below_diagonal, 0)
    block(qi, True)
    for j in hs:
        o_ref[:, j * LANE:(j + 1) * LANE] = (acc_ref[j] / l_ref[j]).T.astype(o_ref.dtype)


def mla_prompt_attention(qt, k, vt, *, bsz, seq, tq, hp):
    nq = seq // tq
    kw = MLA_NOPE + LANE
    return pl.pallas_call(
        functools.partial(_attn_body, tq=tq, hp=hp),
        grid=(bsz, MLA_HEADS // hp, nq),
        in_specs=[
            pl.BlockSpec((hp, None, kw, tq), lambda b, h, qi: (h, b * nq + qi, 0, 0)),
            pl.BlockSpec((seq, hp * kw), lambda b, h, qi: (b, h)),
            pl.BlockSpec((hp, nq, MLA_V, tq), lambda b, h, qi: (h, b, 0, 0)),
        ],
        out_specs=pl.BlockSpec((tq, hp * LANE), lambda b, h, qi: (b * nq + qi, h)),
        out_shape=jax.ShapeDtypeStruct((bsz * seq, MLA_HEADS * MLA_V), BF16),
        scratch_shapes=[pltpu.VMEM((hp, 1, tq), F32), pltpu.VMEM((hp, 1, tq), F32),
                        pltpu.VMEM((hp, MLA_V, tq), F32)],
        compiler_params=_cparams(("parallel", "parallel", "arbitrary")),
        name="mla_prompt_attention",
    )(qt, k, vt)


def _paged_body(pt_ref, qn_ref, qr_ref, rows_ref, w_ref, *rest, gp, nb, n_valid):
    page_refs = rest[:nb * gp]
    o_ref, ql_ref, qrr_ref, m_ref, l_ref, acc_ref = rest[nb * gp:]
    del pt_ref
    g = pl.program_id(1)
    rows_q = MLA_HEADS * SAMPLE_PAD
    kvw = MLA_NOPE + MLA_V

    @pl.when(g == 0)
    def _():
        for sb in range(nb):
            tok = slice(sb * SAMPLE_PAD, (sb + 1) * SAMPLE_PAD)
            for h in range(MLA_HEADS):
                w_uk = w_ref[:, h * kvw: h * kvw + MLA_NOPE]
                ql = _dot_nt(qn_ref[tok, h * LANE:(h + 1) * LANE], w_uk)
                ql_ref[sb, h * SAMPLE_PAD:(h + 1) * SAMPLE_PAD, :] = ql.astype(BF16)
                qrr_ref[sb, h * SAMPLE_PAD:(h + 1) * SAMPLE_PAD, :] = qr_ref[tok, h * LANE:(h + 1) * LANE]
        m_ref[...] = jnp.full(m_ref.shape, -jnp.inf, F32)
        l_ref[...] = jnp.zeros(l_ref.shape, F32)
        acc_ref[...] = jnp.zeros(acc_ref.shape, F32)

    def update(scores, pvs):
        sbs = range(nb)
        m_old = [m_ref[sb] for sb in sbs]
        m_new = [jnp.maximum(m_old[sb], jnp.max(scores[sb], axis=-1, keepdims=True)) for sb in sbs]
        p = [jnp.exp(scores[sb] - m_new[sb]) for sb in sbs]
        pv = [pvs[sb](p[sb].astype(BF16)) for sb in sbs]
        corr = [jnp.exp(m_old[sb] - m_new[sb]) for sb in sbs]
        return [(m_new[sb], l_ref[sb] * corr[sb] + jnp.sum(p[sb], axis=-1, keepdims=True),
                 acc_ref[sb] * corr[sb] + pv[sb]) for sb in sbs]

    def commit(new):
        for sb, (m_new, l_new, acc_new) in enumerate(new):
            m_ref[sb] = m_new
            l_ref[sb] = l_new
            acc_ref[sb] = acc_new

    lat_t = [jnp.concatenate([pr[:MLA_KV_RANK, :].astype(BF16) for pr in page_refs[sb * gp:(sb + 1) * gp]], axis=1)
             for sb in range(nb)]
    rk_t = [jnp.concatenate([pr[MLA_KV_RANK:, :].astype(BF16) for pr in page_refs[sb * gp:(sb + 1) * gp]], axis=1)
            for sb in range(nb)]
    scores = [_dot(ql_ref[sb], lat_t[sb]) + _dot(qrr_ref[sb][:, :MLA_ROPE], rk_t[sb]) for sb in range(nb)]
    commit(update(scores, [lambda p, lt=lt: _dot_nt(p, lt) for lt in lat_t]))

    @pl.when(g == pl.num_programs(1) - 1)
    def _():
        tq = lax.broadcasted_iota(jnp.int32, (rows_q, SAMPLE_PAD), 0) % SAMPLE_PAD
        tk = lax.broadcasted_iota(jnp.int32, (rows_q, SAMPLE_PAD), 1)
        visible = (tk <= tq) & (tk < n_valid)
        scores, pvs = [], []
        for sb in range(nb):
            tok = slice(sb * SAMPLE_PAD, (sb + 1) * SAMPLE_PAD)
            lat = rows_ref[tok, :MLA_KV_RANK].astype(BF16)
            rk = rows_ref[tok, MLA_KV_RANK:].astype(BF16)
            s = _dot_nt(ql_ref[sb], lat) + _dot_nt(qrr_ref[sb][:, :MLA_ROPE], rk)
            scores.append(jnp.where(visible, s, -jnp.inf))
            pvs.append(lambda p, lat=lat: _dot(p, lat))
        for sb, (_, l_new, acc_new) in enumerate(update(scores, pvs)):
            tok = slice(sb * SAMPLE_PAD, (sb + 1) * SAMPLE_PAD)
            out_lat = (acc_new / l_new).astype(BF16)
            for h in range(MLA_HEADS):
                w_uv = w_ref[:, h * kvw + MLA_NOPE:(h + 1) * kvw]
                o = _dot(out_lat[h * SAMPLE_PAD:(h + 1) * SAMPLE_PAD, :], w_uv)
                o_ref[tok, h * LANE:(h + 1) * LANE] = o.astype(o_ref.dtype)


def mla_paged_attention(qn, qr, rows, w_ukv, cache_t, page_table, *, n_valid, gp, nb):
    bsz, n_pages = page_table.shape
    nw = MLA_HEADS * LANE
    npg = n_pages // gp
    rq = MLA_HEADS * SAMPLE_PAD

    def page_spec(sb, i):
        return pl.BlockSpec((None, ROW_W, PAGE_SIZE),
                            lambda b, g, pt: (pt[(b * nb + sb) * n_pages + g * gp + i], 0, 0))

    grid_spec = pltpu.PrefetchScalarGridSpec(
        num_scalar_prefetch=1,
        grid=(bsz // nb, npg),
        in_specs=[
            pl.BlockSpec((nb * SAMPLE_PAD, nw), lambda b, g, pt: (b, 0)),
            pl.BlockSpec((nb * SAMPLE_PAD, nw), lambda b, g, pt: (b, 0)),
            pl.BlockSpec((nb * SAMPLE_PAD, ROW_W), lambda b, g, pt: (b, 0)),
            pl.BlockSpec(w_ukv.shape, lambda b, g, pt: (0, 0)),
        ] + [page_spec(sb, i) for sb in range(nb) for i in range(gp)],
        out_specs=pl.BlockSpec((nb * SAMPLE_PAD, nw), lambda b, g, pt: (b, 0)),
        scratch_shapes=[
            pltpu.VMEM((nb, rq, MLA_KV_RANK), BF16),
            pltpu.VMEM((nb, rq, LANE), BF16),
            pltpu.VMEM((nb, rq, 1), F32),
            pltpu.VMEM((nb, rq, 1), F32),
            pltpu.VMEM((nb, rq, MLA_KV_RANK), F32),
        ],
    )
    return pl.pallas_call(
        functools.partial(_paged_body, gp=gp, nb=nb, n_valid=n_valid),
        grid_spec=grid_spec,
        out_shape=jax.ShapeDtypeStruct((bsz * SAMPLE_PAD, nw), BF16),
        compiler_params=_cparams(("parallel", "arbitrary")),
        name="mla_paged_attention",
    )(page_table.reshape(-1), qn, qr, rows, w_ukv, *([cache_t] * (nb * gp)))


def _mlstm_body(q_ref, k_ref, v_ref, og_ref, gc_ref, gr_ref, bc_ref, br_ref, ng_ref, c0_ref, n0_ref, m0_ref,
                h_ref, c_ref, n_ref, m_ref, *, lc, n_valid, nb, single_chunk):
    if single_chunk:
        cs_ref, ns_ref, ms_ref = c0_ref, n0_ref, m0_ref
        m_ref[...] = m0_ref[...]
    else:
        cs_ref, ns_ref, ms_ref = c_ref, n_ref, m_ref

        @pl.when(pl.program_id(1) == 0)
        def _():
            c_ref[...] = c0_ref[...]
            n_ref[...] = n0_ref[...]
            m_ref[...] = m0_ref[...]

    nh = ML_HEADS
    row = lax.broadcasted_iota(jnp.int32, (lc, lc), 0)
    col = lax.broadcasted_iota(jnp.int32, (lc, lc), 1)
    causal = col <= row
    tril = causal.astype(F32)
    triu = (row <= col).astype(F32)
    seqs = []
    for sb in range(nb):
        tok = slice(sb * lc, (sb + 1) * lc)
        gc = gc_ref[tok, :] + bc_ref[...]
        gr = gr_ref[sb] + br_ref[:, 0:1]
        li_c, lf_c = gc[:, 0:nh], _log_sigmoid(gc[:, nh:2 * nh])
        li_r, lf_r = gr[0:nh, :], _log_sigmoid(gr[nh:2 * nh, :])
        if n_valid < lc:
            vc = lax.broadcasted_iota(jnp.int32, (lc, nh), 0) < n_valid
            vr = lax.broadcasted_iota(jnp.int32, (nh, lc), 1) < n_valid
            li_c, lf_c = jnp.where(vc, li_c, NEG_BIG), jnp.where(vc, lf_c, 0.0)
            li_r, lf_r = jnp.where(vr, li_r, NEG_BIG), jnp.where(vr, lf_r, 0.0)
        seqs.append((li_c, li_r, lf_c, lf_r))
    b_cs = [_dot_sel_lhs(tril, s[2]) for s in seqs]
    b_rs = [_dot_sel_rhs(s[3], triu) for s in seqs]

    items = [(sb, h) for sb in range(nb) for h in range(nh)]
    st = []
    for sb, h in items:
        tok = slice(sb * lc, (sb + 1) * lc)
        qh = q_ref[tok, h * ML_DQK:(h + 1) * ML_DQK] * (ML_DQK ** -0.5)
        kh = k_ref[tok, h * ML_DQK:(h + 1) * ML_DQK]
        st.append(dict(
            bc=b_cs[sb][:, h:h + 1], br=b_rs[sb][h:h + 1, :],
            ic=seqs[sb][0][:, h:h + 1], ir=seqs[sb][1][h:h + 1, :],
            m_prev=ms_ref[sb, h:h + 1, 0:1], qh=qh, qb=qh.astype(BF16), kh=kh, kb=kh.astype(BF16),
            vh=v_ref[tok, h * ML_DV:(h + 1) * ML_DV], og=og_ref[tok, h * ML_DV:(h + 1) * ML_DV],
            c=cs_ref[sb, h], n=ns_ref[sb, h:h + 1, :], ng=ng_ref[:, h * ML_DV:(h + 1) * ML_DV]))
    for d in st:
        d["qk"] = _dot_nt(d["qb"], d["kb"])
        d["qc"] = _dot_nt(d["qb"], d["c"].astype(BF16))
    for d in st:
        dmat = jnp.where(causal, d["bc"] - d["br"] + d["ir"], -jnp.inf)
        inter = d["bc"] + d["m_prev"]
        d["m_t"] = jnp.maximum(inter, jnp.max(dmat, axis=-1, keepdims=True))
        d["w_inter"] = jnp.exp(inter - d["m_t"])
        d["s"] = d["qk"] * jnp.exp(dmat - d["m_t"])
        b_last = d["bc"][lc - 1:lc, :]
        g_c = b_last - d["bc"] + d["ic"]
        g_r = b_last - d["br"] + d["ir"]
        d["m_new"] = jnp.maximum(b_last + d["m_prev"], jnp.max(g_r, axis=-1, keepdims=True))
        d["decay"] = jnp.exp(b_last + d["m_prev"] - d["m_new"])
        d["wg"] = jnp.exp(g_c - d["m_new"])
    for d in st:
        d["sv"] = _dot(d["s"].astype(BF16), d["vh"].astype(BF16))
        d["vk"] = _dot_tn((d["vh"] * d["wg"]).astype(BF16), d["kb"])
    for d in st:
        num = d["sv"] + d["w_inter"] * d["qc"]
        den = (jnp.sum(d["s"], axis=-1, keepdims=True)
               + d["w_inter"] * jnp.sum(d["qh"] * d["n"], axis=-1, keepdims=True))
        hout = num / jnp.maximum(jnp.abs(den), jnp.exp(-d["m_t"]))
        d["h_new"] = (_sigmoid(d["og"]) * _rms(hout, d["ng"])).astype(h_ref.dtype)
        d["c_new"] = d["decay"] * d["c"] + d["vk"]
        d["n_new"] = d["decay"] * d["n"] + jnp.sum(d["kh"] * d["wg"], axis=0, keepdims=True)

    for (sb, h), d in zip(items, st):
        h_ref[sb * lc:(sb + 1) * lc, h * ML_DV:(h + 1) * ML_DV] = d["h_new"]
        c_ref[sb, h] = d["c_new"]
        n_ref[sb, h:h + 1, :] = d["n_new"]
        m_ref[sb, h:h + 1, :] = jnp.broadcast_to(d["m_new"], (1, LANE))


def mlstm(p, gates_rows, b_ig, b_fg, norm_g, c0, n0, m0, *, bsz, lc, nchunk, n_valid, nb):
    assert nb == 1 or nchunk == 1
    nh = ML_HEADS
    bias_c = jnp.concatenate([b_ig, b_fg, jnp.zeros((LANE - 2 * nh,), F32)]).reshape(1, LANE)
    bias_r = jnp.broadcast_to(jnp.concatenate([b_ig, b_fg]).reshape(2 * nh, 1), (2 * nh, LANE))
    tl = nb * lc

    def rows(cb):
        return lambda b, c: (b * nchunk + c, cb)

    const2 = lambda b, c: (0, 0)
    return pl.pallas_call(
        functools.partial(_mlstm_body, lc=lc, n_valid=n_valid, nb=nb, single_chunk=nchunk == 1),
        grid=(bsz // nb, nchunk),
        in_specs=[
            pl.BlockSpec((tl, nh * ML_DQK), rows(P_Q // (nh * ML_DQK))),
            pl.BlockSpec((tl, nh * ML_DQK), rows(P_K // (nh * ML_DQK))),
            pl.BlockSpec((tl, nh * ML_DV), rows(P_V // (nh * ML_DV))),
            pl.BlockSpec((tl, nh * ML_DV), rows(P_OG // (nh * ML_DV))),
            pl.BlockSpec((tl, LANE), rows(P_GATE // LANE)),
            pl.BlockSpec((nb, 2 * nh, lc), lambda b, c: (b * nchunk + c, 0, 0)),
            pl.BlockSpec((1, LANE), const2),
            pl.BlockSpec((2 * nh, LANE), const2),
            pl.BlockSpec((1, nh * ML_DV), const2),
            pl.BlockSpec((nb, nh, ML_DV, ML_DQK), lambda b, c: (b, 0, 0, 0)),
            pl.BlockSpec((nb, nh, ML_DQK), lambda b, c: (b, 0, 0)),
            pl.BlockSpec((nb, 2 * nh, LANE), lambda b, c: (b, 0, 0)),
        ],
        out_specs=[
            pl.BlockSpec((tl, nh * ML_DV), lambda b, c: (b * nchunk + c, 0)),
            pl.BlockSpec((nb, nh, ML_DV, ML_DQK), lambda b, c: (b, 0, 0, 0)),
            pl.BlockSpec((nb, nh, ML_DQK), lambda b, c: (b, 0, 0)),
            pl.BlockSpec((nb, 2 * nh, LANE), lambda b, c: (b, 0, 0)),
        ],
        out_shape=[
            jax.ShapeDtypeStruct((bsz * nchunk * lc, nh * ML_DV), BF16),
            jax.ShapeDtypeStruct((bsz, nh, ML_DV, ML_DQK), F32),
            jax.ShapeDtypeStruct((bsz, nh, ML_DQK), F32),
            jax.ShapeDtypeStruct((bsz, 2 * nh, LANE), F32),
        ],
        compiler_params=_cparams(("parallel", "arbitrary")),
        name="mlstm",
    )(p, p, p, p, p, gates_rows, bias_c, bias_r, norm_g.reshape(1, -1), c0, n0, m0)


def _gates_rows(p, lc):
    nrows = p.shape[0]
    g = p[:, P_GATE:P_GATE + 2 * ML_HEADS]
    return g.reshape(nrows // lc, lc, 2 * ML_HEADS).transpose(0, 2, 1)


def _m_pack(m):
    b = m.shape[0]
    mp = jnp.pad(m.astype(F32), ((0, 0), (0, 2 * ML_HEADS - m.shape[1])))
    return jnp.broadcast_to(mp[:, :, None], (b, 2 * ML_HEADS, LANE))


def _ssd_body(z_ref, x_ref, b_ref, c_ref, dtc_ref, dtr_ref, cwx_ref, cwb_ref, cwc_ref, cbx_ref, cbb_ref, cbc_ref,
              csx_ref, csb_ref, csc_ref, pc_ref, pr_ref, dsk_ref, ng_ref, h0_ref,
              y_ref, h_ref, bufx, bufb, bufc, *, lc, n_valid, gs, single_chunk):
    tail = SAMPLE_PAD
    cxs = [slice(gi * SSM_GW, (gi + 1) * SSM_GW) for gi in range(gs)]
    cns = [slice(gi * SSM_STATE, (gi + 1) * SSM_STATE) for gi in range(gs)]

    def init_history():
        for gi in range(gs):
            bufx[gi, 0:tail, :] = csx_ref[:, cxs[gi]]
            bufb[gi, 0:tail, :] = csb_ref[:, cns[gi]]
            bufc[gi, 0:tail, :] = csc_ref[:, cns[gi]]

    if single_chunk:
        init_history()
        hs_ref = h0_ref
    else:
        hs_ref = h_ref

        @pl.when(pl.program_id(2) == 0)
        def _():
            h_ref[...] = h0_ref[...]
            init_history()

    def conv(raw, buf, cw_ref, cb_ref, cols):
        buf[tail:tail + lc, :] = raw
        acc = cb_ref[:, cols] + cw_ref[SSM_CONV - 1:SSM_CONV, cols] * raw
        for j in range(SSM_CONV - 1):
            off = tail - (SSM_CONV - 1) + j
            acc = acc + cw_ref[j:j + 1, cols] * buf[off:off + lc, :]
        if not single_chunk:
            buf[0:tail, :] = raw[lc - tail:lc, :]
        return _silu(acc)

    convs = []
    for gi in range(gs):
        convs.append((conv(x_ref[:, cxs[gi]], bufx.at[gi], cwx_ref, cbx_ref, cxs[gi]),
                      conv(b_ref[:, cns[gi]], bufb.at[gi], cwb_ref, cbb_ref, cns[gi]),
                      conv(c_ref[:, cns[gi]], bufc.at[gi], cwc_ref, cbc_ref, cns[gi])))

    outs = _ssd_groups(
        convs,
        [dict(z=z_ref[:, cxs[gi]], dtc=dtc_ref[:, gi * SSM_HG:(gi + 1) * SSM_HG],
              dtr=dtr_ref[gi * SSM_HG:(gi + 1) * SSM_HG, :], pc=pc_ref[gi], pr=pr_ref[gi],
              dsk=dsk_ref[:, cxs[gi]], ng=ng_ref[:, cxs[gi]], h=hs_ref[gi]) for gi in range(gs)],
        lc=lc, n_valid=n_valid)

    for gi, (y, h_new) in enumerate(outs):
        y_ref[:, cxs[gi]] = y.astype(y_ref.dtype)
        h_ref[gi] = h_new


def _ssd_groups(convs, ins, *, lc, n_valid):
    hg = SSM_HG
    gs = range(len(ins))
    row = lax.broadcasted_iota(jnp.int32, (lc, lc), 0)
    col = lax.broadcasted_iota(jnp.int32, (lc, lc), 1)
    causal = col <= row
    tril = causal.astype(F32)
    triu = (row <= col).astype(F32)
    expand = (lax.broadcasted_iota(jnp.int32, (hg, SSM_GW), 1) // SSM_HEADDIM
              == lax.broadcasted_iota(jnp.int32, (hg, SSM_GW), 0)).astype(F32)
    lane_head = lax.broadcasted_iota(jnp.int32, (lc, LANE), 1) // SSM_HEADDIM

    dt_c, dt_r = [], []
    for g in gs:
        c = _softplus(ins[g]["dtc"] + ins[g]["pc"][0:1, 0:hg])
        r = _softplus(ins[g]["dtr"] + ins[g]["pr"][:, 0:1])
        if n_valid < lc:
            c = jnp.where(lax.broadcasted_iota(jnp.int32, (lc, hg), 0) < n_valid, c, 0.0)
            r = jnp.where(lax.broadcasted_iota(jnp.int32, (hg, lc), 1) < n_valid, r, 0.0)
        dt_c.append(c)
        dt_r.append(r)
    cum_c = [_dot_sel_lhs(tril, dt_c[g] * (-jnp.exp(ins[g]["pc"][1:2, 0:hg]))) for g in gs]
    cum_r = [_dot_sel_rhs(dt_r[g] * (-jnp.exp(ins[g]["pr"][:, 1:2])), triu) for g in gs]
    cb16 = [convs[g][2].astype(BF16) for g in gs]
    bb16 = [convs[g][1].astype(BF16) for g in gs]
    cbm = [_dot_nt(cb16[g], bb16[g]) for g in gs]
    yh = [_dot_nt(cb16[g], ins[g]["h"].astype(BF16)) for g in gs]
    e_in = [_dot_sel_rhs(jnp.exp(cum_c[g]), expand) for g in gs]
    e_st = [_dot_sel_rhs(jnp.exp(cum_c[g][lc - 1:lc, :] - cum_c[g]) * dt_c[g], expand) for g in gs]
    upd = [_dot_tn((convs[g][0] * e_st[g]).astype(BF16), bb16[g]) for g in gs]

    pieces = [[] for _ in gs]
    for pair in range(hg // 2):
        parts = [[] for _ in gs]
        for sub in range(2):
            j = 2 * pair + sub
            for g in gs:
                seg = jnp.where(causal, cum_c[g][:, j:j + 1] - cum_r[g][j:j + 1, :], -jnp.inf)
                w = cbm[g] * jnp.exp(seg) * dt_r[g][j:j + 1, :]
                xm = jnp.where(lane_head == sub, convs[g][0][:, pair * LANE:(pair + 1) * LANE], 0.0)
                parts[g].append(_dot(w.astype(BF16), xm.astype(BF16)))
        for g in gs:
            pieces[g].append(parts[g][0] + parts[g][1])

    outs = []
    for g in gs:
        xa, h_old = convs[g][0], ins[g]["h"]
        y = yh[g] * e_in[g] + jnp.concatenate(pieces[g], axis=1)
        dec_r = jnp.exp(cum_r[g][:, lc - 1:lc])
        h_new = jnp.concatenate(
            [dec_r[j:j + 1, :] * h_old[j * SSM_HEADDIM:(j + 1) * SSM_HEADDIM, :]
             + upd[g][j * SSM_HEADDIM:(j + 1) * SSM_HEADDIM, :] for j in range(hg)], axis=0)
        y = (y + ins[g]["dsk"] * xa) * _silu(ins[g]["z"])
        outs.append((_rms(y, ins[g]["ng"]), h_new))
    return outs


def ssd(zp, dt_rows, conv_w, conv_b, conv_state8, dt_bias, a_log, d_skip, norm_g, h0, *, bsz, lc, nchunk,
        n_valid, gs):
    assert gs == SSM_GROUPS
    g_, hg = SSM_GROUPS // gs, SSM_HG * gs
    gw, ns = SSM_GW * gs, SSM_STATE * gs

    def rows(off, width):
        assert off % width == 0
        return lambda b, g, c: (b * nchunk + c, off // width + g)

    ng_all, nh1 = SSM_GROUPS, SSM_HG
    pcol = jnp.zeros((ng_all, 8, LANE), F32)
    pcol = pcol.at[:, 0, :nh1].set(dt_bias.reshape(ng_all, nh1)).at[:, 1, :nh1].set(a_log.reshape(ng_all, nh1))
    prow = jnp.zeros((ng_all, nh1, LANE), F32)
    prow = prow.at[:, :, 0].set(dt_bias.reshape(ng_all, nh1)).at[:, :, 1].set(a_log.reshape(ng_all, nh1))
    dsk = jnp.repeat(d_skip, SSM_HEADDIM).reshape(1, SSM_D_INNER)
    cb2 = conv_b.reshape(1, -1)
    bo, co = SSM_D_INNER // ns, SSM_D_INNER // ns + g_
    assert SSM_D_INNER % ns == 0

    in_specs = [
        pl.BlockSpec((lc, gw), rows(0, gw)),
        pl.BlockSpec((lc, gw), rows(Z_X, gw)),
        pl.BlockSpec((lc, ns), rows(Z_B, ns)),
        pl.BlockSpec((lc, ns), rows(Z_C, ns)),
        pl.BlockSpec((lc, LANE), rows(Z_DT, LANE)),
        pl.BlockSpec((None, hg, lc), lambda b, g, c: (b * nchunk + c, g, 0)),
        pl.BlockSpec((SSM_CONV, gw), lambda b, g, c: (0, g)),
        pl.BlockSpec((SSM_CONV, ns), lambda b, g, c: (0, bo + g)),
        pl.BlockSpec((SSM_CONV, ns), lambda b, g, c: (0, co + g)),
        pl.BlockSpec((1, gw), lambda b, g, c: (0, g)),
        pl.BlockSpec((1, ns), lambda b, g, c: (0, bo + g)),
        pl.BlockSpec((1, ns), lambda b, g, c: (0, co + g)),
        pl.BlockSpec((None, 8, gw), lambda b, g, c: (b, 0, g)),
        pl.BlockSpec((None, 8, ns), lambda b, g, c: (b, 0, bo + g)),
        pl.BlockSpec((None, 8, ns), lambda b, g, c: (b, 0, co + g)),
        pl.BlockSpec((gs, 8, LANE), lambda b, g, c: (g, 0, 0)),
        pl.BlockSpec((gs, nh1, LANE), lambda b, g, c: (g, 0, 0)),
        pl.BlockSpec((1, gw), lambda b, g, c: (0, g)),
        pl.BlockSpec((1, gw), lambda b, g, c: (0, g)),
        pl.BlockSpec((None, gs, SSM_GW, SSM_STATE), lambda b, g, c: (b, g, 0, 0)),
    ]
    return pl.pallas_call(
        functools.partial(_ssd_body, lc=lc, n_valid=n_valid, gs=gs, single_chunk=nchunk == 1),
        grid=(bsz, g_, nchunk),
        in_specs=in_specs,
        out_specs=[
            pl.BlockSpec((lc, gw), lambda b, g, c: (b * nchunk + c, g)),
            pl.BlockSpec((None, gs, SSM_GW, SSM_STATE), lambda b, g, c: (b, g, 0, 0)),
        ],
        out_shape=[
            jax.ShapeDtypeStruct((bsz * nchunk * lc, SSM_D_INNER), BF16),
            jax.ShapeDtypeStruct((bsz, SSM_GROUPS, SSM_GW, SSM_STATE), F32),
        ],
        scratch_shapes=[
            pltpu.VMEM((gs, lc + SAMPLE_PAD, SSM_GW), F32),
            pltpu.VMEM((gs, lc + SAMPLE_PAD, SSM_STATE), F32),
            pltpu.VMEM((gs, lc + SAMPLE_PAD, SSM_STATE), F32),
        ],
        compiler_params=_cparams(("parallel", "parallel", "arbitrary")),
        name="ssd",
    )(zp, zp, zp, zp, zp, dt_rows, conv_w, conv_w, conv_w, cb2, cb2, cb2, conv_state8, conv_state8, conv_state8,
      pcol, prow, dsk, norm_g.reshape(1, -1), h0)


def _dt_rows(zp, lc):
    nrows = zp.shape[0]
    return zp[:, Z_DT:Z_DT + SSM_HEADS].reshape(nrows // lc, lc, SSM_HEADS).transpose(0, 2, 1)


def _conv_state8(state):
    return jnp.pad(state, ((0, 0), (SAMPLE_PAD - (SSM_CONV - 1), 0), (0, 0)))


def _trunk(x_prompt, x_sample, cache_mla, state_mlstm_C, state_mlstm_n, state_mlstm_m, state_ssm, state_conv,
           page_table, norm_g, ffn_w_gate, ffn_w_up, ffn_w_down, w_in_even, b_ig, b_fg, mlstm_norm_g, mla_q_norm_g,
           mla_kv_norm_g, w_uq, w_ukv, w_out_even, w_in_ssm, conv_w, conv_b, dt_bias, A_log, D_skip, ssm_norm_g,
           w_out_ssm, final_norm_g, *, tm, tf, tf_s, tn, tn_ie, tn_oe, tn_ssm, tmp, lc_ml, nb_ml, lc_ssd, gs_p, gs_s, tq, hp, gp, nb_pg):
    bp, sp, d = x_prompt.shape
    bs, ts, _ = x_sample.shape
    tp = bp * sp
    tsp = bs * SAMPLE_PAD
    xs = [x_prompt.reshape(tp, d), x_sample.reshape(bs * ts, d)]
    tms = [min(tm, tp), min(tm, bs * ts)]
    tmps = [min(tmp, tp), min(tmp, tsp)]
    assert tmps[0] == tq
    tables = [_rope_tables(jnp.tile(jnp.arange(sp), bp)),
              _rope_tables(jnp.tile(PAST_LEN + jnp.arange(SAMPLE_PAD), bs))]

    def pad_seq(a):
        return jnp.pad(a.reshape(bs, ts, a.shape[1]), ((0, 0), (0, SAMPLE_PAD - ts), (0, 0))).reshape(tsp, a.shape[1])

    def unpad_seq(a):
        return a.reshape(bs, SAMPLE_PAD, a.shape[1])[:, :ts].reshape(bs * ts, a.shape[1])

    def ff(xs, g, layer, which, final_g=None):
        return [ffn(x, g, ffn_w_gate, ffn_w_up, ffn_w_down, layer, which, tm=t, tf=f, final_g=final_g)
                for x, t, f in zip(xs, tms, (tf, tf_s))]

    xs = ff(xs, norm_g[0, 0], 0, 0)
    w_in = _prep_w_in_even(w_in_even)
    p_p, p_s = [norm_matmul(x, norm_g[0, 1], w_in, tm=t, tn=tn_ie) for x, t in zip(xs, tms)]
    p_s = pad_seq(p_s)

    zeros = functools.partial(jnp.zeros, dtype=F32)
    hm_p, c_p, n_p, m_p = mlstm(
        p_p, _gates_rows(p_p, lc_ml), b_ig, b_fg, mlstm_norm_g,
        zeros((bp, ML_HEADS, ML_DV, ML_DQK)), zeros((bp, ML_HEADS, ML_DQK)), zeros((bp, 2 * ML_HEADS, LANE)),
        bsz=bp, lc=lc_ml, nchunk=sp // lc_ml, n_valid=lc_ml, nb=1)
    hm_s, c_s, n_s, m_s = mlstm(
        p_s, _gates_rows(p_s, SAMPLE_PAD), b_ig, b_fg, mlstm_norm_g,
        state_mlstm_C, state_mlstm_n, _m_pack(state_mlstm_m),
        bsz=bs, lc=SAMPLE_PAD, nchunk=1, n_valid=ts, nb=nb_ml)

    w_ukv16 = w_ukv.astype(BF16)
    w_q = _prep_w_uq(w_uq)
    (qt_p,) = q_prep(p_p, mla_q_norm_g, w_q, *tables[0], tm=tmps[0], transposed=True)
    qn_s, qr_s = q_prep(p_s, mla_q_norm_g, w_q, *tables[1], tm=tmps[1])
    rows_p, k_p, vt_p = kv_prep(p_p, mla_kv_norm_g, *tables[0], tm=tmps[0], w_ukv=w_ukv16)
    (rows_s,) = kv_prep(p_s, mla_kv_norm_g, *tables[1], tm=tmps[1])
    ha_p = mla_prompt_attention(qt_p, k_p, vt_p, bsz=bp, seq=sp, tq=tq, hp=hp)
    cache_t = jnp.swapaxes(cache_mla, 1, 2)
    ha_s = mla_paged_attention(qn_s, qr_s, rows_s, w_ukv16, cache_t, page_table, n_valid=ts, gp=gp, nb=nb_pg)

    w_oe = w_out_even.astype(BF16)
    xs = [matmul_residual([hm, ha], w_oe, x, tm=t, tn=tn_oe)
          for hm, ha, x, t in zip((hm_p, unpad_seq(hm_s)), (ha_p, unpad_seq(ha_s)), xs, tms)]
    xs = ff(xs, norm_g[0, 2], 0, 1)

    xs = ff(xs, norm_g[1, 0], 1, 0)
    w_is = _prep_w_in_ssm(w_in_ssm)
    z_p, z_s4 = [norm_matmul(x, norm_g[1, 1], w_is, tm=t, tn=tn_ssm) for x, t in zip(xs, tms)]
    z_s = pad_seq(z_s4)
    y_p, ssm_p = ssd(
        z_p, _dt_rows(z_p, lc_ssd), conv_w, conv_b, zeros((bp, SAMPLE_PAD, SSM_CONV_DIM)), dt_bias, A_log,
        D_skip, ssm_norm_g, zeros((bp, SSM_GROUPS, SSM_GW, SSM_STATE)),
        bsz=bp, lc=lc_ssd, nchunk=sp // lc_ssd, n_valid=lc_ssd, gs=gs_p)
    y_s, ssm_s = ssd(
        z_s, _dt_rows(z_s, SAMPLE_PAD), conv_w, conv_b, _conv_state8(state_conv), dt_bias, A_log,
        D_skip, ssm_norm_g, state_ssm.reshape(bs, SSM_GROUPS, SSM_GW, SSM_STATE),
        bsz=bs, lc=SAMPLE_PAD, nchunk=1, n_valid=ts, gs=gs_s)
    w_os = w_out_ssm.astype(BF16)
    xs = [matmul_residual([y], w_os, x, tm=t, tn=tn) for y, x, t in zip((y_p, unpad_seq(y_s)), xs, tms)]
    out_p, out_s = ff(xs, norm_g[1, 2], 1, 1, final_g=final_norm_g)

    keep = SSM_CONV - 1
    assert sp >= keep and ts >= keep
    conv_p = jnp.stack([lax.slice(z_p, (b * sp + sp - keep, Z_X), (b * sp + sp, Z_DT)) for b in range(bp)])
    conv_s = z_s4[:, Z_X:Z_DT].reshape(bs, ts, SSM_CONV_DIM)[:, ts - keep:ts]
    hshape = (SSM_HEADS, SSM_HEADDIM, SSM_STATE)
    return (out_p.reshape(bp, sp, d), out_s.reshape(bs, ts, d), rows_p.reshape(bp, sp, ROW_W),
            rows_s.reshape(bs, SAMPLE_PAD, ROW_W)[:, :ts],
            c_p, c_s, n_p, n_s, m_p[:, :ML_HEADS, 0], m_s[:, :ML_HEADS, 0],
            ssm_p.reshape((bp,) + hshape), ssm_s.reshape((bs,) + hshape), conv_p, conv_s)


def kernel(x_prompt, x_sample, cache_mla, state_mlstm_C, state_mlstm_n, state_mlstm_m, state_ssm, state_conv, page_table, norm_g, ffn_w_gate, ffn_w_up, ffn_w_down, w_in_even, b_ig, b_fg, mlstm_norm_g, mla_q_norm_g, mla_kv_norm_g, w_uq, w_ukv, w_out_even, w_in_ssm, conv_w, conv_b, dt_bias, A_log, D_skip, ssm_norm_g, w_out_ssm, final_norm_g):
    return _trunk(x_prompt, x_sample, cache_mla, state_mlstm_C, state_mlstm_n, state_mlstm_m, state_ssm, state_conv,
                  page_table, norm_g, ffn_w_gate, ffn_w_up, ffn_w_down, w_in_even, b_ig, b_fg, mlstm_norm_g,
                  mla_q_norm_g, mla_kv_norm_g, w_uq, w_ukv, w_out_even, w_in_ssm, conv_w, conv_b, dt_bias, A_log,
                  D_skip, ssm_norm_g, w_out_ssm, final_norm_g,
                  tm=1024, tf=256, tf_s=512, tn=512, tn_ie=1536, tn_oe=1024, tn_ssm=1152, tmp=512, lc_ml=256, nb_ml=4, lc_ssd=128, gs_p=8, gs_s=8,
                  tq=512, hp=2, gp=32, nb_pg=2)
```

```python
import functools

import jax
import jax.numpy as jnp
from jax import lax
from jax.experimental import pallas as pl
from jax.experimental.pallas import tpu as pltpu

F32 = jnp.float32
BF16 = jnp.bfloat16

EPS = 1e-6
PAST_LEN = 8192
PAGE_SIZE = 128

ML_HEADS = 4
ML_DQK = 128
ML_DV = 256

MLA_HEADS = 8
MLA_NOPE = 128
MLA_ROPE = 64
MLA_V = 128
MLA_Q_RANK = 512
MLA_KV_RANK = 512
MLA_SCALE = (MLA_NOPE + MLA_ROPE) ** -0.5
ROPE_BASE = 10000.0
ROW_W = MLA_KV_RANK + MLA_ROPE

SSM_D_INNER = 4096
SSM_HEADDIM = 64
SSM_HEADS = 64
SSM_GROUPS = 8
SSM_HG = SSM_HEADS // SSM_GROUPS
SSM_GW = SSM_D_INNER // SSM_GROUPS
SSM_STATE = 128
SSM_CONV = 4
SSM_CONV_DIM = SSM_D_INNER + 2 * SSM_GROUPS * SSM_STATE

SAMPLE_PAD = 8
LANE = 128
VMEM_LIMIT = 56 * 1024 * 1024

P_Q, P_K, P_V, P_OG, P_CQ, P_CKV = 0, 512, 1024, 2048, 3072, 3584
P_KR, P_KRSW, P_GATE, P_WIDTH = 4096, 4224, 4352, 4608
Z_X, Z_B, Z_C, Z_DT, Z_WIDTH = 4096, 8192, 9216, 10240, 10368

NEG_BIG = -1e30


def _cparams(sem):
    return pltpu.CompilerParams(dimension_semantics=sem, vmem_limit_bytes=VMEM_LIMIT)


def _rms(x, g):
    return x * lax.rsqrt(jnp.mean(x * x, axis=-1, keepdims=True) + EPS) * g


def _sigmoid(x):
    return 1.0 / (1.0 + jnp.exp(-x))


def _silu(x):
    return x * _sigmoid(x)


def _log_sigmoid(x):
    return jnp.minimum(x, 0.0) - jnp.log1p(jnp.exp(-jnp.abs(x)))


def _softplus(x):
    return jnp.maximum(x, 0.0) + jnp.log1p(jnp.exp(-jnp.abs(x)))


def _dot(a, b):
    return jnp.dot(a, b, preferred_element_type=F32)


def _dot_nt(a, b):
    return lax.dot_general(a, b, (((1,), (1,)), ((), ())), preferred_element_type=F32)


def _dot_tn(a, b):
    return lax.dot_general(a, b, (((0,), (0,)), ((), ())), preferred_element_type=F32)


def _split3(x):
    x1 = x.astype(BF16)
    r1 = x - x1.astype(F32)
    x2 = r1.astype(BF16)
    x3 = (r1 - x2.astype(F32)).astype(BF16)
    return x1, x2, x3


def _dot_sel_lhs(sel, x):
    s = sel.astype(BF16)
    x1, x2, x3 = _split3(x)
    return (_dot(s, x3) + _dot(s, x2)) + _dot(s, x1)


def _dot_sel_rhs(x, sel):
    s = sel.astype(BF16)
    x1, x2, x3 = _split3(x)
    return (_dot(x3, s) + _dot(x2, s)) + _dot(x1, s)


def _ffn_body(x_ref, g_ref, wg_ref, wu_ref, wd_ref, *rest, final_norm):
    if final_norm:
        gf_ref, o_ref, xn_ref = rest
    else:
        o_ref, xn_ref = rest
    j = pl.program_id(1)

    @pl.when(j == 0)
    def _():
        x = x_ref[...]
        xn_ref[...] = _rms(x, g_ref[...]).astype(BF16)
        o_ref[...] = x

    xn = xn_ref[...]
    hg = _dot(xn, wg_ref[...].astype(BF16))
    hu = _dot(xn, wu_ref[...].astype(BF16))
    h = _silu(hg) * (0.5 * hu)
    o_ref[...] += _dot(h.astype(BF16), wd_ref[...].astype(BF16))

    if final_norm:
        @pl.when(j == pl.num_programs(1) - 1)
        def _():
            o_ref[...] = _rms(o_ref[...], gf_ref[...])


def ffn(x, g, w_gate, w_up, w_down, layer, which, *, tm, tf, final_g=None):
    t, d = x.shape
    dff = w_gate.shape[-1]
    final_norm = final_g is not None
    in_specs = [
        pl.BlockSpec((tm, d), lambda i, j: (i, 0), pipeline_mode=pl.Buffered(1)),
        pl.BlockSpec((1, d), lambda i, j: (0, 0)),
        pl.BlockSpec((None, None, d, tf), lambda i, j: (layer, which, 0, j)),
        pl.BlockSpec((None, None, d, tf), lambda i, j: (layer, which, 0, j)),
        pl.BlockSpec((None, None, tf, d), lambda i, j: (layer, which, j, 0)),
    ]
    args = [x, g.reshape(1, d), w_gate, w_up, w_down]
    if final_norm:
        in_specs.append(pl.BlockSpec((1, d), lambda i, j: (0, 0)))
        args.append(final_g.reshape(1, d))
    return pl.pallas_call(
        functools.partial(_ffn_body, final_norm=final_norm),
        grid=(t // tm, dff // tf),
        in_specs=in_specs,
        out_specs=pl.BlockSpec((tm, d), lambda i, j: (i, 0)),
        out_shape=jax.ShapeDtypeStruct((t, d), F32),
        scratch_shapes=[pltpu.VMEM((tm, d), BF16)],
        compiler_params=_cparams(("parallel", "arbitrary")),
        name="ffn",
    )(*args)


def _norm_mm_body(x_ref, g_ref, w_ref, o_ref, xn_ref):
    @pl.when(pl.program_id(1) == 0)
    def _():
        xn_ref[...] = _rms(x_ref[...], g_ref[...]).astype(BF16)

    o_ref[...] = _dot(xn_ref[...], w_ref[...].astype(BF16))


def norm_matmul(x, g, w, *, tm, tn):
    t, d = x.shape
    n = w.shape[1]
    return pl.pallas_call(
        _norm_mm_body,
        grid=(t // tm, n // tn),
        in_specs=[
            pl.BlockSpec((tm, d), lambda i, j: (i, 0)),
            pl.BlockSpec((1, d), lambda i, j: (0, 0)),
            pl.BlockSpec((d, tn), lambda i, j: (0, j)),
        ],
        out_specs=pl.BlockSpec((tm, tn), lambda i, j: (i, j)),
        out_shape=jax.ShapeDtypeStruct((t, n), F32),
        scratch_shapes=[pltpu.VMEM((tm, d), BF16)],
        compiler_params=_cparams(("parallel", "arbitrary")),
        name="norm_matmul",
    )(x, g.reshape(1, d), w)


def _mm_res_body(*refs, n_a):
    a_refs, w_refs = refs[:n_a], refs[n_a:2 * n_a]
    res_ref, o_ref = refs[2 * n_a], refs[2 * n_a + 1]
    acc = res_ref[...]
    for a_ref, w_ref in zip(a_refs, w_refs):
        acc = acc + _dot(a_ref[...].astype(BF16), w_ref[...].astype(BF16))
    o_ref[...] = acc


def matmul_residual(a_list, w, res, *, tm, tn):
    n_a = len(a_list)
    t, ka = a_list[0].shape
    n = w.shape[1]
    in_specs = [pl.BlockSpec((tm, ka), lambda i, j: (i, 0)) for _ in a_list]
    in_specs += [pl.BlockSpec((ka, tn), lambda i, j, s=s: (s, j)) for s in range(n_a)]
    in_specs.append(pl.BlockSpec((tm, tn), lambda i, j: (i, j)))
    return pl.pallas_call(
        functools.partial(_mm_res_body, n_a=n_a),
        grid=(t // tm, n // tn),
        in_specs=in_specs,
        out_specs=pl.BlockSpec((tm, tn), lambda i, j: (i, j)),
        out_shape=jax.ShapeDtypeStruct((t, n), F32),
        compiler_params=_cparams(("parallel", "parallel")),
        name="matmul_residual",
    )(*a_list, *([w] * n_a), res)


def _swap_halves(w):
    half = w.shape[-1] // 2
    return jnp.concatenate([w[..., half:], w[..., :half]], axis=-1)


def _prep_w_in_even(w):
    d = w.shape[0]
    sizes = (512, 512, 1024, 4, 4, 1024, MLA_Q_RANK, MLA_KV_RANK, MLA_ROPE)
    idx = [0]
    for s in sizes:
        idx.append(idx[-1] + s)
    q, k, v, ig, fg, og, cq, ckv, kr = [w[:, idx[i]:idx[i + 1]] for i in range(len(sizes))]
    z64 = jnp.zeros((d, LANE - MLA_ROPE), w.dtype)
    gates = jnp.concatenate([ig, fg, jnp.zeros((d, LANE - 2 * ML_HEADS), w.dtype)], axis=1)
    pad = jnp.zeros((d, P_WIDTH - P_GATE - LANE), w.dtype)
    return jnp.concatenate([q, k, v, og, cq, ckv, kr, z64, _swap_halves(kr), z64, gates, pad], axis=1).astype(BF16)


def _prep_w_uq(w):
    r = w.shape[0]
    w3 = w.reshape(r, MLA_HEADS, MLA_NOPE + MLA_ROPE)
    nope = w3[..., :MLA_NOPE].reshape(r, MLA_HEADS * MLA_NOPE)
    rope = w3[..., MLA_NOPE:]
    zpad = jnp.zeros((r, MLA_HEADS, LANE - MLA_ROPE), w.dtype)
    ra = jnp.concatenate([rope, zpad], axis=-1).reshape(r, MLA_HEADS * LANE)
    rb = jnp.concatenate([_swap_halves(rope), zpad], axis=-1).reshape(r, MLA_HEADS * LANE)
    return jnp.concatenate([nope, ra, rb], axis=1).astype(BF16)


def _prep_w_in_ssm(w):
    return jnp.pad(w, ((0, 0), (0, Z_WIDTH - w.shape[1]))).astype(BF16)


def _rope_tables(pos):
    half = MLA_ROPE // 2
    inv = ROPE_BASE ** (-jnp.arange(half, dtype=F32) / half)
    ang = pos.astype(F32)[:, None] * inv[None, :]
    c, s = jnp.cos(ang), jnp.sin(ang)
    z = jnp.zeros((pos.shape[0], LANE - MLA_ROPE), F32)
    return jnp.concatenate([c, c, z], axis=1), jnp.concatenate([-s, s, z], axis=1)


def _q_prep_body(cq_ref, g_ref, w_ref, cos_ref, sin_ref, *outs, transposed):
    cn = _rms(cq_ref[...], g_ref[...]).astype(BF16)
    a = _dot(cn, w_ref[...])
    nw = MLA_HEADS * MLA_NOPE
    cos, sin = cos_ref[...], sin_ref[...]
    for h in range(MLA_HEADS):
        qn = a[:, h * LANE:(h + 1) * LANE] * MLA_SCALE
        ra = a[:, nw + h * LANE: nw + (h + 1) * LANE]
        rb = a[:, 2 * nw + h * LANE: 2 * nw + (h + 1) * LANE]
        qr = (ra * cos + rb * sin) * MLA_SCALE
        if transposed:
            (qt_ref,) = outs
            qt_ref[h, 0:MLA_NOPE, :] = qn.T.astype(BF16)
            qt_ref[h, MLA_NOPE:MLA_NOPE + LANE, :] = qr.T.astype(BF16)
        else:
            qn_ref, qr_ref = outs
            qn_ref[:, h * LANE:(h + 1) * LANE] = qn.astype(BF16)
            qr_ref[:, h * LANE:(h + 1) * LANE] = qr.astype(BF16)


def q_prep(p, g, wq, cos, sin, *, tm, transposed=False):
    t = p.shape[0]
    nw = MLA_HEADS * LANE
    if transposed:
        qw = MLA_NOPE + LANE
        out_specs = [pl.BlockSpec((MLA_HEADS, None, qw, tm), lambda i: (0, i, 0, 0))]
        out_shape = [jax.ShapeDtypeStruct((MLA_HEADS, t // tm, qw, tm), BF16)]
    else:
        out_specs = [pl.BlockSpec((tm, nw), lambda i: (i, 0)), pl.BlockSpec((tm, nw), lambda i: (i, 0))]
        out_shape = [jax.ShapeDtypeStruct((t, nw), BF16), jax.ShapeDtypeStruct((t, nw), BF16)]
    return pl.pallas_call(
        functools.partial(_q_prep_body, transposed=transposed),
        grid=(t // tm,),
        in_specs=[
            pl.BlockSpec((tm, MLA_Q_RANK), lambda i: (i, P_CQ // MLA_Q_RANK)),
            pl.BlockSpec((1, MLA_Q_RANK), lambda i: (0, 0)),
            pl.BlockSpec(wq.shape, lambda i: (0, 0)),
            pl.BlockSpec((tm, LANE), lambda i: (i, 0)),
            pl.BlockSpec((tm, LANE), lambda i: (i, 0)),
        ],
        out_specs=out_specs,
        out_shape=out_shape,
        compiler_params=_cparams(("parallel",)),
        name="q_prep",
    )(p, g.reshape(1, -1), wq, cos, sin)


def _kv_prep_body(ckv_ref, kr_ref, krsw_ref, g_ref, cos_ref, sin_ref, *rest, expand):
    cn = _rms(ckv_ref[...], g_ref[...])
    kr = kr_ref[...] * cos_ref[...] + krsw_ref[...] * sin_ref[...]
    if expand:
        w_ref, rows_ref, k_ref, vt_ref = rest
        kvw = MLA_NOPE + MLA_V
        kv = _dot(cn.astype(BF16), w_ref[...])
        krb = kr.astype(BF16)
        for h in range(MLA_HEADS):
            k_ref[:, h * kvw:h * kvw + MLA_NOPE] = kv[:, h * kvw:h * kvw + MLA_NOPE].astype(BF16)
            k_ref[:, h * kvw + MLA_NOPE:(h + 1) * kvw] = krb
            vt_ref[h] = kv[:, h * kvw + MLA_NOPE:(h + 1) * kvw].T.astype(BF16)
    else:
        (rows_ref,) = rest
    rows_ref[:, :MLA_KV_RANK] = cn
    rows_ref[:, MLA_KV_RANK:] = kr[:, :MLA_ROPE]


def kv_prep(p, g, cos, sin, *, tm, w_ukv=None):
    t = p.shape[0]
    expand = w_ukv is not None
    in_specs = [
        pl.BlockSpec((tm, MLA_KV_RANK), lambda i: (i, P_CKV // MLA_KV_RANK)),
        pl.BlockSpec((tm, LANE), lambda i: (i, P_KR // LANE)),
        pl.BlockSpec((tm, LANE), lambda i: (i, P_KRSW // LANE)),
        pl.BlockSpec((1, MLA_KV_RANK), lambda i: (0, 0)),
        pl.BlockSpec((tm, LANE), lambda i: (i, 0)),
        pl.BlockSpec((tm, LANE), lambda i: (i, 0)),
    ]
    args = [p, p, p, g.reshape(1, -1), cos, sin]
    out_specs = [pl.BlockSpec((tm, ROW_W), lambda i: (i, 0))]
    out_shape = [jax.ShapeDtypeStruct((t, ROW_W), F32)]
    if expand:
        nkv = w_ukv.shape[1]
        in_specs.append(pl.BlockSpec(w_ukv.shape, lambda i: (0, 0)))
        args.append(w_ukv)
        out_specs += [pl.BlockSpec((tm, nkv), lambda i: (i, 0)),
                      pl.BlockSpec((MLA_HEADS, None, MLA_V, tm), lambda i: (0, i, 0, 0))]
        out_shape += [jax.ShapeDtypeStruct((t, nkv), BF16),
                      jax.ShapeDtypeStruct((MLA_HEADS, t // tm, MLA_V, tm), BF16)]
    return pl.pallas_call(
        functools.partial(_kv_prep_body, expand=expand),
        grid=(t // tm,),
        in_specs=in_specs,
        out_specs=out_specs,
        out_shape=out_shape,
        compiler_params=_cparams(("parallel",)),
        name="kv_prep",
    )(*args)


def _attn_body(qt_ref, k_ref, vt_ref, o_ref, m_ref, l_ref, acc_ref, *, tq, hp):
    qi = pl.program_id(2)
    m_ref[...] = jnp.full(m_ref.shape, -jnp.inf, F32)
    l_ref[...] = jnp.zeros(l_ref.shape, F32)
    acc_ref[...] = jnp.zeros(acc_ref.shape, F32)
    kw = MLA_NOPE + LANE
    hs = range(hp)

    def block(ki, on_diagonal):
        rows = pl.ds(pl.multiple_of(ki * tq, tq), tq)
        s = [_dot(k_ref[rows, j * kw:(j + 1) * kw], qt_ref[j]) for j in hs]
        if on_diagonal:
            visible = (lax.broadcasted_iota(jnp.int32, (tq, tq), 0) <= lax.broadcasted_iota(jnp.int32, (tq, tq), 1))
            s = [jnp.where(visible, sj, -jnp.inf) for sj in s]
        m_old = [m_ref[j] for j in hs]
        m_new = [jnp.maximum(m_old[j], jnp.max(s[j], axis=0, keepdims=True)) for j in hs]
        p = [jnp.exp(s[j] - m_new[j]) for j in hs]
        pv = [_dot(vt_ref[j, ki], p[j].astype(BF16)) for j in hs]
        corr = [jnp.exp(m_old[j] - m_new[j]) for j in hs]
        l_new = [l_ref[j] * corr[j] + jnp.sum(p[j], axis=0, keepdims=True) for j in hs]
        acc_new = [acc_ref[j] * corr[j] + pv[j] for j in hs]
        for j in hs:
            m_ref[j] = m_new[j]
            l_ref[j] = l_new[j]
            acc_ref[j] = acc_new[j]

    def below_diagonal(ki, carry):
        block(ki, False)
        return carry

    lax.fori_loop(0, qi, below_diagonal, 0)
    block(qi, True)
    for j in hs:
        o_ref[:, j * LANE:(j + 1) * LANE] = (acc_ref[j] / l_ref[j]).T.astype(o_ref.dtype)


def mla_prompt_attention(qt, k, vt, *, bsz, seq, tq, hp):
    nq = seq // tq
    kw = MLA_NOPE + LANE
    return pl.pallas_call(
        functools.partial(_attn_body, tq=tq, hp=hp),
        grid=(bsz, MLA_HEADS // hp, nq),
        in_specs=[
            pl.BlockSpec((hp, None, kw, tq), lambda b, h, qi: (h, b * nq + qi, 0, 0)),
            pl.BlockSpec((seq, hp * kw), lambda b, h, qi: (b, h)),
            pl.BlockSpec((hp, nq, MLA_V, tq), lambda b, h, qi: (h, b, 0, 0)),
        ],
        out_specs=pl.BlockSpec((tq, hp * LANE), lambda b, h, qi: (b * nq + qi, h)),
        out_shape=jax.ShapeDtypeStruct((bsz * seq, MLA_HEADS * MLA_V), BF16),
        scratch_shapes=[pltpu.VMEM((hp, 1, tq), F32), pltpu.VMEM((hp, 1, tq), F32),
                        pltpu.VMEM((hp, MLA_V, tq), F32)],
        compiler_params=_cparams(("parallel", "parallel", "arbitrary")),
        name="mla_prompt_attention",
    )(qt, k, vt)


def _paged_body(pt_ref, qn_ref, qr_ref, rows_ref, w_ref, *rest, gp, nb, n_valid):
    page_refs = rest[:nb * gp]
    o_ref, ql_ref, qrr_ref, m_ref, l_ref, acc_ref = rest[nb * gp:]
    del pt_ref
    g = pl.program_id(1)
    rows_q = MLA_HEADS * SAMPLE_PAD
    kvw = MLA_NOPE + MLA_V

    @pl.when(g == 0)
    def _():
        for sb in range(nb):
            tok = slice(sb * SAMPLE_PAD, (sb + 1) * SAMPLE_PAD)
            for h in range(MLA_HEADS):
                w_uk = w_ref[:, h * kvw: h * kvw + MLA_NOPE]
                ql = _dot_nt(qn_ref[tok, h * LANE:(h + 1) * LANE], w_uk)
                ql_ref[sb, h * SAMPLE_PAD:(h + 1) * SAMPLE_PAD, :] = ql.astype(BF16)
                qrr_ref[sb, h * SAMPLE_PAD:(h + 1) * SAMPLE_PAD, :] = qr_ref[tok, h * LANE:(h + 1) * LANE]
        m_ref[...] = jnp.full(m_ref.shape, -jnp.inf, F32)
        l_ref[...] = jnp.zeros(l_ref.shape, F32)
        acc_ref[...] = jnp.zeros(acc_ref.shape, F32)

    def update(scores, pvs):
        sbs = range(nb)
        m_old = [m_ref[sb] for sb in sbs]
        m_new = [jnp.maximum(m_old[sb], jnp.max(scores[sb], axis=-1, keepdims=True)) for sb in sbs]
        p = [jnp.exp(scores[sb] - m_new[sb]) for sb in sbs]
        pv = [pvs[sb](p[sb].astype(BF16)) for sb in sbs]
        corr = [jnp.exp(m_old[sb] - m_new[sb]) for sb in sbs]
        return [(m_new[sb], l_ref[sb] * corr[sb] + jnp.sum(p[sb], axis=-1, keepdims=True),
                 acc_ref[sb] * corr[sb] + pv[sb]) for sb in sbs]

    def commit(new):
        for sb, (m_new, l_new, acc_new) in enumerate(new):
            m_ref[sb] = m_new
            l_ref[sb] = l_new
            acc_ref[sb] = acc_new

    lat_t = [jnp.concatenate([pr[:MLA_KV_RANK, :].astype(BF16) for pr in page_refs[sb * gp:(sb + 1) * gp]], axis=1)
             for sb in range(nb)]
    rk_t = [jnp.concatenate([pr[MLA_KV_RANK:, :].astype(BF16) for pr in page_refs[sb * gp:(sb + 1) * gp]], axis=1)
            for sb in range(nb)]
    scores = [_dot(ql_ref[sb], lat_t[sb]) + _dot(qrr_ref[sb][:, :MLA_ROPE], rk_t[sb]) for sb in range(nb)]
    commit(update(scores, [lambda p, lt=lt: _dot_nt(p, lt) for lt in lat_t]))

    @pl.when(g == pl.num_programs(1) - 1)
    def _():
        tq = lax.broadcasted_iota(jnp.int32, (rows_q, SAMPLE_PAD), 0) % SAMPLE_PAD
        tk = lax.broadcasted_iota(jnp.int32, (rows_q, SAMPLE_PAD), 1)
        visible = (tk <= tq) & (tk < n_valid)
        scores, pvs = [], []
        for sb in range(nb):
            tok = slice(sb * SAMPLE_PAD, (sb + 1) * SAMPLE_PAD)
            lat = rows_ref[tok, :MLA_KV_RANK].astype(BF16)
            rk = rows_ref[tok, MLA_KV_RANK:].astype(BF16)
            s = _dot_nt(ql_ref[sb], lat) + _dot_nt(qrr_ref[sb][:, :MLA_ROPE], rk)
            scores.append(jnp.where(visible, s, -jnp.inf))
            pvs.append(lambda p, lat=lat: _dot(p, lat))
        for sb, (_, l_new, acc_new) in enumerate(update(scores, pvs)):
            tok = slice(sb * SAMPLE_PAD, (sb + 1) * SAMPLE_PAD)
            out_lat = (acc_new / l_new).astype(BF16)
            for h in range(MLA_HEADS):
                w_uv = w_ref[:, h * kvw + MLA_NOPE:(h + 1) * kvw]
                o = _dot(out_lat[h * SAMPLE_PAD:(h + 1) * SAMPLE_PAD, :], w_uv)
                o_ref[tok, h * LANE:(h + 1) * LANE] = o.astype(o_ref.dtype)


def mla_paged_attention(qn, qr, rows, w_ukv, cache_t, page_table, *, n_valid, gp, nb):
    bsz, n_pages = page_table.shape
    nw = MLA_HEADS * LANE
    npg = n_pages // gp
    rq = MLA_HEADS * SAMPLE_PAD

    def page_spec(sb, i):
        return pl.BlockSpec((None, ROW_W, PAGE_SIZE),
                            lambda b, g, pt: (pt[(b * nb + sb) * n_pages + g * gp + i], 0, 0))

    grid_spec = pltpu.PrefetchScalarGridSpec(
        num_scalar_prefetch=1,
        grid=(bsz // nb, npg),
        in_specs=[
            pl.BlockSpec((nb * SAMPLE_PAD, nw), lambda b, g, pt: (b, 0)),
            pl.BlockSpec((nb * SAMPLE_PAD, nw), lambda b, g, pt: (b, 0)),
            pl.BlockSpec((nb * SAMPLE_PAD, ROW_W), lambda b, g, pt: (b, 0)),
            pl.BlockSpec(w_ukv.shape, lambda b, g, pt: (0, 0)),
        ] + [page_spec(sb, i) for sb in range(nb) for i in range(gp)],
        out_specs=pl.BlockSpec((nb * SAMPLE_PAD, nw), lambda b, g, pt: (b, 0)),
        scratch_shapes=[
            pltpu.VMEM((nb, rq, MLA_KV_RANK), BF16),
            pltpu.VMEM((nb, rq, LANE), BF16),
            pltpu.VMEM((nb, rq, 1), F32),
            pltpu.VMEM((nb, rq, 1), F32),
            pltpu.VMEM((nb, rq, MLA_KV_RANK), F32),
        ],
    )
    return pl.pallas_call(
        functools.partial(_paged_body, gp=gp, nb=nb, n_valid=n_valid),
        grid_spec=grid_spec,
        out_shape=jax.ShapeDtypeStruct((bsz * SAMPLE_PAD, nw), BF16),
        compiler_params=_cparams(("parallel", "arbitrary")),
        name="mla_paged_attention",
    )(page_table.reshape(-1), qn, qr, rows, w_ukv, *([cache_t] * (nb * gp)))


def _mlstm_body(q_ref, k_ref, v_ref, og_ref, gc_ref, gr_ref, bc_ref, br_ref, ng_ref, c0_ref, n0_ref, m0_ref,
                h_ref, c_ref, n_ref, m_ref, *, lc, n_valid, nb, single_chunk):
    if single_chunk:
        cs_ref, ns_ref, ms_ref = c0_ref, n0_ref, m0_ref
        m_ref[...] = m0_ref[...]
    else:
        cs_ref, ns_ref, ms_ref = c_ref, n_ref, m_ref

        @pl.when(pl.program_id(1) == 0)
        def _():
            c_ref[...] = c0_ref[...]
            n_ref[...] = n0_ref[...]
            m_ref[...] = m0_ref[...]

    nh = ML_HEADS
    row = lax.broadcasted_iota(jnp.int32, (lc, lc), 0)
    col = lax.broadcasted_iota(jnp.int32, (lc, lc), 1)
    causal = col <= row
    tril = causal.astype(F32)
    triu = (row <= col).astype(F32)
    seqs = []
    for sb in range(nb):
        tok = slice(sb * lc, (sb + 1) * lc)
        gc = gc_ref[tok, :] + bc_ref[...]
        gr = gr_ref[sb] + br_ref[:, 0:1]
        li_c, lf_c = gc[:, 0:nh], _log_sigmoid(gc[:, nh:2 * nh])
        li_r, lf_r = gr[0:nh, :], _log_sigmoid(gr[nh:2 * nh, :])
        if n_valid < lc:
            vc = lax.broadcasted_iota(jnp.int32, (lc, nh), 0) < n_valid
            vr = lax.broadcasted_iota(jnp.int32, (nh, lc), 1) < n_valid
            li_c, lf_c = jnp.where(vc, li_c, NEG_BIG), jnp.where(vc, lf_c, 0.0)
            li_r, lf_r = jnp.where(vr, li_r, NEG_BIG), jnp.where(vr, lf_r, 0.0)
        seqs.append((li_c, li_r, lf_c, lf_r))
    b_cs = [_dot_sel_lhs(tril, s[2]) for s in seqs]
    b_rs = [_dot_sel_rhs(s[3], triu) for s in seqs]

    items = [(sb, h) for sb in range(nb) for h in range(nh)]
    st = []
    for sb, h in items:
        tok = slice(sb * lc, (sb + 1) * lc)
        qh = q_ref[tok, h * ML_DQK:(h + 1) * ML_DQK] * (ML_DQK ** -0.5)
        kh = k_ref[tok, h * ML_DQK:(h + 1) * ML_DQK]
        st.append(dict(
            bc=b_cs[sb][:, h:h + 1], br=b_rs[sb][h:h + 1, :],
            ic=seqs[sb][0][:, h:h + 1], ir=seqs[sb][1][h:h + 1, :],
            m_prev=ms_ref[sb, h:h + 1, 0:1], qh=qh, qb=qh.astype(BF16), kh=kh, kb=kh.astype(BF16),
            vh=v_ref[tok, h * ML_DV:(h + 1) * ML_DV], og=og_ref[tok, h * ML_DV:(h + 1) * ML_DV],
            c=cs_ref[sb, h], n=ns_ref[sb, h:h + 1, :], ng=ng_ref[:, h * ML_DV:(h + 1) * ML_DV]))
    for d in st:
        d["qk"] = _dot_nt(d["qb"], d["kb"])
        d["qc"] = _dot_nt(d["qb"], d["c"].astype(BF16))
    for d in st:
        dmat = jnp.where(causal, d["bc"] - d["br"] + d["ir"], -jnp.inf)
        inter = d["bc"] + d["m_prev"]
        d["m_t"] = jnp.maximum(inter, jnp.max(dmat, axis=-1, keepdims=True))
        d["w_inter"] = jnp.exp(inter - d["m_t"])
        d["s"] = d["qk"] * jnp.exp(dmat - d["m_t"])
        b_last = d["bc"][lc - 1:lc, :]
        g_c = b_last - d["bc"] + d["ic"]
        g_r = b_last - d["br"] + d["ir"]
        d["m_new"] = jnp.maximum(b_last + d["m_prev"], jnp.max(g_r, axis=-1, keepdims=True))
        d["decay"] = jnp.exp(b_last + d["m_prev"] - d["m_new"])
        d["wg"] = jnp.exp(g_c - d["m_new"])
    for d in st:
        d["sv"] = _dot(d["s"].astype(BF16), d["vh"].astype(BF16))
        d["vk"] = _dot_tn((d["vh"] * d["wg"]).astype(BF16), d["kb"])
    for d in st:
        num = d["sv"] + d["w_inter"] * d["qc"]
        den = (jnp.sum(d["s"], axis=-1, keepdims=True)
               + d["w_inter"] * jnp.sum(d["qh"] * d["n"], axis=-1, keepdims=True))
        hout = num / jnp.maximum(jnp.abs(den), jnp.exp(-d["m_t"]))
        d["h_new"] = (_sigmoid(d["og"]) * _rms(hout, d["ng"])).astype(h_ref.dtype)
        d["c_new"] = d["decay"] * d["c"] + d["vk"]
        d["n_new"] = d["decay"] * d["n"] + jnp.sum(d["kh"] * d["wg"], axis=0, keepdims=True)

    for (sb, h), d in zip(items, st):
        h_ref[sb * lc:(sb + 1) * lc, h * ML_DV:(h + 1) * ML_DV] = d["h_new"]
        c_ref[sb, h] = d["c_new"]
        n_ref[sb, h:h + 1, :] = d["n_new"]
        m_ref[sb, h:h + 1, :] = jnp.broadcast_to(d["m_new"], (1, LANE))


def mlstm(p, gates_rows, b_ig, b_fg, norm_g, c0, n0, m0, *, bsz, lc, nchunk, n_valid, nb):
    assert nb == 1 or nchunk == 1
    nh = ML_HEADS
    bias_c = jnp.concatenate([b_ig, b_fg, jnp.zeros((LANE - 2 * nh,), F32)]).reshape(1, LANE)
    bias_r = jnp.broadcast_to(jnp.concatenate([b_ig, b_fg]).reshape(2 * nh, 1), (2 * nh, LANE))
    tl = nb * lc

    def rows(cb):
        return lambda b, c: (b * nchunk + c, cb)

    const2 = lambda b, c: (0, 0)
    return pl.pallas_call(
        functools.partial(_mlstm_body, lc=lc, n_valid=n_valid, nb=nb, single_chunk=nchunk == 1),
        grid=(bsz // nb, nchunk),
        in_specs=[
            pl.BlockSpec((tl, nh * ML_DQK), rows(P_Q // (nh * ML_DQK))),
            pl.BlockSpec((tl, nh * ML_DQK), rows(P_K // (nh * ML_DQK))),
            pl.BlockSpec((tl, nh * ML_DV), rows(P_V // (nh * ML_DV))),
            pl.BlockSpec((tl, nh * ML_DV), rows(P_OG // (nh * ML_DV))),
            pl.BlockSpec((tl, LANE), rows(P_GATE // LANE)),
            pl.BlockSpec((nb, 2 * nh, lc), lambda b, c: (b * nchunk + c, 0, 0)),
            pl.BlockSpec((1, LANE), const2),
            pl.BlockSpec((2 * nh, LANE), const2),
            pl.BlockSpec((1, nh * ML_DV), const2),
            pl.BlockSpec((nb, nh, ML_DV, ML_DQK), lambda b, c: (b, 0, 0, 0)),
            pl.BlockSpec((nb, nh, ML_DQK), lambda b, c: (b, 0, 0)),
            pl.BlockSpec((nb, 2 * nh, LANE), lambda b, c: (b, 0, 0)),
        ],
        out_specs=[
            pl.BlockSpec((tl, nh * ML_DV), lambda b, c: (b * nchunk + c, 0)),
            pl.BlockSpec((nb, nh, ML_DV, ML_DQK), lambda b, c: (b, 0, 0, 0)),
            pl.BlockSpec((nb, nh, ML_DQK), lambda b, c: (b, 0, 0)),
            pl.BlockSpec((nb, 2 * nh, LANE), lambda b, c: (b, 0, 0)),
        ],
        out_shape=[
            jax.ShapeDtypeStruct((bsz * nchunk * lc, nh * ML_DV), BF16),
            jax.ShapeDtypeStruct((bsz, nh, ML_DV, ML_DQK), F32),
            jax.ShapeDtypeStruct((bsz, nh, ML_DQK), F32),
            jax.ShapeDtypeStruct((bsz, 2 * nh, LANE), F32),
        ],
        compiler_params=_cparams(("parallel", "arbitrary")),
        name="mlstm",
    )(p, p, p, p, p, gates_rows, bias_c, bias_r, norm_g.reshape(1, -1), c0, n0, m0)


def _gates_rows(p, lc):
    nrows = p.shape[0]
    g = p[:, P_GATE:P_GATE + 2 * ML_HEADS]
    return g.reshape(nrows // lc, lc, 2 * ML_HEADS).transpose(0, 2, 1)


def _m_pack(m):
    b = m.shape[0]
    mp = jnp.pad(m.astype(F32), ((0, 0), (0, 2 * ML_HEADS - m.shape[1])))
    return jnp.broadcast_to(mp[:, :, None], (b, 2 * ML_HEADS, LANE))


def _ssd_body(z_ref, x_ref, b_ref, c_ref, dtc_ref, dtr_ref, cwx_ref, cwb_ref, cwc_ref, cbx_ref, cbb_ref, cbc_ref,
              csx_ref, csb_ref, csc_ref, pc_ref, pr_ref, dsk_ref, ng_ref, h0_ref,
              y_ref, h_ref, bufx, bufb, bufc, *, lc, n_valid, gs, single_chunk):
    tail = SAMPLE_PAD
    cxs = [slice(gi * SSM_GW, (gi + 1) * SSM_GW) for gi in range(gs)]
    cns = [slice(gi * SSM_STATE, (gi + 1) * SSM_STATE) for gi in range(gs)]

    def init_history():
        for gi in range(gs):
            bufx[gi, 0:tail, :] = csx_ref[:, cxs[gi]]
            bufb[gi, 0:tail, :] = csb_ref[:, cns[gi]]
            bufc[gi, 0:tail, :] = csc_ref[:, cns[gi]]

    if single_chunk:
        init_history()
        hs_ref = h0_ref
    else:
        hs_ref = h_ref

        @pl.when(pl.program_id(2) == 0)
        def _():
            h_ref[...] = h0_ref[...]
            init_history()

    def conv(raw, buf, cw_ref, cb_ref, cols):
        buf[tail:tail + lc, :] = raw
        acc = cb_ref[:, cols] + cw_ref[SSM_CONV - 1:SSM_CONV, cols] * raw
        for j in range(SSM_CONV - 1):
            off = tail - (SSM_CONV - 1) + j
            acc = acc + cw_ref[j:j + 1, cols] * buf[off:off + lc, :]
        if not single_chunk:
            buf[0:tail, :] = raw[lc - tail:lc, :]
        return _silu(acc)

    convs = []
    for gi in range(gs):
        convs.append((conv(x_ref[:, cxs[gi]], bufx.at[gi], cwx_ref, cbx_ref, cxs[gi]),
                      conv(b_ref[:, cns[gi]], bufb.at[gi], cwb_ref, cbb_ref, cns[gi]),
                      conv(c_ref[:, cns[gi]], bufc.at[gi], cwc_ref, cbc_ref, cns[gi])))

    outs = _ssd_groups(
        convs,
        [dict(z=z_ref[:, cxs[gi]], dtc=dtc_ref[:, gi * SSM_HG:(gi + 1) * SSM_HG],
              dtr=dtr_ref[gi * SSM_HG:(gi + 1) * SSM_HG, :], pc=pc_ref[gi], pr=pr_ref[gi],
              dsk=dsk_ref[:, cxs[gi]], ng=ng_ref[:, cxs[gi]], h=hs_ref[gi]) for gi in range(gs)],
        lc=lc, n_valid=n_valid)

    for gi, (y, h_new) in enumerate(outs):
        y_ref[:, cxs[gi]] = y.astype(y_ref.dtype)
        h_ref[gi] = h_new


def _ssd_groups(convs, ins, *, lc, n_valid):
    hg = SSM_HG
    gs = range(len(ins))
    row = lax.broadcasted_iota(jnp.int32, (lc, lc), 0)
    col = lax.broadcasted_iota(jnp.int32, (lc, lc), 1)
    causal = col <= row
    tril = causal.astype(F32)
    triu = (row <= col).astype(F32)
    expand = (lax.broadcasted_iota(jnp.int32, (hg, SSM_GW), 1) // SSM_HEADDIM
              == lax.broadcasted_iota(jnp.int32, (hg, SSM_GW), 0)).astype(F32)
    lane_head = lax.broadcasted_iota(jnp.int32, (lc, LANE), 1) // SSM_HEADDIM

    dt_c, dt_r = [], []
    for g in gs:
        c = _softplus(ins[g]["dtc"] + ins[g]["pc"][0:1, 0:hg])
        r = _softplus(ins[g]["dtr"] + ins[g]["pr"][:, 0:1])
        if n_valid < lc:
            c = jnp.where(lax.broadcasted_iota(jnp.int32, (lc, hg), 0) < n_valid, c, 0.0)
            r = jnp.where(lax.broadcasted_iota(jnp.int32, (hg, lc), 1) < n_valid, r, 0.0)
        dt_c.append(c)
        dt_r.append(r)
    cum_c = [_dot_sel_lhs(tril, dt_c[g] * (-jnp.exp(ins[g]["pc"][1:2, 0:hg]))) for g in gs]
    cum_r = [_dot_sel_rhs(dt_r[g] * (-jnp.exp(ins[g]["pr"][:, 1:2])), triu) for g in gs]
    cb16 = [convs[g][2].astype(BF16) for g in gs]
    bb16 = [convs[g][1].astype(BF16) for g in gs]
    cbm = [_dot_nt(cb16[g], bb16[g]) for g in gs]
    yh = [_dot_nt(cb16[g], ins[g]["h"].astype(BF16)) for g in gs]
    e_in = [_dot_sel_rhs(jnp.exp(cum_c[g]), expand) for g in gs]
    e_st = [_dot_sel_rhs(jnp.exp(cum_c[g][lc - 1:lc, :] - cum_c[g]) * dt_c[g], expand) for g in gs]
    upd = [_dot_tn((convs[g][0] * e_st[g]).astype(BF16), bb16[g]) for g in gs]

    pieces = [[] for _ in gs]
    for pair in range(hg // 2):
        parts = [[] for _ in gs]
        for sub in range(2):
            j = 2 * pair + sub
            for g in gs:
                seg = jnp.where(causal, cum_c[g][:, j:j + 1] - cum_r[g][j:j + 1, :], -jnp.inf)
                w = cbm[g] * jnp.exp(seg) * dt_r[g][j:j + 1, :]
                xm = jnp.where(lane_head == sub, convs[g][0][:, pair * LANE:(pair + 1) * LANE], 0.0)
                parts[g].append(_dot(w.astype(BF16), xm.astype(BF16)))
        for g in gs:
            pieces[g].append(parts[g][0] + parts[g][1])

    outs = []
    for g in gs:
        xa, h_old = convs[g][0], ins[g]["h"]
        y = yh[g] * e_in[g] + jnp.concatenate(pieces[g], axis=1)
        dec_r = jnp.exp(cum_r[g][:, lc - 1:lc])
        h_new = jnp.concatenate(
            [dec_r[j:j + 1, :] * h_old[j * SSM_HEADDIM:(j + 1) * SSM_HEADDIM, :]
             + upd[g][j * SSM_HEADDIM:(j + 1) * SSM_HEADDIM, :] for j in range(hg)], axis=0)
        y = (y + ins[g]["dsk"] * xa) * _silu(ins[g]["z"])
        outs.append((_rms(y, ins[g]["ng"]), h_new))
    return outs


def ssd(zp, dt_rows, conv_w, conv_b, conv_state8, dt_bias, a_log, d_skip, norm_g, h0, *, bsz, lc, nchunk,
        n_valid, gs):
    assert gs == SSM_GROUPS
    g_, hg = SSM_GROUPS // gs, SSM_HG * gs
    gw, ns = SSM_GW * gs, SSM_STATE * gs

    def rows(off, width):
        assert off % width == 0
        return lambda b, g, c: (b * nchunk + c, off // width + g)

    ng_all, nh1 = SSM_GROUPS, SSM_HG
    pcol = jnp.zeros((ng_all, 8, LANE), F32)
    pcol = pcol.at[:, 0, :nh1].set(dt_bias.reshape(ng_all, nh1)).at[:, 1, :nh1].set(a_log.reshape(ng_all, nh1))
    prow = jnp.zeros((ng_all, nh1, LANE), F32)
    prow = prow.at[:, :, 0].set(dt_bias.reshape(ng_all, nh1)).at[:, :, 1].set(a_log.reshape(ng_all, nh1))
    dsk = jnp.repeat(d_skip, SSM_HEADDIM).reshape(1, SSM_D_INNER)
    cb2 = conv_b.reshape(1, -1)
    bo, co = SSM_D_INNER // ns, SSM_D_INNER // ns + g_
    assert SSM_D_INNER % ns == 0

    in_specs = [
        pl.BlockSpec((lc, gw), rows(0, gw)),
        pl.BlockSpec((lc, gw), rows(Z_X, gw)),
        pl.BlockSpec((lc, ns), rows(Z_B, ns)),
        pl.BlockSpec((lc, ns), rows(Z_C, ns)),
        pl.BlockSpec((lc, LANE), rows(Z_DT, LANE)),
        pl.BlockSpec((None, hg, lc), lambda b, g, c: (b * nchunk + c, g, 0)),
        pl.BlockSpec((SSM_CONV, gw), lambda b, g, c: (0, g)),
        pl.BlockSpec((SSM_CONV, ns), lambda b, g, c: (0, bo + g)),
        pl.BlockSpec((SSM_CONV, ns), lambda b, g, c: (0, co + g)),
        pl.BlockSpec((1, gw), lambda b, g, c: (0, g)),
        pl.BlockSpec((1, ns), lambda b, g, c: (0, bo + g)),
        pl.BlockSpec((1, ns), lambda b, g, c: (0, co + g)),
        pl.BlockSpec((None, 8, gw), lambda b, g, c: (b, 0, g)),
        pl.BlockSpec((None, 8, ns), lambda b, g, c: (b, 0, bo + g)),
        pl.BlockSpec((None, 8, ns), lambda b, g, c: (b, 0, co + g)),
        pl.BlockSpec((gs, 8, LANE), lambda b, g, c: (g, 0, 0)),
        pl.BlockSpec((gs, nh1, LANE), lambda b, g, c: (g, 0, 0)),
        pl.BlockSpec((1, gw), lambda b, g, c: (0, g)),
        pl.BlockSpec((1, gw), lambda b, g, c: (0, g)),
        pl.BlockSpec((None, gs, SSM_GW, SSM_STATE), lambda b, g, c: (b, g, 0, 0)),
    ]
    return pl.pallas_call(
        functools.partial(_ssd_body, lc=lc, n_valid=n_valid, gs=gs, single_chunk=nchunk == 1),
        grid=(bsz, g_, nchunk),
        in_specs=in_specs,
        out_specs=[
            pl.BlockSpec((lc, gw), lambda b, g, c: (b * nchunk + c, g)),
            pl.BlockSpec((None, gs, SSM_GW, SSM_STATE), lambda b, g, c: (b, g, 0, 0)),
        ],
        out_shape=[
            jax.ShapeDtypeStruct((bsz * nchunk * lc, SSM_D_INNER), BF16),
            jax.ShapeDtypeStruct((bsz, SSM_GROUPS, SSM_GW, SSM_STATE), F32),
        ],
        scratch_shapes=[
            pltpu.VMEM((gs, lc + SAMPLE_PAD, SSM_GW), F32),
            pltpu.VMEM((gs, lc + SAMPLE_PAD, SSM_STATE), F32),
            pltpu.VMEM((gs, lc + SAMPLE_PAD, SSM_STATE), F32),
        ],
        compiler_params=_cparams(("parallel", "parallel", "arbitrary")),
        name="ssd",
    )(zp, zp, zp, zp, zp, dt_rows, conv_w, conv_w, conv_w, cb2, cb2, cb2, conv_state8, conv_state8, conv_state8,
      pcol, prow, dsk, norm_g.reshape(1, -1), h0)


def _dt_rows(zp, lc):
    nrows = zp.shape[0]
    return zp[:, Z_DT:Z_DT + SSM_HEADS].reshape(nrows // lc, lc, SSM_HEADS).transpose(0, 2, 1)


def _conv_state8(state):
    return jnp.pad(state, ((0, 0), (SAMPLE_PAD - (SSM_CONV - 1), 0), (0, 0)))


def _trunk(x_prompt, x_sample, cache_mla, state_mlstm_C, state_mlstm_n, state_mlstm_m, state_ssm, state_conv,
           page_table, norm_g, ffn_w_gate, ffn_w_up, ffn_w_down, w_in_even, b_ig, b_fg, mlstm_norm_g, mla_q_norm_g,
           mla_kv_norm_g, w_uq, w_ukv, w_out_even, w_in_ssm, conv_w, conv_b, dt_bias, A_log, D_skip, ssm_norm_g,
           w_out_ssm, final_norm_g, *, tm, tf, tf_s, tn, tn_ie, tn_oe, tn_ssm, tmp, lc_ml, nb_ml, lc_ssd, gs_p, gs_s, tq, hp, gp, nb_pg):
    bp, sp, d = x_prompt.shape
    bs, ts, _ = x_sample.shape
    tp = bp * sp
    tsp = bs * SAMPLE_PAD
    xs = [x_prompt.reshape(tp, d), x_sample.reshape(bs * ts, d)]
    tms = [min(tm, tp), min(tm, bs * ts)]
    tmps = [min(tmp, tp), min(tmp, tsp)]
    assert tmps[0] == tq
    tables = [_rope_tables(jnp.tile(jnp.arange(sp), bp)),
              _rope_tables(jnp.tile(PAST_LEN + jnp.arange(SAMPLE_PAD), bs))]

    def pad_seq(a):
        return jnp.pad(a.reshape(bs, ts, a.shape[1]), ((0, 0), (0, SAMPLE_PAD - ts), (0, 0))).reshape(tsp, a.shape[1])

    def unpad_seq(a):
        return a.reshape(bs, SAMPLE_PAD, a.shape[1])[:, :ts].reshape(bs * ts, a.shape[1])

    def ff(xs, g, layer, which, final_g=None):
        return [ffn(x, g, ffn_w_gate, ffn_w_up, ffn_w_down, layer, which, tm=t, tf=f, final_g=final_g)
                for x, t, f in zip(xs, tms, (tf, tf_s))]

    xs = ff(xs, norm_g[0, 0], 0, 0)
    w_in = _prep_w_in_even(w_in_even)
    p_p, p_s = [norm_matmul(x, norm_g[0, 1], w_in, tm=t, tn=tn_ie) for x, t in zip(xs, tms)]
    p_s = pad_seq(p_s)

    zeros = functools.partial(jnp.zeros, dtype=F32)
    hm_p, c_p, n_p, m_p = mlstm(
        p_p, _gates_rows(p_p, lc_ml), b_ig, b_fg, mlstm_norm_g,
        zeros((bp, ML_HEADS, ML_DV, ML_DQK)), zeros((bp, ML_HEADS, ML_DQK)), zeros((bp, 2 * ML_HEADS, LANE)),
        bsz=bp, lc=lc_ml, nchunk=sp // lc_ml, n_valid=lc_ml, nb=1)
    hm_s, c_s, n_s, m_s = mlstm(
        p_s, _gates_rows(p_s, SAMPLE_PAD), b_ig, b_fg, mlstm_norm_g,
        state_mlstm_C, state_mlstm_n, _m_pack(state_mlstm_m),
        bsz=bs, lc=SAMPLE_PAD, nchunk=1, n_valid=ts, nb=nb_ml)

    w_ukv16 = w_ukv.astype(BF16)
    w_q = _prep_w_uq(w_uq)
    (qt_p,) = q_prep(p_p, mla_q_norm_g, w_q, *tables[0], tm=tmps[0], transposed=True)
    qn_s, qr_s = q_prep(p_s, mla_q_norm_g, w_q, *tables[1], tm=tmps[1])
    rows_p, k_p, vt_p = kv_prep(p_p, mla_kv_norm_g, *tables[0], tm=tmps[0], w_ukv=w_ukv16)
    (rows_s,) = kv_prep(p_s, mla_kv_norm_g, *tables[1], tm=tmps[1])
    ha_p = mla_prompt_attention(qt_p, k_p, vt_p, bsz=bp, seq=sp, tq=tq, hp=hp)
    cache_t = jnp.swapaxes(cache_mla, 1, 2)
    ha_s = mla_paged_attention(qn_s, qr_s, rows_s, w_ukv16, cache_t, page_table, n_valid=ts, gp=gp, nb=nb_pg)

    w_oe = w_out_even.astype(BF16)
    xs = [matmul_residual([hm, ha], w_oe, x, tm=t, tn=tn_oe)
          for hm, ha, x, t in zip((hm_p, unpad_seq(hm_s)), (ha_p, unpad_seq(ha_s)), xs, tms)]
    xs = ff(xs, norm_g[0, 2], 0, 1)

    xs = ff(xs, norm_g[1, 0], 1, 0)
    w_is = _prep_w_in_ssm(w_in_ssm)
    z_p, z_s4 = [norm_matmul(x, norm_g[1, 1], w_is, tm=t, tn=tn_ssm) for x, t in zip(xs, tms)]
    z_s = pad_seq(z_s4)
    y_p, ssm_p = ssd(
        z_p, _dt_rows(z_p, lc_ssd), conv_w, conv_b, zeros((bp, SAMPLE_PAD, SSM_CONV_DIM)), dt_bias, A_log,
        D_skip, ssm_norm_g, zeros((bp, SSM_GROUPS, SSM_GW, SSM_STATE)),
        bsz=bp, lc=lc_ssd, nchunk=sp // lc_ssd, n_valid=lc_ssd, gs=gs_p)
    y_s, ssm_s = ssd(
        z_s, _dt_rows(z_s, SAMPLE_PAD), conv_w, conv_b, _conv_state8(state_conv), dt_bias, A_log,
        D_skip, ssm_norm_g, state_ssm.reshape(bs, SSM_GROUPS, SSM_GW, SSM_STATE),
        bsz=bs, lc=SAMPLE_PAD, nchunk=1, n_valid=ts, gs=gs_s)
    w_os = w_out_ssm.astype(BF16)
    xs = [matmul_residual([y], w_os, x, tm=t, tn=tn) for y, x, t in zip((y_p, unpad_seq(y_s)), xs, tms)]
    out_p, out_s = ff(xs, norm_g[1, 2], 1, 1, final_g=final_norm_g)

    keep = SSM_CONV - 1
    assert sp >= keep and ts >= keep
    conv_p = jnp.stack([lax.slice(z_p, (b * sp + sp - keep, Z_X), (b * sp + sp, Z_DT)) for b in range(bp)])
    conv_s = z_s4[:, Z_X:Z_DT].reshape(bs, ts, SSM_CONV_DIM)[:, ts - keep:ts]
    hshape = (SSM_HEADS, SSM_HEADDIM, SSM_STATE)
    return (out_p.reshape(bp, sp, d), out_s.reshape(bs, ts, d), rows_p.reshape(bp, sp, ROW_W),
            rows_s.reshape(bs, SAMPLE_PAD, ROW_W)[:, :ts],
            c_p, c_s, n_p, n_s, m_p[:, :ML_HEADS, 0], m_s[:, :ML_HEADS, 0],
            ssm_p.reshape((bp,) + hshape), ssm_s.reshape((bs,) + hshape), conv_p, conv_s)


def kernel(x_prompt, x_sample, cache_mla, state_mlstm_C, state_mlstm_n, state_mlstm_m, state_ssm, state_conv, page_table, norm_g, ffn_w_gate, ffn_w_up, ffn_w_down, w_in_even, b_ig, b_fg, mlstm_norm_g, mla_q_norm_g, mla_kv_norm_g, w_uq, w_ukv, w_out_even, w_in_ssm, conv_w, conv_b, dt_bias, A_log, D_skip, ssm_norm_g, w_out_ssm, final_norm_g):
    return _trunk(x_prompt, x_sample, cache_mla, state_mlstm_C, state_mlstm_n, state_mlstm_m, state_ssm, state_conv,
                  page_table, norm_g, ffn_w_gate, ffn_w_up, ffn_w_down, w_in_even, b_ig, b_fg, mlstm_norm_g,
                  mla_q_norm_g, mla_kv_norm_g, w_uq, w_ukv, w_out_even, w_in_ssm, conv_w, conv_b, dt_bias, A_log,
                  D_skip, ssm_norm_g, w_out_ssm, final_norm_g,
                  tm=1024, tf=256, tf_s=512, tn=512, tn_ie=1536, tn_oe=1024, tn_ssm=1152, tmp=512, lc_ml=256, nb_ml=4, lc_ssd=128, gs_p=8, gs_s=8,
                  tq=512, hp=4, gp=32, nb_pg=2)
```

```python
import functools

import jax
import jax.numpy as jnp
from jax import lax
from jax.experimental import pallas as pl
from jax.experimental.pallas import tpu as pltpu

F32 = jnp.float32
BF16 = jnp.bfloat16

EPS = 1e-6
PAST_LEN = 8192
PAGE_SIZE = 128

ML_HEADS = 4
ML_DQK = 128
ML_DV = 256

MLA_HEADS = 8
MLA_NOPE = 128
MLA_ROPE = 64
MLA_V = 128
MLA_Q_RANK = 512
MLA_KV_RANK = 512
MLA_SCALE = (MLA_NOPE + MLA_ROPE) ** -0.5
ROPE_BASE = 10000.0
ROW_W = MLA_KV_RANK + MLA_ROPE

SSM_D_INNER = 4096
SSM_HEADDIM = 64
SSM_HEADS = 64
SSM_GROUPS = 8
SSM_HG = SSM_HEADS // SSM_GROUPS
SSM_GW = SSM_D_INNER // SSM_GROUPS
SSM_STATE = 128
SSM_CONV = 4
SSM_CONV_DIM = SSM_D_INNER + 2 * SSM_GROUPS * SSM_STATE

SAMPLE_PAD = 8
LANE = 128
VMEM_LIMIT = 56 * 1024 * 1024

P_Q, P_K, P_V, P_OG, P_CQ, P_CKV = 0, 512, 1024, 2048, 3072, 3584
P_KR, P_KRSW, P_GATE, P_WIDTH = 4096, 4224, 4352, 4608
Z_X, Z_B, Z_C, Z_DT, Z_WIDTH = 4096, 8192, 9216, 10240, 10368

NEG_BIG = -1e30


def _cparams(sem):
    return pltpu.CompilerParams(dimension_semantics=sem, vmem_limit_bytes=VMEM_LIMIT)


def _rms(x, g):
    return x * lax.rsqrt(jnp.mean(x * x, axis=-1, keepdims=True) + EPS) * g


def _sigmoid(x):
    return 1.0 / (1.0 + jnp.exp(-x))


def _silu(x):
    return x * _sigmoid(x)


def _log_sigmoid(x):
    return jnp.minimum(x, 0.0) - jnp.log1p(jnp.exp(-jnp.abs(x)))


def _softplus(x):
    return jnp.maximum(x, 0.0) + jnp.log1p(jnp.exp(-jnp.abs(x)))


def _dot(a, b):
    return jnp.dot(a, b, preferred_element_type=F32)


def _dot_nt(a, b):
    return lax.dot_general(a, b, (((1,), (1,)), ((), ())), preferred_element_type=F32)


def _dot_tn(a, b):
    return lax.dot_general(a, b, (((0,), (0,)), ((), ())), preferred_element_type=F32)


def _split3(x):
    x1 = x.astype(BF16)
    r1 = x - x1.astype(F32)
    x2 = r1.astype(BF16)
    x3 = (r1 - x2.astype(F32)).astype(BF16)
    return x1, x2, x3


def _dot_sel_lhs(sel, x):
    s = sel.astype(BF16)
    x1, x2, x3 = _split3(x)
    return (_dot(s, x3) + _dot(s, x2)) + _dot(s, x1)


def _dot_sel_rhs(x, sel):
    s = sel.astype(BF16)
    x1, x2, x3 = _split3(x)
    return (_dot(x3, s) + _dot(x2, s)) + _dot(x1, s)


def _ffn_body(x_ref, g_ref, wg_ref, wu_ref, wd_ref, *rest, final_norm):
    if final_norm:
        gf_ref, o_ref, xn_ref = rest
    else:
        o_ref, xn_ref = rest
    j = pl.program_id(1)

    @pl.when(j == 0)
    def _():
        x = x_ref[...]
        xn_ref[...] = _rms(x, g_ref[...]).astype(BF16)
        o_ref[...] = x

    xn = xn_ref[...]
    hg = _dot(xn, wg_ref[...].astype(BF16))
    hu = _dot(xn, wu_ref[...].astype(BF16))
    h = _silu(hg) * (0.5 * hu)
    o_ref[...] += _dot(h.astype(BF16), wd_ref[...].astype(BF16))

    if final_norm:
        @pl.when(j == pl.num_programs(1) - 1)
        def _():
            o_ref[...] = _rms(o_ref[...], gf_ref[...])


def ffn(x, g, w_gate, w_up, w_down, layer, which, *, tm, tf, final_g=None):
    t, d = x.shape
    dff = w_gate.shape[-1]
    final_norm = final_g is not None
    in_specs = [
        pl.BlockSpec((tm, d), lambda i, j: (i, 0)),
        pl.BlockSpec((1, d), lambda i, j: (0, 0)),
        pl.BlockSpec((None, None, d, tf), lambda i, j: (layer, which, 0, j)),
        pl.BlockSpec((None, None, d, tf), lambda i, j: (layer, which, 0, j)),
        pl.BlockSpec((None, None, tf, d), lambda i, j: (layer, which, j, 0)),
    ]
    args = [x, g.reshape(1, d), w_gate, w_up, w_down]
    if final_norm:
        in_specs.append(pl.BlockSpec((1, d), lambda i, j: (0, 0)))
        args.append(final_g.reshape(1, d))
    return pl.pallas_call(
        functools.partial(_ffn_body, final_norm=final_norm),
        grid=(t // tm, dff // tf),
        in_specs=in_specs,
        out_specs=pl.BlockSpec((tm, d), lambda i, j: (i, 0)),
        out_shape=jax.ShapeDtypeStruct((t, d), F32),
        scratch_shapes=[pltpu.VMEM((tm, d), BF16)],
        compiler_params=_cparams(("parallel", "arbitrary")),
        name="ffn",
    )(*args)


def _norm_mm_body(x_ref, g_ref, w_ref, o_ref, xn_ref):
    @pl.when(pl.program_id(1) == 0)
    def _():
        xn_ref[...] = _rms(x_ref[...], g_ref[...]).astype(BF16)

    o_ref[...] = _dot(xn_ref[...], w_ref[...].astype(BF16))


def norm_matmul(x, g, w, *, tm, tn):
    t, d = x.shape
    n = w.shape[1]
    return pl.pallas_call(
        _norm_mm_body,
        grid=(t // tm, n // tn),
        in_specs=[
            pl.BlockSpec((tm, d), lambda i, j: (i, 0)),
            pl.BlockSpec((1, d), lambda i, j: (0, 0)),
            pl.BlockSpec((d, tn), lambda i, j: (0, j)),
        ],
        out_specs=pl.BlockSpec((tm, tn), lambda i, j: (i, j)),
        out_shape=jax.ShapeDtypeStruct((t, n), F32),
        scratch_shapes=[pltpu.VMEM((tm, d), BF16)],
        compiler_params=_cparams(("parallel", "arbitrary")),
        name="norm_matmul",
    )(x, g.reshape(1, d), w)


def _mm_res_body(*refs, n_a):
    a_refs, w_refs = refs[:n_a], refs[n_a:2 * n_a]
    res_ref, o_ref = refs[2 * n_a], refs[2 * n_a + 1]
    acc = res_ref[...]
    for a_ref, w_ref in zip(a_refs, w_refs):
        acc = acc + _dot(a_ref[...].astype(BF16), w_ref[...].astype(BF16))
    o_ref[...] = acc


def matmul_residual(a_list, w, res, *, tm, tn):
    n_a = len(a_list)
    t, ka = a_list[0].shape
    n = w.shape[1]
    in_specs = [pl.BlockSpec((tm, ka), lambda i, j: (i, 0)) for _ in a_list]
    in_specs += [pl.BlockSpec((ka, tn), lambda i, j, s=s: (s, j)) for s in range(n_a)]
    in_specs.append(pl.BlockSpec((tm, tn), lambda i, j: (i, j)))
    return pl.pallas_call(
        functools.partial(_mm_res_body, n_a=n_a),
        grid=(t // tm, n // tn),
        in_specs=in_specs,
        out_specs=pl.BlockSpec((tm, tn), lambda i, j: (i, j)),
        out_shape=jax.ShapeDtypeStruct((t, n), F32),
        compiler_params=_cparams(("parallel", "parallel")),
        name="matmul_residual",
    )(*a_list, *([w] * n_a), res)


def _swap_halves(w):
    half = w.shape[-1] // 2
    return jnp.concatenate([w[..., half:], w[..., :half]], axis=-1)


def _prep_w_in_even(w):
    d = w.shape[0]
    sizes = (512, 512, 1024, 4, 4, 1024, MLA_Q_RANK, MLA_KV_RANK, MLA_ROPE)
    idx = [0]
    for s in sizes:
        idx.append(idx[-1] + s)
    q, k, v, ig, fg, og, cq, ckv, kr = [w[:, idx[i]:idx[i + 1]] for i in range(len(sizes))]
    z64 = jnp.zeros((d, LANE - MLA_ROPE), w.dtype)
    gates = jnp.concatenate([ig, fg, jnp.zeros((d, LANE - 2 * ML_HEADS), w.dtype)], axis=1)
    pad = jnp.zeros((d, P_WIDTH - P_GATE - LANE), w.dtype)
    return jnp.concatenate([q, k, v, og, cq, ckv, kr, z64, _swap_halves(kr), z64, gates, pad], axis=1).astype(BF16)


def _prep_w_uq(w):
    r = w.shape[0]
    w3 = w.reshape(r, MLA_HEADS, MLA_NOPE + MLA_ROPE)
    nope = w3[..., :MLA_NOPE].reshape(r, MLA_HEADS * MLA_NOPE)
    rope = w3[..., MLA_NOPE:]
    zpad = jnp.zeros((r, MLA_HEADS, LANE - MLA_ROPE), w.dtype)
    ra = jnp.concatenate([rope, zpad], axis=-1).reshape(r, MLA_HEADS * LANE)
    rb = jnp.concatenate([_swap_halves(rope), zpad], axis=-1).reshape(r, MLA_HEADS * LANE)
    return jnp.concatenate([nope, ra, rb], axis=1).astype(BF16)


def _prep_w_in_ssm(w):
    return jnp.pad(w, ((0, 0), (0, Z_WIDTH - w.shape[1]))).astype(BF16)


def _rope_tables(pos):
    half = MLA_ROPE // 2
    inv = ROPE_BASE ** (-jnp.arange(half, dtype=F32) / half)
    ang = pos.astype(F32)[:, None] * inv[None, :]
    c, s = jnp.cos(ang), jnp.sin(ang)
    z = jnp.zeros((pos.shape[0], LANE - MLA_ROPE), F32)
    return jnp.concatenate([c, c, z], axis=1), jnp.concatenate([-s, s, z], axis=1)


def _q_prep_body(cq_ref, g_ref, w_ref, cos_ref, sin_ref, *outs, transposed):
    cn = _rms(cq_ref[...], g_ref[...]).astype(BF16)
    a = _dot(cn, w_ref[...])
    nw = MLA_HEADS * MLA_NOPE
    cos, sin = cos_ref[...], sin_ref[...]
    for h in range(MLA_HEADS):
        qn = a[:, h * LANE:(h + 1) * LANE] * MLA_SCALE
        ra = a[:, nw + h * LANE: nw + (h + 1) * LANE]
        rb = a[:, 2 * nw + h * LANE: 2 * nw + (h + 1) * LANE]
        qr = (ra * cos + rb * sin) * MLA_SCALE
        if transposed:
            (qt_ref,) = outs
            qt_ref[h, 0:MLA_NOPE, :] = qn.T.astype(BF16)
            qt_ref[h, MLA_NOPE:MLA_NOPE + LANE, :] = qr.T.astype(BF16)
        else:
            qn_ref, qr_ref = outs
            qn_ref[:, h * LANE:(h + 1) * LANE] = qn.astype(BF16)
            qr_ref[:, h * LANE:(h + 1) * LANE] = qr.astype(BF16)


def q_prep(p, g, wq, cos, sin, *, tm, transposed=False):
    t = p.shape[0]
    nw = MLA_HEADS * LANE
    if transposed:
        qw = MLA_NOPE + LANE
        out_specs = [pl.BlockSpec((MLA_HEADS, None, qw, tm), lambda i: (0, i, 0, 0))]
        out_shape = [jax.ShapeDtypeStruct((MLA_HEADS, t // tm, qw, tm), BF16)]
    else:
        out_specs = [pl.BlockSpec((tm, nw), lambda i: (i, 0)), pl.BlockSpec((tm, nw), lambda i: (i, 0))]
        out_shape = [jax.ShapeDtypeStruct((t, nw), BF16), jax.ShapeDtypeStruct((t, nw), BF16)]
    return pl.pallas_call(
        functools.partial(_q_prep_body, transposed=transposed),
        grid=(t // tm,),
        in_specs=[
            pl.BlockSpec((tm, MLA_Q_RANK), lambda i: (i, P_CQ // MLA_Q_RANK)),
            pl.BlockSpec((1, MLA_Q_RANK), lambda i: (0, 0)),
            pl.BlockSpec(wq.shape, lambda i: (0, 0)),
            pl.BlockSpec((tm, LANE), lambda i: (i, 0)),
            pl.BlockSpec((tm, LANE), lambda i: (i, 0)),
        ],
        out_specs=out_specs,
        out_shape=out_shape,
        compiler_params=_cparams(("parallel",)),
        name="q_prep",
    )(p, g.reshape(1, -1), wq, cos, sin)


def _kv_prep_body(ckv_ref, kr_ref, krsw_ref, g_ref, cos_ref, sin_ref, *rest, expand):
    cn = _rms(ckv_ref[...], g_ref[...])
    kr = kr_ref[...] * cos_ref[...] + krsw_ref[...] * sin_ref[...]
    if expand:
        w_ref, rows_ref, k_ref, vt_ref = rest
        kvw = MLA_NOPE + MLA_V
        kv = _dot(cn.astype(BF16), w_ref[...])
        krb = kr.astype(BF16)
        for h in range(MLA_HEADS):
            k_ref[:, h * kvw:h * kvw + MLA_NOPE] = kv[:, h * kvw:h * kvw + MLA_NOPE].astype(BF16)
            k_ref[:, h * kvw + MLA_NOPE:(h + 1) * kvw] = krb
            vt_ref[h] = kv[:, h * kvw + MLA_NOPE:(h + 1) * kvw].T.astype(BF16)
    else:
        (rows_ref,) = rest
    rows_ref[:, :MLA_KV_RANK] = cn
    rows_ref[:, MLA_KV_RANK:] = kr[:, :MLA_ROPE]


def kv_prep(p, g, cos, sin, *, tm, w_ukv=None):
    t = p.shape[0]
    expand = w_ukv is not None
    in_specs = [
        pl.BlockSpec((tm, MLA_KV_RANK), lambda i: (i, P_CKV // MLA_KV_RANK)),
        pl.BlockSpec((tm, LANE), lambda i: (i, P_KR // LANE)),
        pl.BlockSpec((tm, LANE), lambda i: (i, P_KRSW // LANE)),
        pl.BlockSpec((1, MLA_KV_RANK), lambda i: (0, 0)),
        pl.BlockSpec((tm, LANE), lambda i: (i, 0)),
        pl.BlockSpec((tm, LANE), lambda i: (i, 0)),
    ]
    args = [p, p, p, g.reshape(1, -1), cos, sin]
    out_specs = [pl.BlockSpec((tm, ROW_W), lambda i: (i, 0))]
    out_shape = [jax.ShapeDtypeStruct((t, ROW_W), F32)]
    if expand:
        nkv = w_ukv.shape[1]
        in_specs.append(pl.BlockSpec(w_ukv.shape, lambda i: (0, 0)))
        args.append(w_ukv)
        out_specs += [pl.BlockSpec((tm, nkv), lambda i: (i, 0)),
                      pl.BlockSpec((MLA_HEADS, None, MLA_V, tm), lambda i: (0, i, 0, 0))]
        out_shape += [jax.ShapeDtypeStruct((t, nkv), BF16),
                      jax.ShapeDtypeStruct((MLA_HEADS, t // tm, MLA_V, tm), BF16)]
    return pl.pallas_call(
        functools.partial(_kv_prep_body, expand=expand),
        grid=(t // tm,),
        in_specs=in_specs,
        out_specs=out_specs,
        out_shape=out_shape,
        compiler_params=_cparams(("parallel",)),
        name="kv_prep",
    )(*args)


def _attn_body(qt_ref, k_ref, vt_ref, o_ref, m_ref, l_ref, acc_ref, *, tq, hp):
    qi = pl.program_id(2)
    m_ref[...] = jnp.full(m_ref.shape, -jnp.inf, F32)
    l_ref[...] = jnp.zeros(l_ref.shape, F32)
    acc_ref[...] = jnp.zeros(acc_ref.shape, F32)
    kw = MLA_NOPE + LANE
    hs = range(hp)

    def block(ki, on_diagonal):
        rows = pl.ds(pl.multiple_of(ki * tq, tq), tq)
        s = [_dot(k_ref[rows, j * kw:(j + 1) * kw], qt_ref[j]) for j in hs]
        if on_diagonal:
            visible = (lax.broadcasted_iota(jnp.int32, (tq, tq), 0) <= lax.broadcasted_iota(jnp.int32, (tq, tq), 1))
            s = [jnp.where(visible, sj, -jnp.inf) for sj in s]
        m_old = [m_ref[j] for j in hs]
        m_new = [jnp.maximum(m_old[j], jnp.max(s[j], axis=0, keepdims=True)) for j in hs]
        p = [jnp.exp(s[j] - m_new[j]) for j in hs]
        pv = [_dot(vt_ref[j, ki], p[j].astype(BF16)) for j in hs]
        corr = [jnp.exp(m_old[j] - m_new[j]) for j in hs]
        l_new = [l_ref[j] * corr[j] + jnp.sum(p[j], axis=0, keepdims=True) for j in hs]
        acc_new = [acc_ref[j] * corr[j] + pv[j] for j in hs]
        for j in hs:
            m_ref[j] = m_new[j]
            l_ref[j] = l_new[j]
            acc_ref[j] = acc_new[j]

    def below_diagonal(ki, carry):
        block(ki, False)
        return carry

    lax.fori_loop(0, qi, below_diagonal, 0)
    block(qi, True)
    for j in hs:
        o_ref[:, j * LANE:(j + 1) * LANE] = (acc_ref[j] / l_ref[j]).T.astype(o_ref.dtype)


def mla_prompt_attention(qt, k, vt, *, bsz, seq, tq, hp):
    nq = seq // tq
    kw = MLA_NOPE + LANE
    return pl.pallas_call(
        functools.partial(_attn_body, tq=tq, hp=hp),
        grid=(bsz, MLA_HEADS // hp, nq),
        in_specs=[
            pl.BlockSpec((hp, None, kw, tq), lambda b, h, qi: (h, b * nq + qi, 0, 0)),
            pl.BlockSpec((seq, hp * kw), lambda b, h, qi: (b, h)),
            pl.BlockSpec((hp, nq, MLA_V, tq), lambda b, h, qi: (h, b, 0, 0)),
        ],
        out_specs=pl.BlockSpec((tq, hp * LANE), lambda b, h, qi: (b * nq + qi, h)),
        out_shape=jax.ShapeDtypeStruct((bsz * seq, MLA_HEADS * MLA_V), BF16),
        scratch_shapes=[pltpu.VMEM((hp, 1, tq), F32), pltpu.VMEM((hp, 1, tq), F32),
                        pltpu.VMEM((hp, MLA_V, tq), F32)],
        compiler_params=_cparams(("parallel", "parallel", "arbitrary")),
        name="mla_prompt_attention",
    )(qt, k, vt)


def _paged_body(pt_ref, qn_ref, qr_ref, rows_ref, w_ref, *rest, gp, nb, n_valid):
    page_refs = rest[:nb * gp]
    o_ref, ql_ref, qrr_ref, m_ref, l_ref, acc_ref = rest[nb * gp:]
    del pt_ref
    g = pl.program_id(1)
    rows_q = MLA_HEADS * SAMPLE_PAD
    kvw = MLA_NOPE + MLA_V

    @pl.when(g == 0)
    def _():
        for sb in range(nb):
            tok = slice(sb * SAMPLE_PAD, (sb + 1) * SAMPLE_PAD)
            for h in range(MLA_HEADS):
                w_uk = w_ref[:, h * kvw: h * kvw + MLA_NOPE]
                ql = _dot_nt(qn_ref[tok, h * LANE:(h + 1) * LANE], w_uk)
                ql_ref[sb, h * SAMPLE_PAD:(h + 1) * SAMPLE_PAD, :] = ql.astype(BF16)
                qrr_ref[sb, h * SAMPLE_PAD:(h + 1) * SAMPLE_PAD, :] = qr_ref[tok, h * LANE:(h + 1) * LANE]
        m_ref[...] = jnp.full(m_ref.shape, -jnp.inf, F32)
        l_ref[...] = jnp.zeros(l_ref.shape, F32)
        acc_ref[...] = jnp.zeros(acc_ref.shape, F32)

    def update(scores, pvs):
        sbs = range(nb)
        m_old = [m_ref[sb] for sb in sbs]
        m_new = [jnp.maximum(m_old[sb], jnp.max(scores[sb], axis=-1, keepdims=True)) for sb in sbs]
        p = [jnp.exp(scores[sb] - m_new[sb]) for sb in sbs]
        pv = [pvs[sb](p[sb].astype(BF16)) for sb in sbs]
        corr = [jnp.exp(m_old[sb] - m_new[sb]) for sb in sbs]
        return [(m_new[sb], l_ref[sb] * corr[sb] + jnp.sum(p[sb], axis=-1, keepdims=True),
                 acc_ref[sb] * corr[sb] + pv[sb]) for sb in sbs]

    def commit(new):
        for sb, (m_new, l_new, acc_new) in enumerate(new):
            m_ref[sb] = m_new
            l_ref[sb] = l_new
            acc_ref[sb] = acc_new

    lat_t = [jnp.concatenate([pr[:MLA_KV_RANK, :].astype(BF16) for pr in page_refs[sb * gp:(sb + 1) * gp]], axis=1)
             for sb in range(nb)]
    rk_t = [jnp.concatenate([pr[MLA_KV_RANK:, :].astype(BF16) for pr in page_refs[sb * gp:(sb + 1) * gp]], axis=1)
            for sb in range(nb)]
    scores = [_dot(ql_ref[sb], lat_t[sb]) + _dot(qrr_ref[sb][:, :MLA_ROPE], rk_t[sb]) for sb in range(nb)]
    commit(update(scores, [lambda p, lt=lt: _dot_nt(p, lt) for lt in lat_t]))

    @pl.when(g == pl.num_programs(1) - 1)
    def _():
        tq = lax.broadcasted_iota(jnp.int32, (rows_q, SAMPLE_PAD), 0) % SAMPLE_PAD
        tk = lax.broadcasted_iota(jnp.int32, (rows_q, SAMPLE_PAD), 1)
        visible = (tk <= tq) & (tk < n_valid)
        scores, pvs = [], []
        for sb in range(nb):
            tok = slice(sb * SAMPLE_PAD, (sb + 1) * SAMPLE_PAD)
            lat = rows_ref[tok, :MLA_KV_RANK].astype(BF16)
            rk = rows_ref[tok, MLA_KV_RANK:].astype(BF16)
            s = _dot_nt(ql_ref[sb], lat) + _dot_nt(qrr_ref[sb][:, :MLA_ROPE], rk)
            scores.append(jnp.where(visible, s, -jnp.inf))
            pvs.append(lambda p, lat=lat: _dot(p, lat))
        for sb, (_, l_new, acc_new) in enumerate(update(scores, pvs)):
            tok = slice(sb * SAMPLE_PAD, (sb + 1) * SAMPLE_PAD)
            out_lat = (acc_new / l_new).astype(BF16)
            for h in range(MLA_HEADS):
                w_uv = w_ref[:, h * kvw + MLA_NOPE:(h + 1) * kvw]
                o = _dot(out_lat[h * SAMPLE_PAD:(h + 1) * SAMPLE_PAD, :], w_uv)
                o_ref[tok, h * LANE:(h + 1) * LANE] = o.astype(o_ref.dtype)


def mla_paged_attention(qn, qr, rows, w_ukv, cache_t, page_table, *, n_valid, gp, nb):
    bsz, n_pages = page_table.shape
    nw = MLA_HEADS * LANE
    npg = n_pages // gp
    rq = MLA_HEADS * SAMPLE_PAD

    def page_spec(sb, i):
        return pl.BlockSpec((None, ROW_W, PAGE_SIZE),
                            lambda b, g, pt: (pt[(b * nb + sb) * n_pages + g * gp + i], 0, 0))

    grid_spec = pltpu.PrefetchScalarGridSpec(
        num_scalar_prefetch=1,
        grid=(bsz // nb, npg),
        in_specs=[
            pl.BlockSpec((nb * SAMPLE_PAD, nw), lambda b, g, pt: (b, 0)),
            pl.BlockSpec((nb * SAMPLE_PAD, nw), lambda b, g, pt: (b, 0)),
            pl.BlockSpec((nb * SAMPLE_PAD, ROW_W), lambda b, g, pt: (b, 0)),
            pl.BlockSpec(w_ukv.shape, lambda b, g, pt: (0, 0)),
        ] + [page_spec(sb, i) for sb in range(nb) for i in range(gp)],
        out_specs=pl.BlockSpec((nb * SAMPLE_PAD, nw), lambda b, g, pt: (b, 0)),
        scratch_shapes=[
            pltpu.VMEM((nb, rq, MLA_KV_RANK), BF16),
            pltpu.VMEM((nb, rq, LANE), BF16),
            pltpu.VMEM((nb, rq, 1), F32),
            pltpu.VMEM((nb, rq, 1), F32),
            pltpu.VMEM((nb, rq, MLA_KV_RANK), F32),
        ],
    )
    return pl.pallas_call(
        functools.partial(_paged_body, gp=gp, nb=nb, n_valid=n_valid),
        grid_spec=grid_spec,
        out_shape=jax.ShapeDtypeStruct((bsz * SAMPLE_PAD, nw), BF16),
        compiler_params=_cparams(("parallel", "arbitrary")),
        name="mla_paged_attention",
    )(page_table.reshape(-1), qn, qr, rows, w_ukv, *([cache_t] * (nb * gp)))


def _mlstm_body(q_ref, k_ref, v_ref, og_ref, gc_ref, gr_ref, bc_ref, br_ref, ng_ref, c0_ref, n0_ref, m0_ref,
                h_ref, c_ref, n_ref, m_ref, *, lc, n_valid, nb, single_chunk):
    if single_chunk:
        cs_ref, ns_ref, ms_ref = c0_ref, n0_ref, m0_ref
        m_ref[...] = m0_ref[...]
    else:
        cs_ref, ns_ref, ms_ref = c_ref, n_ref, m_ref

        @pl.when(pl.program_id(1) == 0)
        def _():
            c_ref[...] = c0_ref[...]
            n_ref[...] = n0_ref[...]
            m_ref[...] = m0_ref[...]

    nh = ML_HEADS
    row = lax.broadcasted_iota(jnp.int32, (lc, lc), 0)
    col = lax.broadcasted_iota(jnp.int32, (lc, lc), 1)
    causal = col <= row
    tril = causal.astype(F32)
    triu = (row <= col).astype(F32)
    seqs = []
    for sb in range(nb):
        tok = slice(sb * lc, (sb + 1) * lc)
        gc = gc_ref[tok, :] + bc_ref[...]
        gr = gr_ref[sb] + br_ref[:, 0:1]
        li_c, lf_c = gc[:, 0:nh], _log_sigmoid(gc[:, nh:2 * nh])
        li_r, lf_r = gr[0:nh, :], _log_sigmoid(gr[nh:2 * nh, :])
        if n_valid < lc:
            vc = lax.broadcasted_iota(jnp.int32, (lc, nh), 0) < n_valid
            vr = lax.broadcasted_iota(jnp.int32, (nh, lc), 1) < n_valid
            li_c, lf_c = jnp.where(vc, li_c, NEG_BIG), jnp.where(vc, lf_c, 0.0)
            li_r, lf_r = jnp.where(vr, li_r, NEG_BIG), jnp.where(vr, lf_r, 0.0)
        seqs.append((li_c, li_r, lf_c, lf_r))
    b_cs = [_dot_sel_lhs(tril, s[2]) for s in seqs]
    b_rs = [_dot_sel_rhs(s[3], triu) for s in seqs]

    items = [(sb, h) for sb in range(nb) for h in range(nh)]
    st = []
    for sb, h in items:
        tok = slice(sb * lc, (sb + 1) * lc)
        qh = q_ref[tok, h * ML_DQK:(h + 1) * ML_DQK] * (ML_DQK ** -0.5)
        kh = k_ref[tok, h * ML_DQK:(h + 1) * ML_DQK]
        st.append(dict(
            bc=b_cs[sb][:, h:h + 1], br=b_rs[sb][h:h + 1, :],
            ic=seqs[sb][0][:, h:h + 1], ir=seqs[sb][1][h:h + 1, :],
            m_prev=ms_ref[sb, h:h + 1, 0:1], qh=qh, qb=qh.astype(BF16), kh=kh, kb=kh.astype(BF16),
            vh=v_ref[tok, h * ML_DV:(h + 1) * ML_DV], og=og_ref[tok, h * ML_DV:(h + 1) * ML_DV],
            c=cs_ref[sb, h], n=ns_ref[sb, h:h + 1, :], ng=ng_ref[:, h * ML_DV:(h + 1) * ML_DV]))
    for d in st:
        d["qk"] = _dot_nt(d["qb"], d["kb"])
        d["qc"] = _dot_nt(d["qb"], d["c"].astype(BF16))
    for d in st:
        dmat = jnp.where(causal, d["bc"] - d["br"] + d["ir"], -jnp.inf)
        inter = d["bc"] + d["m_prev"]
        d["m_t"] = jnp.maximum(inter, jnp.max(dmat, axis=-1, keepdims=True))
        d["w_inter"] = jnp.exp(inter - d["m_t"])
        d["s"] = d["qk"] * jnp.exp(dmat - d["m_t"])
        b_last = d["bc"][lc - 1:lc, :]
        g_c = b_last - d["bc"] + d["ic"]
        g_r = b_last - d["br"] + d["ir"]
        d["m_new"] = jnp.maximum(b_last + d["m_prev"], jnp.max(g_r, axis=-1, keepdims=True))
        d["decay"] = jnp.exp(b_last + d["m_prev"] - d["m_new"])
        d["wg"] = jnp.exp(g_c - d["m_new"])
    for d in st:
        d["sv"] = _dot(d["s"].astype(BF16), d["vh"].astype(BF16))
        d["vk"] = _dot_tn((d["vh"] * d["wg"]).astype(BF16), d["kb"])
    for d in st:
        num = d["sv"] + d["w_inter"] * d["qc"]
        den = (jnp.sum(d["s"], axis=-1, keepdims=True)
               + d["w_inter"] * jnp.sum(d["qh"] * d["n"], axis=-1, keepdims=True))
        hout = num / jnp.maximum(jnp.abs(den), jnp.exp(-d["m_t"]))
        d["h_new"] = (_sigmoid(d["og"]) * _rms(hout, d["ng"])).astype(h_ref.dtype)
        d["c_new"] = d["decay"] * d["c"] + d["vk"]
        d["n_new"] = d["decay"] * d["n"] + jnp.sum(d["kh"] * d["wg"], axis=0, keepdims=True)

    for (sb, h), d in zip(items, st):
        h_ref[sb * lc:(sb + 1) * lc, h * ML_DV:(h + 1) * ML_DV] = d["h_new"]
        c_ref[sb, h] = d["c_new"]
        n_ref[sb, h:h + 1, :] = d["n_new"]
        m_ref[sb, h:h + 1, :] = jnp.broadcast_to(d["m_new"], (1, LANE))


def mlstm(p, gates_rows, b_ig, b_fg, norm_g, c0, n0, m0, *, bsz, lc, nchunk, n_valid, nb):
    assert nb == 1 or nchunk == 1
    nh = ML_HEADS
    bias_c = jnp.concatenate([b_ig, b_fg, jnp.zeros((LANE - 2 * nh,), F32)]).reshape(1, LANE)
    bias_r = jnp.broadcast_to(jnp.concatenate([b_ig, b_fg]).reshape(2 * nh, 1), (2 * nh, LANE))
    tl = nb * lc

    def rows(cb):
        return lambda b, c: (b * nchunk + c, cb)

    const2 = lambda b, c: (0, 0)
    return pl.pallas_call(
        functools.partial(_mlstm_body, lc=lc, n_valid=n_valid, nb=nb, single_chunk=nchunk == 1),
        grid=(bsz // nb, nchunk),
        in_specs=[
            pl.BlockSpec((tl, nh * ML_DQK), rows(P_Q // (nh * ML_DQK))),
            pl.BlockSpec((tl, nh * ML_DQK), rows(P_K // (nh * ML_DQK))),
            pl.BlockSpec((tl, nh * ML_DV), rows(P_V // (nh * ML_DV))),
            pl.BlockSpec((tl, nh * ML_DV), rows(P_OG // (nh * ML_DV))),
            pl.BlockSpec((tl, LANE), rows(P_GATE // LANE)),
            pl.BlockSpec((nb, 2 * nh, lc), lambda b, c: (b * nchunk + c, 0, 0)),
            pl.BlockSpec((1, LANE), const2),
            pl.BlockSpec((2 * nh, LANE), const2),
            pl.BlockSpec((1, nh * ML_DV), const2),
            pl.BlockSpec((nb, nh, ML_DV, ML_DQK), lambda b, c: (b, 0, 0, 0)),
            pl.BlockSpec((nb, nh, ML_DQK), lambda b, c: (b, 0, 0)),
            pl.BlockSpec((nb, 2 * nh, LANE), lambda b, c: (b, 0, 0)),
        ],
        out_specs=[
            pl.BlockSpec((tl, nh * ML_DV), lambda b, c: (b * nchunk + c, 0)),
            pl.BlockSpec((nb, nh, ML_DV, ML_DQK), lambda b, c: (b, 0, 0, 0)),
            pl.BlockSpec((nb, nh, ML_DQK), lambda b, c: (b, 0, 0)),
            pl.BlockSpec((nb, 2 * nh, LANE), lambda b, c: (b, 0, 0)),
        ],
        out_shape=[
            jax.ShapeDtypeStruct((bsz * nchunk * lc, nh * ML_DV), BF16),
            jax.ShapeDtypeStruct((bsz, nh, ML_DV, ML_DQK), F32),
            jax.ShapeDtypeStruct((bsz, nh, ML_DQK), F32),
            jax.ShapeDtypeStruct((bsz, 2 * nh, LANE), F32),
        ],
        compiler_params=_cparams(("parallel", "arbitrary")),
        name="mlstm",
    )(p, p, p, p, p, gates_rows, bias_c, bias_r, norm_g.reshape(1, -1), c0, n0, m0)


def _gates_rows(p, lc):
    nrows = p.shape[0]
    g = p[:, P_GATE:P_GATE + 2 * ML_HEADS]
    return g.reshape(nrows // lc, lc, 2 * ML_HEADS).transpose(0, 2, 1)


def _m_pack(m):
    b = m.shape[0]
    mp = jnp.pad(m.astype(F32), ((0, 0), (0, 2 * ML_HEADS - m.shape[1])))
    return jnp.broadcast_to(mp[:, :, None], (b, 2 * ML_HEADS, LANE))


def _ssd_body(z_ref, x_ref, b_ref, c_ref, dtc_ref, dtr_ref, cwx_ref, cwb_ref, cwc_ref, cbx_ref, cbb_ref, cbc_ref,
              csx_ref, csb_ref, csc_ref, pc_ref, pr_ref, dsk_ref, ng_ref, h0_ref,
              y_ref, h_ref, bufx, bufb, bufc, *, lc, n_valid, gs, single_chunk):
    tail = SAMPLE_PAD
    cxs = [slice(gi * SSM_GW, (gi + 1) * SSM_GW) for gi in range(gs)]
    cns = [slice(gi * SSM_STATE, (gi + 1) * SSM_STATE) for gi in range(gs)]

    def init_history():
        for gi in range(gs):
            bufx[gi, 0:tail, :] = csx_ref[:, cxs[gi]]
            bufb[gi, 0:tail, :] = csb_ref[:, cns[gi]]
            bufc[gi, 0:tail, :] = csc_ref[:, cns[gi]]

    if single_chunk:
        init_history()
        hs_ref = h0_ref
    else:
        hs_ref = h_ref

        @pl.when(pl.program_id(2) == 0)
        def _():
            h_ref[...] = h0_ref[...]
            init_history()

    def conv(raw, buf, cw_ref, cb_ref, cols):
        buf[tail:tail + lc, :] = raw
        acc = cb_ref[:, cols] + cw_ref[SSM_CONV - 1:SSM_CONV, cols] * raw
        for j in range(SSM_CONV - 1):
            off = tail - (SSM_CONV - 1) + j
            acc = acc + cw_ref[j:j + 1, cols] * buf[off:off + lc, :]
        if not single_chunk:
            buf[0:tail, :] = raw[lc - tail:lc, :]
        return _silu(acc)

    convs = []
    for gi in range(gs):
        convs.append((conv(x_ref[:, cxs[gi]], bufx.at[gi], cwx_ref, cbx_ref, cxs[gi]),
                      conv(b_ref[:, cns[gi]], bufb.at[gi], cwb_ref, cbb_ref, cns[gi]),
                      conv(c_ref[:, cns[gi]], bufc.at[gi], cwc_ref, cbc_ref, cns[gi])))

    outs = _ssd_groups(
        convs,
        [dict(z=z_ref[:, cxs[gi]], dtc=dtc_ref[:, gi * SSM_HG:(gi + 1) * SSM_HG],
              dtr=dtr_ref[gi * SSM_HG:(gi + 1) * SSM_HG, :], pc=pc_ref[gi], pr=pr_ref[gi],
              dsk=dsk_ref[:, cxs[gi]], ng=ng_ref[:, cxs[gi]], h=hs_ref[gi]) for gi in range(gs)],
        lc=lc, n_valid=n_valid)

    for gi, (y, h_new) in enumerate(outs):
        y_ref[:, cxs[gi]] = y.astype(y_ref.dtype)
        h_ref[gi] = h_new


def _ssd_groups(convs, ins, *, lc, n_valid):
    hg = SSM_HG
    gs = range(len(ins))
    row = lax.broadcasted_iota(jnp.int32, (lc, lc), 0)
    col = lax.broadcasted_iota(jnp.int32, (lc, lc), 1)
    causal = col <= row
    tril = causal.astype(F32)
    triu = (row <= col).astype(F32)
    expand = (lax.broadcasted_iota(jnp.int32, (hg, SSM_GW), 1) // SSM_HEADDIM
              == lax.broadcasted_iota(jnp.int32, (hg, SSM_GW), 0)).astype(F32)
    lane_head = lax.broadcasted_iota(jnp.int32, (lc, LANE), 1) // SSM_HEADDIM

    dt_c, dt_r = [], []
    for g in gs:
        c = _softplus(ins[g]["dtc"] + ins[g]["pc"][0:1, 0:hg])
        r = _softplus(ins[g]["dtr"] + ins[g]["pr"][:, 0:1])
        if n_valid < lc:
            c = jnp.where(lax.broadcasted_iota(jnp.int32, (lc, hg), 0) < n_valid, c, 0.0)
            r = jnp.where(lax.broadcasted_iota(jnp.int32, (hg, lc), 1) < n_valid, r, 0.0)
        dt_c.append(c)
        dt_r.append(r)
    cum_c = [_dot_sel_lhs(tril, dt_c[g] * (-jnp.exp(ins[g]["pc"][1:2, 0:hg]))) for g in gs]
    cum_r = [_dot_sel_rhs(dt_r[g] * (-jnp.exp(ins[g]["pr"][:, 1:2])), triu) for g in gs]
    cb16 = [convs[g][2].astype(BF16) for g in gs]
    bb16 = [convs[g][1].astype(BF16) for g in gs]
    cbm = [_dot_nt(cb16[g], bb16[g]) for g in gs]
    yh = [_dot_nt(cb16[g], ins[g]["h"].astype(BF16)) for g in gs]
    e_in = [_dot_sel_rhs(jnp.exp(cum_c[g]), expand) for g in gs]
    e_st = [_dot_sel_rhs(jnp.exp(cum_c[g][lc - 1:lc, :] - cum_c[g]) * dt_c[g], expand) for g in gs]
    upd = [_dot_tn((convs[g][0] * e_st[g]).astype(BF16), bb16[g]) for g in gs]

    pieces = [[] for _ in gs]
    for pair in range(hg // 2):
        parts = [[] for _ in gs]
        for sub in range(2):
            j = 2 * pair + sub
            for g in gs:
                seg = jnp.where(causal, cum_c[g][:, j:j + 1] - cum_r[g][j:j + 1, :], -jnp.inf)
                w = cbm[g] * jnp.exp(seg) * dt_r[g][j:j + 1, :]
                xm = jnp.where(lane_head == sub, convs[g][0][:, pair * LANE:(pair + 1) * LANE], 0.0)
                parts[g].append(_dot(w.astype(BF16), xm.astype(BF16)))
        for g in gs:
            pieces[g].append(parts[g][0] + parts[g][1])

    outs = []
    for g in gs:
        xa, h_old = convs[g][0], ins[g]["h"]
        y = yh[g] * e_in[g] + jnp.concatenate(pieces[g], axis=1)
        dec_r = jnp.exp(cum_r[g][:, lc - 1:lc])
        h_new = jnp.concatenate(
            [dec_r[j:j + 1, :] * h_old[j * SSM_HEADDIM:(j + 1) * SSM_HEADDIM, :]
             + upd[g][j * SSM_HEADDIM:(j + 1) * SSM_HEADDIM, :] for j in range(hg)], axis=0)
        y = (y + ins[g]["dsk"] * xa) * _silu(ins[g]["z"])
        outs.append((_rms(y, ins[g]["ng"]), h_new))
    return outs


def ssd(zp, dt_rows, conv_w, conv_b, conv_state8, dt_bias, a_log, d_skip, norm_g, h0, *, bsz, lc, nchunk,
        n_valid, gs):
    assert gs == SSM_GROUPS
    g_, hg = SSM_GROUPS // gs, SSM_HG * gs
    gw, ns = SSM_GW * gs, SSM_STATE * gs

    def rows(off, width):
        assert off % width == 0
        return lambda b, g, c: (b * nchunk + c, off // width + g)

    ng_all, nh1 = SSM_GROUPS, SSM_HG
    pcol = jnp.zeros((ng_all, 8, LANE), F32)
    pcol = pcol.at[:, 0, :nh1].set(dt_bias.reshape(ng_all, nh1)).at[:, 1, :nh1].set(a_log.reshape(ng_all, nh1))
    prow = jnp.zeros((ng_all, nh1, LANE), F32)
    prow = prow.at[:, :, 0].set(dt_bias.reshape(ng_all, nh1)).at[:, :, 1].set(a_log.reshape(ng_all, nh1))
    dsk = jnp.repeat(d_skip, SSM_HEADDIM).reshape(1, SSM_D_INNER)
    cb2 = conv_b.reshape(1, -1)
    bo, co = SSM_D_INNER // ns, SSM_D_INNER // ns + g_
    assert SSM_D_INNER % ns == 0

    in_specs = [
        pl.BlockSpec((lc, gw), rows(0, gw)),
        pl.BlockSpec((lc, gw), rows(Z_X, gw)),
        pl.BlockSpec((lc, ns), rows(Z_B, ns)),
        pl.BlockSpec((lc, ns), rows(Z_C, ns)),
        pl.BlockSpec((lc, LANE), rows(Z_DT, LANE)),
        pl.BlockSpec((None, hg, lc), lambda b, g, c: (b * nchunk + c, g, 0)),
        pl.BlockSpec((SSM_CONV, gw), lambda b, g, c: (0, g)),
        pl.BlockSpec((SSM_CONV, ns), lambda b, g, c: (0, bo + g)),
        pl.BlockSpec((SSM_CONV, ns), lambda b, g, c: (0, co + g)),
        pl.BlockSpec((1, gw), lambda b, g, c: (0, g)),
        pl.BlockSpec((1, ns), lambda b, g, c: (0, bo + g)),
        pl.BlockSpec((1, ns), lambda b, g, c: (0, co + g)),
        pl.BlockSpec((None, 8, gw), lambda b, g, c: (b, 0, g)),
        pl.BlockSpec((None, 8, ns), lambda b, g, c: (b, 0, bo + g)),
        pl.BlockSpec((None, 8, ns), lambda b, g, c: (b, 0, co + g)),
        pl.BlockSpec((gs, 8, LANE), lambda b, g, c: (g, 0, 0)),
        pl.BlockSpec((gs, nh1, LANE), lambda b, g, c: (g, 0, 0)),
        pl.BlockSpec((1, gw), lambda b, g, c: (0, g)),
        pl.BlockSpec((1, gw), lambda b, g, c: (0, g)),
        pl.BlockSpec((None, gs, SSM_GW, SSM_STATE), lambda b, g, c: (b, g, 0, 0)),
    ]
    return pl.pallas_call(
        functools.partial(_ssd_body, lc=lc, n_valid=n_valid, gs=gs, single_chunk=nchunk == 1),
        grid=(bsz, g_, nchunk),
        in_specs=in_specs,
        out_specs=[
            pl.BlockSpec((lc, gw), lambda b, g, c: (b * nchunk + c, g)),
            pl.BlockSpec((None, gs, SSM_GW, SSM_STATE), lambda b, g, c: (b, g, 0, 0)),
        ],
        out_shape=[
            jax.ShapeDtypeStruct((bsz * nchunk * lc, SSM_D_INNER), BF16),
            jax.ShapeDtypeStruct((bsz, SSM_GROUPS, SSM_GW, SSM_STATE), F32),
        ],
        scratch_shapes=[
            pltpu.VMEM((gs, lc + SAMPLE_PAD, SSM_GW), F32),
            pltpu.VMEM((gs, lc + SAMPLE_PAD, SSM_STATE), F32),
            pltpu.VMEM((gs, lc + SAMPLE_PAD, SSM_STATE), F32),
        ],
        compiler_params=_cparams(("parallel", "parallel", "arbitrary")),
        name="ssd",
    )(zp, zp, zp, zp, zp, dt_rows, conv_w, conv_w, conv_w, cb2, cb2, cb2, conv_state8, conv_state8, conv_state8,
      pcol, prow, dsk, norm_g.reshape(1, -1), h0)


def _dt_rows(zp, lc):
    nrows = zp.shape[0]
    return zp[:, Z_DT:Z_DT + SSM_HEADS].reshape(nrows // lc, lc, SSM_HEADS).transpose(0, 2, 1)


def _conv_state8(state):
    return jnp.pad(state, ((0, 0), (SAMPLE_PAD - (SSM_CONV - 1), 0), (0, 0)))


def _trunk(x_prompt, x_sample, cache_mla, state_mlstm_C, state_mlstm_n, state_mlstm_m, state_ssm, state_conv,
           page_table, norm_g, ffn_w_gate, ffn_w_up, ffn_w_down, w_in_even, b_ig, b_fg, mlstm_norm_g, mla_q_norm_g,
           mla_kv_norm_g, w_uq, w_ukv, w_out_even, w_in_ssm, conv_w, conv_b, dt_bias, A_log, D_skip, ssm_norm_g,
           w_out_ssm, final_norm_g, *, tm, tf, tf_s, tn, tn_ie, tn_oe, tn_ssm, tmp, lc_ml, nb_ml, lc_ssd, gs_p, gs_s, tq, hp, gp, nb_pg):
    bp, sp, d = x_prompt.shape
    bs, ts, _ = x_sample.shape
    tp = bp * sp
    tsp = bs * SAMPLE_PAD
    xs = [x_prompt.reshape(tp, d), x_sample.reshape(bs * ts, d)]
    tms = [min(tm, tp), min(tm, bs * ts)]
    tmps = [min(tmp, tp), min(tmp, tsp)]
    assert tmps[0] == tq
    tables = [_rope_tables(jnp.tile(jnp.arange(sp), bp)),
              _rope_tables(jnp.tile(PAST_LEN + jnp.arange(SAMPLE_PAD), bs))]

    def pad_seq(a):
        return jnp.pad(a.reshape(bs, ts, a.shape[1]), ((0, 0), (0, SAMPLE_PAD - ts), (0, 0))).reshape(tsp, a.shape[1])

    def unpad_seq(a):
        return a.reshape(bs, SAMPLE_PAD, a.shape[1])[:, :ts].reshape(bs * ts, a.shape[1])

    def ff(xs, g, layer, which, final_g=None):
        return [ffn(x, g, ffn_w_gate, ffn_w_up, ffn_w_down, layer, which, tm=t, tf=f, final_g=final_g)
                for x, t, f in zip(xs, tms, (tf, tf_s))]

    xs = ff(xs, norm_g[0, 0], 0, 0)
    w_in = _prep_w_in_even(w_in_even)
    p_p, p_s = [norm_matmul(x, norm_g[0, 1], w_in, tm=t, tn=tn_ie) for x, t in zip(xs, tms)]
    p_s = pad_seq(p_s)

    zeros = functools.partial(jnp.zeros, dtype=F32)
    hm_p, c_p, n_p, m_p = mlstm(
        p_p, _gates_rows(p_p, lc_ml), b_ig, b_fg, mlstm_norm_g,
        zeros((bp, ML_HEADS, ML_DV, ML_DQK)), zeros((bp, ML_HEADS, ML_DQK)), zeros((bp, 2 * ML_HEADS, LANE)),
        bsz=bp, lc=lc_ml, nchunk=sp // lc_ml, n_valid=lc_ml, nb=1)
    hm_s, c_s, n_s, m_s = mlstm(
        p_s, _gates_rows(p_s, SAMPLE_PAD), b_ig, b_fg, mlstm_norm_g,
        state_mlstm_C, state_mlstm_n, _m_pack(state_mlstm_m),
        bsz=bs, lc=SAMPLE_PAD, nchunk=1, n_valid=ts, nb=nb_ml)

    w_ukv16 = w_ukv.astype(BF16)
    w_q = _prep_w_uq(w_uq)
    (qt_p,) = q_prep(p_p, mla_q_norm_g, w_q, *tables[0], tm=tmps[0], transposed=True)
    qn_s, qr_s = q_prep(p_s, mla_q_norm_g, w_q, *tables[1], tm=tmps[1])
    rows_p, k_p, vt_p = kv_prep(p_p, mla_kv_norm_g, *tables[0], tm=tmps[0], w_ukv=w_ukv16)
    (rows_s,) = kv_prep(p_s, mla_kv_norm_g, *tables[1], tm=tmps[1])
    ha_p = mla_prompt_attention(qt_p, k_p, vt_p, bsz=bp, seq=sp, tq=tq, hp=hp)
    cache_t = jnp.swapaxes(cache_mla, 1, 2)
    ha_s = mla_paged_attention(qn_s, qr_s, rows_s, w_ukv16, cache_t, page_table, n_valid=ts, gp=gp, nb=nb_pg)

    w_oe = w_out_even.astype(BF16)
    xs = [matmul_residual([hm, ha], w_oe, x, tm=t, tn=tn_oe)
          for hm, ha, x, t in zip((hm_p, unpad_seq(hm_s)), (ha_p, unpad_seq(ha_s)), xs, tms)]
    xs = ff(xs, norm_g[0, 2], 0, 1)

    xs = ff(xs, norm_g[1, 0], 1, 0)
    w_is = _prep_w_in_ssm(w_in_ssm)
    z_p, z_s4 = [norm_matmul(x, norm_g[1, 1], w_is, tm=t, tn=tn_ssm) for x, t in zip(xs, tms)]
    z_s = pad_seq(z_s4)
    y_p, ssm_p = ssd(
        z_p, _dt_rows(z_p, lc_ssd), conv_w, conv_b, zeros((bp, SAMPLE_PAD, SSM_CONV_DIM)), dt_bias, A_log,
        D_skip, ssm_norm_g, zeros((bp, SSM_GROUPS, SSM_GW, SSM_STATE)),
        bsz=bp, lc=lc_ssd, nchunk=sp // lc_ssd, n_valid=lc_ssd, gs=gs_p)
    y_s, ssm_s = ssd(
        z_s, _dt_rows(z_s, SAMPLE_PAD), conv_w, conv_b, _conv_state8(state_conv), dt_bias, A_log,
        D_skip, ssm_norm_g, state_ssm.reshape(bs, SSM_GROUPS, SSM_GW, SSM_STATE),
        bsz=bs, lc=SAMPLE_PAD, nchunk=1, n_valid=ts, gs=gs_s)
    w_os = w_out_ssm.astype(BF16)
    xs = [matmul_residual([y], w_os, x, tm=t, tn=tn) for y, x, t in zip((y_p, unpad_seq(y_s)), xs, tms)]
    out_p, out_s = ff(xs, norm_g[1, 2], 1, 1, final_g=final_norm_g)

    keep = SSM_CONV - 1
    assert sp >= keep and ts >= keep
    conv_p = jnp.stack([lax.slice(z_p, (b * sp + sp - keep, Z_X), (b * sp + sp, Z_DT)) for b in range(bp)])
    conv_s = z_s4[:, Z_X:Z_DT].reshape(bs, ts, SSM_CONV_DIM)[:, ts - keep:ts]
    hshape = (SSM_HEADS, SSM_HEADDIM, SSM_STATE)
    return (out_p.reshape(bp, sp, d), out_s.reshape(bs, ts, d), rows_p.reshape(bp, sp, ROW_W),
            rows_s.reshape(bs, SAMPLE_PAD, ROW_W)[:, :ts],
            c_p, c_s, n_p, n_s, m_p[:, :ML_HEADS, 0], m_s[:, :ML_HEADS, 0],
            ssm_p.reshape((bp,) + hshape), ssm_s.reshape((bs,) + hshape), conv_p, conv_s)


def kernel(x_prompt, x_sample, cache_mla, state_mlstm_C, state_mlstm_n, state_mlstm_m, state_ssm, state_conv, page_table, norm_g, ffn_w_gate, ffn_w_up, ffn_w_down, w_in_even, b_ig, b_fg, mlstm_norm_g, mla_q_norm_g, mla_kv_norm_g, w_uq, w_ukv, w_out_even, w_in_ssm, conv_w, conv_b, dt_bias, A_log, D_skip, ssm_norm_g, w_out_ssm, final_norm_g):
    return _trunk(x_prompt, x_sample, cache_mla, state_mlstm_C, state_mlstm_n, state_mlstm_m, state_ssm, state_conv,
                  page_table, norm_g, ffn_w_gate, ffn_w_up, ffn_w_down, w_in_even, b_ig, b_fg, mlstm_norm_g,
                  mla_q_norm_g, mla_kv_norm_g, w_uq, w_ukv, w_out_even, w_in_ssm, conv_w, conv_b, dt_bias, A_log,
                  D_skip, ssm_norm_g, w_out_ssm, final_norm_g,
                  tm=1024, tf=256, tf_s=512, tn=512, tn_ie=1536, tn_oe=1024, tn_ssm=1152, tmp=512, lc_ml=256, nb_ml=4, lc_ssd=128, gs_p=8, gs_s=8,
                  tq=512, hp=4, gp=32, nb_pg=2)
```
